```python
import math
import jax, jax.numpy as jnp
from jax import lax
import numpy as np

D_MODEL = 1024
BATCH = 2
SEQ = 8192
DEPTH = 2
DEC_BATCH = 32
DEC_SEQ = 8
PAST_LEN = 8192
PAGE_SIZE = 128

HEAD_DIM = 64
A_HEADS = 6
B_HEADS = 6
C_GROUPS = 4
A_WIDTH = A_HEADS * HEAD_DIM
B_WIDTH = B_HEADS * HEAD_DIM
C_WIDTH = C_GROUPS * HEAD_DIM
MIX_WIDTH = A_WIDTH + B_WIDTH + C_WIDTH
IDX_HEADS = 8
IDX_DIM = 64
TOPK_MAX = 256
Q_BLOCK = 128
CONV_W = 3
PEER_HEADS = 8
PEER_NKEYS = 128
PEER_N = PEER_NKEYS * PEER_NKEYS
PEER_DKEY = 128
PEER_TOPK = 16
PEER_BLOCK = 256
LN_EPS = 1e-5
DN_ALPHA = (2 * DEPTH) ** 0.25
DN_BETA = (8 * DEPTH) ** -0.25
IN_WIDTHS = (A_WIDTH, A_WIDTH, A_WIDTH, A_HEADS,
             B_WIDTH, B_WIDTH, B_WIDTH, IDX_HEADS * IDX_DIM, IDX_DIM, IDX_HEADS,
             C_WIDTH, C_WIDTH, C_WIDTH)
IN_TOTAL = sum(IN_WIDTHS)

kernel_name = 'hymba_fox_dsa_conv_peer_step'


def _layer_norm(x, g, b):
    xf = x.astype(jnp.float32)
    mu = jnp.mean(xf, axis=-1, keepdims=True)
    var = jnp.mean(jnp.square(xf - mu), axis=-1, keepdims=True)
    return ((xf - mu) * lax.rsqrt(var + LN_EPS) * g.astype(jnp.float32) + b.astype(jnp.float32)).astype(x.dtype)


def _project(x, w_in, b_f):
    B, T, _ = x.shape
    p = jnp.einsum('btd,dp->btp', x, w_in)
    parts = []
    off = 0
    for w in IN_WIDTHS:
        parts.append(p[..., off:off + w])
        off += w
    qa, ka, va, fa, qb, kb, vb, qi, ki, wi, cin, gb, gc = parts
    logf = jax.nn.log_sigmoid(fa.astype(jnp.float32) + b_f.astype(jnp.float32))
    return (qa.reshape(B, T, A_HEADS, HEAD_DIM), ka.reshape(B, T, A_HEADS, HEAD_DIM),
            va.reshape(B, T, A_HEADS, HEAD_DIM), logf,
            qb.reshape(B, T, B_HEADS, HEAD_DIM), kb.reshape(B, T, B_HEADS, HEAD_DIM),
            vb.reshape(B, T, B_HEADS, HEAD_DIM),
            qi.reshape(B, T, IDX_HEADS, IDX_DIM), ki, wi, cin, gb, gc)


def _to_blocks(a):
    B, S = a.shape[:2]
    return jnp.moveaxis(a.reshape((B, S // Q_BLOCK, Q_BLOCK) + a.shape[2:]), 1, 0)


def _from_blocks(o):
    nb, B, qb = o.shape[:3]
    return jnp.moveaxis(o, 0, 1).reshape((B, nb * qb) + o.shape[3:])


def _gather_pages(cache, page_table):
    g = cache[page_table]
    return g.reshape((g.shape[0], g.shape[1] * g.shape[2]) + g.shape[3:])


def _fox_block(q, cq, qpos, k, v, ck_t):
    L = k.shape[1]
    s = jnp.einsum('bqhd,bkhd->bhqk', q, k, preferred_element_type=jnp.float32) * (HEAD_DIM ** -0.5)
    s = s + (jnp.moveaxis(cq, 2, 1)[..., :, None] - ck_t[..., None, :])
    causal = jnp.arange(L)[None, :] <= qpos[:, None]
    s = jnp.where(causal[None, None], s, -jnp.inf)
    p = jax.nn.softmax(s, axis=-1)
    return jnp.einsum('bhqk,bkhd->bqhd', p.astype(v.dtype), v)


def _fox_prompt(q, k, v, logf):
    S = q.shape[1]
    nb = S // Q_BLOCK
    cum = jnp.cumsum(logf, axis=1)
    ck_t = jnp.moveaxis(cum, 2, 1)
    starts = jnp.arange(nb, dtype=jnp.int32) * Q_BLOCK

    def one(args):
        qq, cc, st = args
        return _fox_block(qq, cc, st + jnp.arange(Q_BLOCK, dtype=jnp.int32), k, v, ck_t)

    return _from_blocks(lax.map(one, (_to_blocks(q), _to_blocks(cum), starts)))


def _dsa_block(q, qi, wi, qpos, k, v, ki, topk):
    L = k.shape[1]
    rel = jax.nn.relu(jnp.einsum('bqid,bkd->bqik', qi, ki, preferred_element_type=jnp.float32) * (IDX_DIM ** -0.5))
    score = jnp.einsum('bqi,bqik->bqk', wi.astype(jnp.float32) * (IDX_HEADS ** -0.5), rel)
    admissible = jnp.arange(L)[None, :] <= qpos[:, None]
    score = jnp.where(admissible[None], score, -jnp.inf)
    _, idx = lax.top_k(score, topk)
    ks = jax.vmap(lambda kk, ii: kk[ii])(k, idx)
    vs = jax.vmap(lambda vv, ii: vv[ii])(v, idx)
    s = jnp.einsum('bqhd,bqjhd->bhqj', q, ks, preferred_element_type=jnp.float32) * (HEAD_DIM ** -0.5)
    valid = idx <= qpos[None, :, None]
    s = jnp.where(valid[:, None], s, -jnp.inf)
    p = jax.nn.softmax(s, axis=-1)
    return jnp.einsum('bhqj,bqjhd->bqhd', p.astype(vs.dtype), vs)


def _dsa_prompt(q, qi, wi, k, v, ki):
    S = q.shape[1]
    nb = S // Q_BLOCK
    topk = min(TOPK_MAX, S // 4)
    starts = jnp.arange(nb, dtype=jnp.int32) * Q_BLOCK

    def one(args):
        qq, qqi, ww, st = args
        return _dsa_block(qq, qqi, ww, st + jnp.arange(Q_BLOCK, dtype=jnp.int32), k, v, ki, topk)

    return _from_blocks(lax.map(one, (_to_blocks(q), _to_blocks(qi), _to_blocks(wi), starts)))


def _short_conv(u, buf, w):
    T = u.shape[1]
    up = jnp.concatenate([buf.astype(u.dtype), u], axis=1)
    y = w[0] * up[:, 0:T]
    for j in range(1, CONV_W):
        y = y + w[j] * up[:, j:j + T]
    return y, up[:, T:]


def _merge(oa, ob, oc, w_o):
    B, T = oc.shape[:2]
    cat = jnp.concatenate([oa.reshape(B, T, A_WIDTH), ob.reshape(B, T, B_WIDTH), oc], axis=-1)
    return jnp.einsum('btm,md->btd', cat, w_o)


def _peer(x, w_q, sub_keys, u_tab, v_tab):
    B, T, D = x.shape
    n = B * T
    nb = -(-n // PEER_BLOCK)
    xt = jnp.pad(x.reshape(n, D), ((0, nb * PEER_BLOCK - n), (0, 0)))

    def one(xb):
        q = jnp.einsum('td,dk->tk', xb, w_q).reshape(PEER_BLOCK, PEER_HEADS, 2, PEER_DKEY // 2)
        s = jnp.einsum('thcd,hcnd->thcn', q, sub_keys, preferred_element_type=jnp.float32)
        s1, i1 = lax.top_k(s[:, :, 0], PEER_TOPK)
        s2, i2 = lax.top_k(s[:, :, 1], PEER_TOPK)
        cs = (s1[..., :, None] + s2[..., None, :]).reshape(PEER_BLOCK, PEER_HEADS, PEER_TOPK * PEER_TOPK)
        ci = (i1[..., :, None] * PEER_NKEYS + i2[..., None, :]).reshape(PEER_BLOCK, PEER_HEADS, PEER_TOPK * PEER_TOPK)
        top_s, pos = lax.top_k(cs, PEER_TOPK)
        e = jnp.take_along_axis(ci, pos, axis=-1)
        g = jax.nn.softmax(top_s, axis=-1)
        u = u_tab[e]
        h = jax.nn.gelu(jnp.einsum('td,thkd->thk', xb, u, preferred_element_type=jnp.float32), approximate=False)
        vv = v_tab[e]
        return jnp.einsum('thk,thkd->td', (g * h).astype(vv.dtype), vv)

    out = lax.map(one, xt.reshape(nb, PEER_BLOCK, D))
    return out.reshape(nb * PEER_BLOCK, D)[:n].reshape(B, T, D)


def setup_inputs(seed: int = 0) -> dict:
    key = jax.random.key(seed)
    ks = jax.random.split(key, 24)
    f32 = jnp.float32
    n_pages = PAST_LEN // PAGE_SIZE
    n_used = DEC_BATCH * n_pages
    n_pool = n_used + n_used // 4

    def nrm(k, shape, s=1.0):
        return jax.random.normal(k, shape, f32) * s

    x_prompt = nrm(ks[0], (BATCH, SEQ, D_MODEL))
    x_sample = nrm(ks[1], (DEC_BATCH, DEC_SEQ, D_MODEL))
    cache_a_k = nrm(ks[2], (DEPTH, n_pool, PAGE_SIZE, A_HEADS, HEAD_DIM))
    cache_a_v = nrm(ks[3], (DEPTH, n_pool, PAGE_SIZE, A_HEADS, HEAD_DIM))
    cache_a_logf = jax.nn.log_sigmoid(3.0 + nrm(ks[4], (DEPTH, n_pool, PAGE_SIZE, A_HEADS), 0.5))
    cache_b_k = nrm(ks[5], (DEPTH, n_pool, PAGE_SIZE, B_HEADS, HEAD_DIM))
    cache_b_v = nrm(ks[6], (DEPTH, n_pool, PAGE_SIZE, B_HEADS, HEAD_DIM))
    cache_b_idx_k = nrm(ks[7], (DEPTH, n_pool, PAGE_SIZE, IDX_DIM))
    state_conv = nrm(ks[8], (DEPTH, DEC_BATCH, CONV_W - 1, C_WIDTH))
    page_table = jax.random.permutation(ks[9], n_pool)[:n_used].reshape(DEC_BATCH, n_pages).astype(jnp.int32)
    w_in = nrm(ks[10], (DEPTH, D_MODEL, IN_TOTAL), D_MODEL ** -0.5)
    b_fgate = 3.0 + nrm(ks[11], (DEPTH, A_HEADS), 0.5)
    conv_w = nrm(ks[12], (DEPTH, CONV_W, C_WIDTH), CONV_W ** -0.5)
    w_o = nrm(ks[13], (DEPTH, MIX_WIDTH, D_MODEL), MIX_WIDTH ** -0.5 * DN_BETA)
    ln1_g = 1.0 + nrm(ks[14], (DEPTH, D_MODEL), 0.02)
    ln1_b = nrm(ks[15], (DEPTH, D_MODEL), 0.02)
    peer_wq = nrm(ks[16], (DEPTH, D_MODEL, PEER_HEADS * PEER_DKEY), D_MODEL ** -0.5)
    peer_subkeys = nrm(ks[17], (DEPTH, PEER_HEADS, 2, PEER_NKEYS, PEER_DKEY // 2), (PEER_DKEY // 2) ** -0.5)
    peer_u = nrm(ks[18], (DEPTH, PEER_N, D_MODEL), D_MODEL ** -0.5)
    peer_v = nrm(ks[19], (DEPTH, PEER_N, D_MODEL), DN_BETA * PEER_HEADS ** -0.5)
    ln2_g = 1.0 + nrm(ks[20], (DEPTH, D_MODEL), 0.02)
    ln2_b = nrm(ks[21], (DEPTH, D_MODEL), 0.02)
    return {'x_prompt': x_prompt, 'x_sample': x_sample,
            'cache_a_k': cache_a_k, 'cache_a_v': cache_a_v, 'cache_a_logf': cache_a_logf,
            'cache_b_k': cache_b_k, 'cache_b_v': cache_b_v, 'cache_b_idx_k': cache_b_idx_k,
            'state_conv': state_conv, 'page_table': page_table,
            'w_in': w_in, 'b_fgate': b_fgate, 'conv_w': conv_w, 'w_o': w_o,
            'ln1_g': ln1_g, 'ln1_b': ln1_b, 'peer_wq': peer_wq, 'peer_subkeys': peer_subkeys,
            'peer_u': peer_u, 'peer_v': peer_v, 'ln2_g': ln2_g, 'ln2_b': ln2_b}


def reference(x_prompt, x_sample, cache_a_k, cache_a_v, cache_a_logf, cache_b_k, cache_b_v, cache_b_idx_k,
              state_conv, page_table, w_in, b_fgate, conv_w, w_o, ln1_g, ln1_b,
              peer_wq, peer_subkeys, peer_u, peer_v, ln2_g, ln2_b):
    xp = x_prompt
    xs = x_sample
    Bp = xp.shape[0]
    Ts = xs.shape[1]
    L_s = PAST_LEN + Ts
    topk_s = min(TOPK_MAX, L_s // 4)
    qpos_s = PAST_LEN + jnp.arange(Ts, dtype=jnp.int32)
    p_ak, p_av, p_al, p_bk, p_bv, p_bi, p_cv = [], [], [], [], [], [], []
    s_ak, s_av, s_al, s_bk, s_bv, s_bi, s_cv = [], [], [], [], [], [], []
    for l in range(DEPTH):
        qa, ka, va, lfa, qb, kb, vb, qi, ki, wi, cin, gb, gc = _project(xp, w_in[l], b_fgate[l])
        oa = _fox_prompt(qa, ka, va, lfa)
        ob = _dsa_prompt(qb, qi, wi, kb, vb, ki)
        zero_buf = jnp.zeros((Bp, CONV_W - 1, C_WIDTH), xp.dtype)
        yc, buf_p = _short_conv(gc * cin, zero_buf, conv_w[l])
        mix = _merge(oa, ob, gb * yc, w_o[l])
        xp = _layer_norm(DN_ALPHA * xp + mix, ln1_g[l], ln1_b[l])
        xp = _layer_norm(DN_ALPHA * xp + _peer(xp, peer_wq[l], peer_subkeys[l], peer_u[l], peer_v[l]), ln2_g[l], ln2_b[l])
        p_ak.append(ka); p_av.append(va); p_al.append(lfa)
        p_bk.append(kb); p_bv.append(vb); p_bi.append(ki); p_cv.append(buf_p)

        qa, ka, va, lfa, qb, kb, vb, qi, ki, wi, cin, gb, gc = _project(xs, w_in[l], b_fgate[l])
        ka_all = jnp.concatenate([_gather_pages(cache_a_k[l], page_table).astype(ka.dtype), ka], axis=1)
        va_all = jnp.concatenate([_gather_pages(cache_a_v[l], page_table).astype(va.dtype), va], axis=1)
        lf_all = jnp.concatenate([_gather_pages(cache_a_logf[l], page_table).astype(jnp.float32), lfa], axis=1)
        cum = jnp.cumsum(lf_all, axis=1)
        oa = _fox_block(qa, cum[:, PAST_LEN:], qpos_s, ka_all, va_all, jnp.moveaxis(cum, 2, 1))
        kb_all = jnp.concatenate([_gather_pages(cache_b_k[l], page_table).astype(kb.dtype), kb], axis=1)
        vb_all = jnp.concatenate([_gather_pages(cache_b_v[l], page_table).astype(vb.dtype), vb], axis=1)
        ki_all = jnp.concatenate([_gather_pages(cache_b_idx_k[l], page_table).astype(ki.dtype), ki], axis=1)
        ob = _dsa_block(qb, qi, wi, qpos_s, kb_all, vb_all, ki_all, topk_s)
        yc, buf_s = _short_conv(gc * cin, state_conv[l], conv_w[l])
        mix = _merge(oa, ob, gb * yc, w_o[l])
        xs = _layer_norm(DN_ALPHA * xs + mix, ln1_g[l], ln1_b[l])
        xs = _layer_norm(DN_ALPHA * xs + _peer(xs, peer_wq[l], peer_subkeys[l], peer_u[l], peer_v[l]), ln2_g[l], ln2_b[l])
        s_ak.append(ka); s_av.append(va); s_al.append(lfa)
        s_bk.append(kb); s_bv.append(vb); s_bi.append(ki); s_cv.append(buf_s)

    prompt_a_k = jnp.stack(p_ak); prompt_a_v = jnp.stack(p_av); prompt_a_logf = jnp.stack(p_al)
    prompt_b_k = jnp.stack(p_bk); prompt_b_v = jnp.stack(p_bv); prompt_b_idx_k = jnp.stack(p_bi)
    prompt_conv = jnp.stack(p_cv)
    sample_a_k = jnp.stack(s_ak); sample_a_v = jnp.stack(s_av); sample_a_logf = jnp.stack(s_al)
    sample_b_k = jnp.stack(s_bk); sample_b_v = jnp.stack(s_bv); sample_b_idx_k = jnp.stack(s_bi)
    sample_conv = jnp.stack(s_cv)
    return (xp, xs, prompt_a_k, prompt_a_v, prompt_a_logf, prompt_b_k, prompt_b_v, prompt_b_idx_k, prompt_conv,
            sample_a_k, sample_a_v, sample_a_logf, sample_b_k, sample_b_v, sample_b_idx_k, sample_conv)
```

```python
import functools

import jax
import jax.numpy as jnp
from jax import lax
from jax.experimental import pallas as pl
from jax.experimental.pallas import tpu as pltpu

F32 = jnp.float32
BF16 = jnp.bfloat16
I32 = jnp.int32

D_MODEL = 1024
PAGE_SIZE = 128
HEAD_DIM = 64
A_HEADS = 6
B_HEADS = 6
A_WIDTH = A_HEADS * HEAD_DIM
B_WIDTH = B_HEADS * HEAD_DIM
C_WIDTH = 256
IDX_HEADS = 8
IDX_DIM = 64
TOPK_MAX = 256
CONV_W = 3
PEER_HEADS = 8
PEER_DKEY = 128
PEER_TOPK = 16
LN_EPS = 1e-5
DEPTH = 2
DN_ALPHA = (2 * DEPTH) ** 0.25
IN_WIDTHS = (A_WIDTH, A_WIDTH, A_WIDTH, A_HEADS,
             B_WIDTH, B_WIDTH, B_WIDTH, IDX_HEADS * IDX_DIM, IDX_DIM, IDX_HEADS,
             C_WIDTH, C_WIDTH, C_WIDTH)

LANES = 128
VMEM_LIMIT = 56 * 1024 * 1024
NEG = -1e30
INT_MIN = -2 ** 31

MISC_LF = IDX_DIM
MISC_WI = IDX_DIM + A_HEADS

O_QA, O_KA, O_VA = 0, 384, 768
O_QB, O_KB, O_VB = 1152, 1536, 1920
O_QI = 2304
O_CIN, O_GB, O_GC = 3328, 3584, 3840
O_MISC = 4096
W_TOTAL = 4224


def _nt(a, b):
    return lax.dot_general(a, b, (((1,), (1,)), ((), ())), preferred_element_type=F32)


def _cparams(sem):
    return pltpu.CompilerParams(dimension_semantics=sem, vmem_limit_bytes=VMEM_LIMIT)


def _sort_key(x):
    bits = pltpu.bitcast(x, I32)
    return bits ^ ((bits >> 31) & 0x7FFFFFFF)


def _layer_norm_rows(z, g, b):
    mu = jnp.mean(z, axis=-1, keepdims=True)
    zc = z - mu
    var = jnp.mean(zc * zc, axis=-1, keepdims=True)
    return zc * lax.rsqrt(var + LN_EPS) * g + b


def _proj_kernel(x_ref, w_ref, bf_ref, qa_ref, ka_ref, va_ref, kab_ref, vab_ref,
                 qb_ref, kb_ref, vb_ref, kbb_ref, vbb_ref, qi_ref, u_ref, gb_ref,
                 misc_ref, miscb_ref):
    xb = x_ref[...].astype(BF16)

    def seg(off, width):
        return jnp.dot(xb, w_ref[:, off:off + width], preferred_element_type=F32)

    qa_ref[...] = (seg(O_QA, A_WIDTH) * (HEAD_DIM ** -0.5)).astype(BF16)
    ka = seg(O_KA, A_WIDTH)
    ka_ref[...] = ka
    kab_ref[...] = ka.astype(BF16)
    va = seg(O_VA, A_WIDTH)
    va_ref[...] = va
    vab_ref[...] = va.astype(BF16)
    qb_ref[...] = (seg(O_QB, B_WIDTH) * (HEAD_DIM ** -0.5)).astype(BF16)
    kb = seg(O_KB, B_WIDTH)
    kb_ref[...] = kb
    kbb_ref[...] = kb.astype(BF16)
    vb = seg(O_VB, B_WIDTH)
    vb_ref[...] = vb
    vbb_ref[...] = vb.astype(BF16)
    qi_ref[...] = (seg(O_QI, IDX_HEADS * LANES) * (IDX_DIM ** -0.5)).astype(BF16)
    cin = seg(O_CIN, C_WIDTH)
    gb_ref[...] = seg(O_GB, C_WIDTH)
    gc = seg(O_GC, C_WIDTH)
    u_ref[...] = gc * cin
    misc = seg(O_MISC, LANES)
    lane = lax.broadcasted_iota(I32, misc.shape, 1)
    z = misc + bf_ref[...]
    logsig = jnp.minimum(z, 0.0) - jnp.log1p(jnp.exp(-jnp.abs(z)))
    misc = jnp.where((lane >= MISC_LF) & (lane < MISC_WI), logsig, misc)
    misc_ref[...] = misc
    miscb_ref[...] = misc.astype(BF16)


def _project(x, wcat, bfrow, tm):
    n = x.shape[0]
    row = lambda w: pl.BlockSpec((tm, w), lambda i: (i, 0))
    widths = [(A_WIDTH, BF16), (A_WIDTH, F32), (A_WIDTH, F32), (A_WIDTH, BF16), (A_WIDTH, BF16),
              (B_WIDTH, BF16), (B_WIDTH, F32), (B_WIDTH, F32), (B_WIDTH, BF16), (B_WIDTH, BF16),
              (IDX_HEADS * LANES, BF16), (C_WIDTH, F32), (C_WIDTH, F32), (LANES, F32), (LANES, BF16)]
    names = ["qa", "ka", "va", "kab", "vab", "qb", "kb", "vb", "kbb", "vbb", "qi", "u", "gb", "misc", "miscb"]
    outs = pl.pallas_call(
        _proj_kernel,
        grid=(n // tm,),
        in_specs=[row(D_MODEL),
                  pl.BlockSpec((D_MODEL, W_TOTAL), lambda i: (0, 0)),
                  pl.BlockSpec((1, LANES), lambda i: (0, 0))],
        out_specs=[row(w) for w, _ in widths],
        out_shape=[jax.ShapeDtypeStruct((n, w), dt) for w, dt in widths],
        compiler_params=_cparams(("parallel",)),
        name="proj",
    )(x, wcat, bfrow)
    return dict(zip(names, outs))


def _build_wcat(w_in, b_f):
    parts, off = [], 0
    for w in IN_WIDTHS:
        parts.append(w_in[:, off:off + w])
        off += w
    qa, ka, va, fa, qb, kb, vb, qi, ki, wi, cin, gb, gc = parts
    d = w_in.shape[0]
    qi_pad = jnp.pad(qi.reshape(d, IDX_HEADS, IDX_DIM), ((0, 0), (0, 0), (0, LANES - IDX_DIM)))
    qi_pad = qi_pad.reshape(d, IDX_HEADS * LANES)
    misc = jnp.concatenate([ki, fa, wi, jnp.zeros((d, LANES - MISC_WI - IDX_HEADS), w_in.dtype)], axis=1)
    wcat = jnp.concatenate([qa, ka, va, qb, kb, vb, qi_pad, cin, gb, gc, misc], axis=1).astype(BF16)
    bfrow = jnp.zeros((1, LANES), F32).at[0, MISC_LF:MISC_WI].set(b_f.astype(F32))
    return wcat, bfrow


def _cumsum_kernel(m_ref, cum_ref, cumt_ref, carry_ref, *, tc):
    @pl.when(pl.program_id(1) == 0)
    def _():
        carry_ref[...] = jnp.zeros_like(carry_ref)

    v = m_ref[0]
    r = lax.broadcasted_iota(I32, (tc, tc), 0)
    c = lax.broadcasted_iota(I32, (tc, tc), 1)
    tri = (c <= r).astype(F32)
    loc = jnp.dot(tri, v, precision=lax.Precision.HIGHEST, preferred_element_type=F32) + carry_ref[...]
    cum_ref[0] = loc
    cumt_ref[0] = loc.T
    carry_ref[...] = loc[tc - 1:tc, :]


def _cumsum(misc3, tc):
    b, s, _ = misc3.shape
    return pl.pallas_call(
        functools.partial(_cumsum_kernel, tc=tc),
        grid=(b, s // tc),
        in_specs=[pl.BlockSpec((1, tc, LANES), lambda i, j: (i, j, 0))],
        out_specs=[pl.BlockSpec((1, tc, LANES), lambda i, j: (i, j, 0)),
                   pl.BlockSpec((1, LANES, tc), lambda i, j: (i, 0, j))],
        out_shape=[jax.ShapeDtypeStruct((b, s, LANES), F32), jax.ShapeDtypeStruct((b, LANES, s), F32)],
        scratch_shapes=[pltpu.VMEM((1, LANES), F32)],
        compiler_params=_cparams(("parallel", "arbitrary")),
        name="logf_cumsum",
    )(misc3)


def _flash_update(carry, s, vblk):
    m, l, acc = carry
    m_new = jnp.maximum(m, jnp.max(s, axis=1, keepdims=True))
    alpha = jnp.exp(m - m_new)
    p = jnp.exp(s - m_new)
    l = alpha * l + jnp.sum(p, axis=1, keepdims=True)
    acc = alpha * acc + jnp.dot(p.astype(BF16), vblk, preferred_element_type=F32)
    return m_new, l, acc


def _fox_kernel(q_ref, k_ref, v_ref, cum_ref, cumt_ref, o_ref, *, tq):
    p = pl.program_id(1)
    i = pl.program_id(2)
    qp = q_ref[0]
    lane = lax.broadcasted_iota(I32, (tq, LANES), 1)
    cumblk = cum_ref[0]
    row = lax.broadcasted_iota(I32, (tq, tq), 0)
    col = lax.broadcasted_iota(I32, (tq, tq), 1)
    outs = []
    for hh in range(2):
        half = (lane < HEAD_DIM) if hh == 0 else (lane >= HEAD_DIM)
        qh = jnp.where(half, qp, jnp.zeros_like(qp))
        hidx = 2 * p + hh
        cq = jnp.sum(jnp.where(lane == MISC_LF + hidx, cumblk, 0.0), axis=1, keepdims=True)

        def scores(j):
            off = pl.multiple_of(j * tq, tq)
            kblk = k_ref[0, pl.ds(off, tq), :]
            vblk = v_ref[0, pl.ds(off, tq), :]
            ck = cumt_ref[0, pl.ds(hidx, 1), pl.ds(off, tq)]
            return _nt(qh, kblk) + (cq - ck), vblk

        def body(j, carry):
            s, vblk = scores(j)
            return _flash_update(carry, s, vblk)

        init = (jnp.full((tq, 1), NEG, F32), jnp.zeros((tq, 1), F32), jnp.zeros((tq, LANES), F32))
        carry = lax.fori_loop(0, i, body, init)
        s, vblk = scores(i)
        m, l, acc = _flash_update(carry, jnp.where(col <= row, s, NEG), vblk)
        outs.append(acc / l)
    o_ref[0] = jnp.where(lane < HEAD_DIM, outs[0], outs[1]).astype(BF16)


def _fox_prompt(qa, kab, vab, cum, cumt, tq):
    b, s, _ = qa.shape
    npairs = A_HEADS // 2
    return pl.pallas_call(
        functools.partial(_fox_kernel, tq=tq),
        grid=(b, npairs, s // tq),
        in_specs=[pl.BlockSpec((1, tq, LANES), lambda bb, p, i: (bb, i, p)),
                  pl.BlockSpec((1, s, LANES), lambda bb, p, i: (bb, 0, p)),
                  pl.BlockSpec((1, s, LANES), lambda bb, p, i: (bb, 0, p)),
                  pl.BlockSpec((1, tq, LANES), lambda bb, p, i: (bb, i, 0)),
                  pl.BlockSpec((1, 8, s), lambda bb, p, i: (bb, MISC_LF // 8, 0))],
        out_specs=pl.BlockSpec((1, tq, LANES), lambda bb, p, i: (bb, i, p)),
        out_shape=jax.ShapeDtypeStruct((b, s, A_WIDTH), BF16),
        compiler_params=_cparams(("parallel", "parallel", "arbitrary")),
        name="fox_prompt",
    )(qa, kab, vab, cum, cumt)


def _kth_threshold(count_ge, rows, k):
    def body(it, t):
        cand = t | lax.shift_left(jnp.int32(1), 31 - it)
        cnt = count_ge(cand ^ INT_MIN)
        return jnp.where(cnt >= k, cand, t)
    t = lax.fori_loop(0, 32, body, jnp.zeros((rows, 1), I32))
    return t ^ INT_MIN


def _tie_cutoff(count_tie_below, rows, need, nbits):
    def body(it, j):
        cand = j | lax.shift_left(jnp.int32(1), nbits - 1 - it)
        cnt = count_tie_below(cand)
        return jnp.where(cnt <= need, cand, j)
    return lax.fori_loop(0, nbits, body, jnp.zeros((rows, 1), I32))


def _dsa_kernel(qi_ref, misc_ref, kib_ref, qb_ref, kb_ref, vb_ref, o_ref, key_ref, bias_ref,
                *, tq, kc, topk, nbits):
    i = pl.program_id(1)
    nch = (i * tq + tq + kc - 1) // kc
    ngrp = kc // LANES
    w8 = misc_ref[0][:, MISC_WI:MISC_WI + IDX_HEADS] * (IDX_HEADS ** -0.5)
    row_g = i * tq + lax.broadcasted_iota(I32, (tq, kc), 0)
    col_l = lax.broadcasted_iota(I32, (tq, kc), 1)

    def p1(c, carry):
        off = pl.multiple_of(c * kc, kc)
        kib = kib_ref[0, pl.ds(off, kc), :]
        acc = jnp.zeros((tq, kc), F32)
        for h in range(IDX_HEADS):
            r = _nt(qi_ref[0, :, h * LANES:(h + 1) * LANES], kib)
            acc = acc + w8[:, h:h + 1] * jnp.maximum(r, 0.0)
        acc = jnp.where(acc == 0.0, 0.0, acc)
        sc = jnp.where(off + col_l <= row_g, acc, -jnp.inf)
        key_ref[:, pl.ds(off, kc)] = _sort_key(sc)
        return carry

    lax.fori_loop(0, nch, p1, 0)

    def lane_groups(x):
        tot = x[:, 0:LANES]
        for g in range(1, ngrp):
            tot = tot + x[:, g * LANES:(g + 1) * LANES]
        return tot

    def count(pred):
        def body(c, cnt):
            off = pl.multiple_of(c * kc, kc)
            return cnt + lane_groups(pred(key_ref[:, pl.ds(off, kc)], off).astype(I32))
        cnt = lax.fori_loop(0, nch, body, jnp.zeros((tq, LANES), I32))
        return jnp.sum(cnt, axis=1, keepdims=True)

    thr = _kth_threshold(lambda t: count(lambda kk, off: kk >= t), tq, topk)
    n_gt = count(lambda kk, off: kk > thr)
    n_ge = count(lambda kk, off: kk >= thr)
    need = topk - n_gt
    big = jnp.full((tq, 1), 2 ** nbits - 1, I32)
    excess = jnp.max(jnp.where(n_ge > topk, 1, 0))
    jstar = lax.cond(
        excess > 0,
        lambda: _tie_cutoff(lambda j: count(lambda kk, off: (kk == thr) & (off + col_l < j)), tq, need, nbits),
        lambda: big)

    def p2(c, carry):
        off = pl.multiple_of(c * kc, kc)
        kk = key_ref[:, pl.ds(off, kc)]
        colg = off + col_l
        sel = (kk > thr) | ((kk == thr) & (colg < jstar))
        bias_ref[:, pl.ds(off, kc)] = jnp.where(sel & (colg <= row_g), 0.0, NEG)
        return carry

    lax.fori_loop(0, nch, p2, 0)

    lane = lax.broadcasted_iota(I32, (tq, LANES), 1)
    for pr in range(B_HEADS // 2):
        qp = qb_ref[0, :, pr * LANES:(pr + 1) * LANES]
        outs = []
        for hh in range(2):
            half = (lane < HEAD_DIM) if hh == 0 else (lane >= HEAD_DIM)
            qh = jnp.where(half, qp, jnp.zeros_like(qp))

            def body(c, carry):
                off = pl.multiple_of(c * kc, kc)
                kblk = kb_ref[0, pl.ds(off, kc), pr * LANES:(pr + 1) * LANES]
                vblk = vb_ref[0, pl.ds(off, kc), pr * LANES:(pr + 1) * LANES]
                s = _nt(qh, kblk) + bias_ref[:, pl.ds(off, kc)]
                return _flash_update(carry, s, vblk)

            init = (jnp.full((tq, 1), NEG, F32), jnp.zeros((tq, 1), F32), jnp.zeros((tq, LANES), F32))
            m, l, acc = lax.fori_loop(0, nch, body, init)
            outs.append(acc / l)
        o_ref[0, :, pr * LANES:(pr + 1) * LANES] = jnp.where(lane < HEAD_DIM, outs[0], outs[1]).astype(BF16)


def _dsa_prompt(qi, misc, miscb, qb, kbb, vbb, tq, kc, topk):
    b, s, _ = qb.shape
    nbits = max(1, (s - 1).bit_length()) + 1
    qspec = lambda w: pl.BlockSpec((1, tq, w), lambda bb, i: (bb, i, 0))
    full = lambda w: pl.BlockSpec((1, s, w), lambda bb, i: (bb, 0, 0))
    return pl.pallas_call(
        functools.partial(_dsa_kernel, tq=tq, kc=kc, topk=topk, nbits=nbits),
        grid=(b, s // tq),
        in_specs=[qspec(IDX_HEADS * LANES), qspec(LANES), full(LANES), qspec(B_WIDTH), full(B_WIDTH), full(B_WIDTH)],
        out_specs=qspec(B_WIDTH),
        out_shape=jax.ShapeDtypeStruct((b, s, B_WIDTH), BF16),
        scratch_shapes=[pltpu.VMEM((tq, s), I32), pltpu.VMEM((tq, s), F32)],
        compiler_params=_cparams(("parallel", "arbitrary")),
        name="dsa_prompt",
    )(qi, misc, miscb, qb, kbb, vbb)


def _sidx_kernel(pt_ref, qi_ref, misc_ref, *rest, pps, nps, past, topk, nbits):
    page_refs = rest[:pps]
    knew_ref, bias_ref, key_ref = rest[pps:]
    j = pl.program_id(1)
    t = DEC_T = qi_ref.shape[1]
    lp = key_ref.shape[1]
    w8 = misc_ref[0][:, MISC_WI:MISC_WI + IDX_HEADS] * (IDX_HEADS ** -0.5)

    def score(kpage):
        kb = kpage.astype(BF16)
        acc = jnp.zeros((t, PAGE_SIZE), F32)
        for h in range(IDX_HEADS):
            r = _nt(qi_ref[0, :, h * LANES:h * LANES + IDX_DIM], kb)
            acc = acc + w8[:, h:h + 1] * jnp.maximum(r, 0.0)
        return jnp.where(acc == 0.0, 0.0, acc)

    @pl.when(j < nps)
    def _():
        for k in range(pps):
            off = pl.multiple_of((j * pps + k) * PAGE_SIZE, PAGE_SIZE)
            key_ref[:, pl.ds(off, PAGE_SIZE)] = _sort_key(score(page_refs[k][0, 0]))

    @pl.when(j == nps)
    def _():
        rowi = lax.broadcasted_iota(I32, (t, PAGE_SIZE), 0)
        coli = lax.broadcasted_iota(I32, (t, PAGE_SIZE), 1)
        sc = jnp.where(coli <= rowi, score(knew_ref[0]), -jnp.inf)
        key_ref[:, past:past + PAGE_SIZE] = _sort_key(sc)

        keys = key_ref[...]
        colg = lax.broadcasted_iota(I32, (t, lp), 1)
        count = lambda pred: jnp.sum(pred.astype(I32), axis=1, keepdims=True)
        thr = _kth_threshold(lambda th: count(keys >= th), t, topk)
        need = topk - count(keys > thr)
        jstar = _tie_cutoff(lambda jj: count((keys == thr) & (colg < jj)), t, need, nbits)
        sel = (keys > thr) | ((keys == thr) & (colg < jstar))
        rowg = past + lax.broadcasted_iota(I32, (t, lp), 0)
        bias_ref[0] = jnp.where(sel & (colg <= rowg), 0.0, NEG)


def _sample_index(page_table, qi3, misc3, cache_idx, layer, knew_pad, pps, topk):
    nb, t, _ = qi3.shape
    npg = page_table.shape[1]
    nps = npg // pps
    past = npg * PAGE_SIZE
    lp = past + PAGE_SIZE
    nbits = lp.bit_length() + 1

    def page_map(k):
        return lambda b, j, pt: (layer, pt[b * npg + jnp.minimum(j, nps - 1) * pps + k], 0, 0)

    bmap = lambda b, j, pt: (b, 0, 0)
    grid_spec = pltpu.PrefetchScalarGridSpec(
        num_scalar_prefetch=1,
        grid=(nb, nps + 1),
        in_specs=[pl.BlockSpec((1, t, IDX_HEADS * LANES), bmap),
                  pl.BlockSpec((1, t, LANES), bmap)]
                 + [pl.BlockSpec((1, 1, PAGE_SIZE, IDX_DIM), page_map(k)) for k in range(pps)]
                 + [pl.BlockSpec((1, PAGE_SIZE, IDX_DIM), bmap)],
        out_specs=pl.BlockSpec((1, t, lp), bmap),
        scratch_shapes=[pltpu.VMEM((t, lp), I32)],
    )
    return pl.pallas_call(
        functools.partial(_sidx_kernel, pps=pps, nps=nps, past=past, topk=topk, nbits=nbits),
        grid_spec=grid_spec,
        out_shape=jax.ShapeDtypeStruct((nb, t, lp), F32),
        compiler_params=_cparams(("parallel", "arbitrary")),
        name="sample_index",
    )(page_table.reshape(-1), qi3, misc3, *([cache_idx] * pps), knew_pad)


def _sattn_kernel(pt_ref, qa_ref, qb_ref, bias_ref, *rest, pps, nps):
    grp = lambda g: rest[g * pps:(g + 1) * pps]
    ak_refs, av_refs, lf_refs, bk_refs, bv_refs = (grp(g) for g in range(5))
    (akn_ref, avn_ref, lfn_ref, bkn_ref, bvn_ref, oa_ref, ob_ref,
     qbd_a, qbd_b, ma, la, acca, mb, lb, accb, carry_ref) = rest[5 * pps:]
    j = pl.program_id(1)
    t = qa_ref.shape[1]
    rows = A_HEADS * t
    lane_q = lax.broadcasted_iota(I32, (t, A_WIDTH), 1)

    def block_diag(q):
        qf = q.astype(F32)
        parts = [jnp.where((lane_q >= h * HEAD_DIM) & (lane_q < (h + 1) * HEAD_DIM), qf, 0.0)
                 for h in range(A_HEADS)]
        return jnp.concatenate(parts, axis=0).astype(BF16)

    @pl.when(j == 0)
    def _():
        qbd_a[...] = block_diag(qa_ref[0])
        qbd_b[...] = block_diag(qb_ref[0])
        for r in (ma, mb):
            r[...] = jnp.full(r.shape, NEG, F32)
        for r in (la, lb, acca, accb, carry_ref):
            r[...] = jnp.zeros(r.shape, F32)

    r_i = lax.broadcasted_iota(I32, (PAGE_SIZE, PAGE_SIZE), 0)
    c_i = lax.broadcasted_iota(I32, (PAGE_SIZE, PAGE_SIZE), 1)
    triu = (r_i <= c_i).astype(F32)

    def rep_heads(x8):
        return jnp.concatenate([jnp.broadcast_to(x8[h:h + 1, :], (t, PAGE_SIZE)) for h in range(A_HEADS)], axis=0)

    def update(m_ref, l_ref, acc_ref, s, v):
        m, l, acc = _flash_update((m_ref[...], l_ref[...], acc_ref[...]), s, v.astype(BF16))
        m_ref[...] = m
        l_ref[...] = l
        acc_ref[...] = acc

    def process(ak, av, lft, bk, bv, bias8, mask):
        cl = jnp.dot(lft, triu, precision=lax.Precision.HIGHEST, preferred_element_type=F32) + carry_ref[...]
        carry_ref[...] = jnp.broadcast_to(cl[:, PAGE_SIZE - 1:PAGE_SIZE], cl.shape)
        sa = _nt(qbd_a[...], ak.astype(BF16)) - rep_heads(cl)
        if mask is not None:
            sa = jnp.where(mask, sa, NEG)
        update(ma, la, acca, sa, av)
        sb = _nt(qbd_b[...], bk.astype(BF16)) + jnp.concatenate([bias8] * B_HEADS, axis=0)
        update(mb, lb, accb, sb, bv)

    @pl.when(j < nps)
    def _():
        for k in range(pps):
            off = pl.multiple_of((j * pps + k) * PAGE_SIZE, PAGE_SIZE)
            process(ak_refs[k][0, 0], av_refs[k][0, 0], lf_refs[k][0, 0], bk_refs[k][0, 0], bv_refs[k][0, 0],
                    bias_ref[0, :, pl.ds(off, PAGE_SIZE)], None)

    @pl.when(j == nps)
    def _():
        past = nps * pps * PAGE_SIZE
        rowi = lax.broadcasted_iota(I32, (rows, PAGE_SIZE), 0)
        coli = lax.broadcasted_iota(I32, (rows, PAGE_SIZE), 1)
        mask = coli <= (rowi & (t - 1))
        process(akn_ref[0], avn_ref[0], lfn_ref[0], bkn_ref[0], bvn_ref[0],
                bias_ref[0, :, past:past + PAGE_SIZE], mask)

        def gather_heads(acc_ref, l_ref):
            o = jnp.zeros((t, A_WIDTH), F32)
            for h in range(A_HEADS):
                blk = acc_ref[h * t:(h + 1) * t, :] / l_ref[h * t:(h + 1) * t, :]
                o = jnp.where((lane_q >= h * HEAD_DIM) & (lane_q < (h + 1) * HEAD_DIM), blk, o)
            return o.astype(BF16)

        oa_ref[0] = gather_heads(acca, la)
        ob_ref[0] = gather_heads(accb, lb)


def _sample_attn(page_table, qa3, qb3, bias, caches, layer, news, pps):
    nb, t, _ = qa3.shape
    assert t & (t - 1) == 0
    npg = page_table.shape[1]
    nps = npg // pps
    lp = bias.shape[2]
    rows = A_HEADS * t

    def page_map(k):
        return lambda b, j, pt: (layer, pt[b * npg + jnp.minimum(j, nps - 1) * pps + k], 0, 0)

    bmap = lambda b, j, pt: (b, 0, 0)
    cache_specs, cache_args = [], []
    for c in caches:
        blk = (1, 1) + c.shape[2:]
        for k in range(pps):
            cache_specs.append(pl.BlockSpec(blk, page_map(k)))
            cache_args.append(c)
    new_specs = [pl.BlockSpec((1,) + a.shape[1:], bmap) for a in news]
    grid_spec = pltpu.PrefetchScalarGridSpec(
        num_scalar_prefetch=1,
        grid=(nb, nps + 1),
        in_specs=[pl.BlockSpec((1, t, A_WIDTH), bmap), pl.BlockSpec((1, t, B_WIDTH), bmap),
                  pl.BlockSpec((1, t, lp), bmap)] + cache_specs + new_specs,
        out_specs=[pl.BlockSpec((1, t, A_WIDTH), bmap), pl.BlockSpec((1, t, B_WIDTH), bmap)],
        scratch_shapes=[pltpu.VMEM((rows, A_WIDTH), BF16), pltpu.VMEM((rows, B_WIDTH), BF16),
                        pltpu.VMEM((rows, 1), F32), pltpu.VMEM((rows, 1), F32), pltpu.VMEM((rows, A_WIDTH), F32),
                        pltpu.VMEM((rows, 1), F32), pltpu.VMEM((rows, 1), F32), pltpu.VMEM((rows, B_WIDTH), F32),
                        pltpu.VMEM((8, PAGE_SIZE), F32)],
    )
    return pl.pallas_call(
        functools.partial(_sattn_kernel, pps=pps, nps=nps),
        grid_spec=grid_spec,
        out_shape=[jax.ShapeDtypeStruct((nb, t, A_WIDTH), BF16), jax.ShapeDtypeStruct((nb, t, B_WIDTH), BF16)],
        compiler_params=_cparams(("parallel", "arbitrary")),
        name="sample_attn",
    )(page_table.reshape(-1), qa3, qb3, bias, *cache_args, *news)


def _merge_kernel(oa_ref, ob_ref, u_ref, gb_ref, x_ref, buf_ref, cw_ref, wo_ref, g_ref, b_ref,
                  y_ref, nbuf_ref, carry_ref, *, tm):
    j = pl.program_id(1)

    @pl.when(j == 0)
    def _():
        carry_ref[...] = buf_ref[0]

    u = u_ref[0]
    rowi = lax.broadcasted_iota(I32, u.shape, 0)
    c0 = carry_ref[0:1, :]
    c1 = carry_ref[1:2, :]
    u1 = jnp.where(rowi == 0, c1, pltpu.roll(u, 1, 0))
    u2 = jnp.where(rowi == 0, c0, jnp.where(rowi == 1, c1, pltpu.roll(u, 2, 0)))
    yc = cw_ref[0:1, :] * u2 + cw_ref[1:2, :] * u1 + cw_ref[2:3, :] * u
    oc = (gb_ref[0] * yc).astype(BF16)
    new_carry = u[tm - 2:tm, :]
    carry_ref[...] = new_carry
    nbuf_ref[0] = new_carry
    mix = jnp.dot(oa_ref[0], wo_ref[0:A_WIDTH, :], preferred_element_type=F32)
    mix = mix + jnp.dot(ob_ref[0], wo_ref[A_WIDTH:A_WIDTH + B_WIDTH, :], preferred_element_type=F32)
    mix = mix + jnp.dot(oc, wo_ref[A_WIDTH + B_WIDTH:, :], preferred_element_type=F32)
    y_ref[0] = _layer_norm_rows(DN_ALPHA * x_ref[0] + mix, g_ref[...], b_ref[...])


def _merge(oa, ob, u, gb, x, buf, conv_w, wo_bf, g, b, tm):
    nseq, t, _ = x.shape
    blk = lambda w: pl.BlockSpec((1, tm, w), lambda s, j: (s, j, 0))
    const = lambda r, w: pl.BlockSpec((r, w), lambda s, j: (0, 0))
    return pl.pallas_call(
        functools.partial(_merge_kernel, tm=tm),
        grid=(nseq, t // tm),
        in_specs=[blk(A_WIDTH), blk(B_WIDTH), blk(C_WIDTH), blk(C_WIDTH), blk(D_MODEL),
                  pl.BlockSpec((1, CONV_W - 1, C_WIDTH), lambda s, j: (s, 0, 0)),
                  const(CONV_W, C_WIDTH), const(D_MODEL, D_MODEL), const(1, D_MODEL), const(1, D_MODEL)],
        out_specs=[blk(D_MODEL), pl.BlockSpec((1, CONV_W - 1, C_WIDTH), lambda s, j: (s, 0, 0))],
        out_shape=[jax.ShapeDtypeStruct((nseq, t, D_MODEL), F32),
                   jax.ShapeDtypeStruct((nseq, CONV_W - 1, C_WIDTH), F32)],
        scratch_shapes=[pltpu.VMEM((CONV_W - 1, C_WIDTH), F32)],
        compiler_params=_cparams(("parallel", "arbitrary")),
        name="merge_ln",
    )(oa, ob, u, gb, x, buf, conv_w, wo_bf, g, b)


def _top16_rows(s):
    nk, tm = s.shape
    idx = lax.broadcasted_iota(I32, (nk, tm), 0).astype(F32)
    work = s
    rank = jnp.full((nk, tm), 99.0, F32)
    vals = []
    for r in range(PEER_TOPK):
        m = jnp.max(work, axis=0, keepdims=True)
        first = jnp.min(jnp.where(work == m, idx, 1e9), axis=0, keepdims=True)
        hit = idx == first
        rank = jnp.where(hit, float(r), rank)
        work = jnp.where(hit, -jnp.inf, work)
        vals.append(m)
    return jnp.concatenate(vals, axis=0), rank


def _peer_kernel(x_ref, wqt_ref, sk_ref, u_ref, vt_ref, g_ref, b_ref, o_ref,
                 xt_s, outt_s, a_s, n1_s, bb_s, r2_s, *, tm, ec, nk):
    e = pl.program_id(1)
    ne = pl.num_programs(1)
    k = PEER_TOPK
    half = PEER_DKEY // 2

    @pl.when(e == 0)
    def _():
        xt = x_ref[...].T.astype(BF16)
        xt_s[...] = xt
        qt = jnp.dot(wqt_ref[...], xt, preferred_element_type=F32).astype(BF16)
        rho = lax.broadcasted_iota(I32, (80, tm), 0)
        mid = rho - 16
        r1 = jnp.where(rho < 16, rho, jnp.where(rho < 72, mid & 7, 0))
        r2 = jnp.where(rho < 16, 0, jnp.where(rho < 72, (mid >> 3) + 1, rho - 64))
        pos = (r1 * k + r2).astype(F32)
        valid = (r1 + 1) * (r2 + 1) <= k
        row8 = lax.broadcasted_iota(I32, (8, tm), 0)
        for h in range(PEER_HEADS):
            s1 = jnp.dot(sk_ref[h, 0], qt[(2 * h) * half:(2 * h + 1) * half, :], preferred_element_type=F32)
            s2 = jnp.dot(sk_ref[h, 1], qt[(2 * h + 1) * half:(2 * h + 2) * half, :], preferred_element_type=F32)
            v1, rank1 = _top16_rows(s1)
            v2, rank2 = _top16_rows(s2)
            slabs = [v1 + v2[0:1, :]]
            for j in range(1, 8):
                slabs.append(v1[0:8, :] + v2[j:j + 1, :])
            slabs.append(v1[0:1, :] + v2[8:16, :])
            cand = jnp.where(valid, jnp.concatenate(slabs, axis=0), -jnp.inf)
            cmax = v1[0:1, :] + v2[0:1, :]
            sel = jnp.zeros((80, tm), F32)
            z = jnp.zeros((1, tm), F32)
            for r in range(k):
                m = jnp.max(cand, axis=0, keepdims=True)
                first = jnp.min(jnp.where(cand == m, pos, 1e9), axis=0, keepdims=True)
                hit = pos == first
                sel = jnp.where(hit, 1.0, sel)
                cand = jnp.where(hit, -jnp.inf, cand)
                z = z + jnp.exp(m - cmax)
            top8 = sel[0:8, :]
            for j in range(1, 8):
                top8 = top8 + sel[8 + 8 * j:16 + 8 * j, :]
            extra = jnp.sum(sel[72:80, :], axis=0, keepdims=True)
            top8 = top8 + jnp.where(row8 == 0, extra, 0.0)
            ncount = jnp.concatenate([top8, sel[8:16, :]], axis=0)
            n1 = jnp.zeros((nk, tm), F32)
            for r in range(k):
                n1 = jnp.where(rank1 == float(r), ncount[r:r + 1, :], n1)
            a_s[h] = jnp.exp(s1 - v1[0:1, :])
            n1_s[h] = n1
            bb_s[h] = jnp.exp(s2 - v2[0:1, :]) / z
            r2_s[h] = rank2
        outt_s[...] = jnp.zeros_like(outt_s)

    ht = jnp.dot(u_ref[...], xt_s[...], preferred_element_type=F32)
    act = 0.5 * ht * (1.0 + lax.erf(ht * (0.5 ** 0.5)))
    per = ec // nk
    coefs = []
    for g in range(per):
        i1 = e * per + g
        gate = jnp.zeros((nk, tm), F32)
        for h in range(PEER_HEADS):
            arow = a_s[h, pl.ds(i1, 1), :]
            nrow = n1_s[h, pl.ds(i1, 1), :]
            gate = gate + jnp.where(r2_s[h] < nrow, bb_s[h], 0.0) * arow
        coefs.append((gate * act[g * nk:(g + 1) * nk, :]).astype(BF16))
    coef = jnp.concatenate(coefs, axis=0) if per > 1 else coefs[0]
    outt_s[...] += jnp.dot(vt_ref[...], coef, preferred_element_type=F32)

    @pl.when(e == ne - 1)
    def _():
        y = outt_s[...].T
        o_ref[...] = _layer_norm_rows(DN_ALPHA * x_ref[...] + y, g_ref[...], b_ref[...])


def _peer(x, wqt, sk, u_bf, vt_bf, g, b, tm, ec):
    n = x.shape[0]
    nk = sk.shape[2]
    ne = (nk * nk) // ec
    return pl.pallas_call(
        functools.partial(_peer_kernel, tm=tm, ec=ec, nk=nk),
        grid=(n // tm, ne),
        in_specs=[pl.BlockSpec((tm, D_MODEL), lambda i, e: (i, 0)),
                  pl.BlockSpec((D_MODEL, D_MODEL), lambda i, e: (0, 0)),
                  pl.BlockSpec(sk.shape, lambda i, e: (0, 0, 0, 0)),
                  pl.BlockSpec((ec, D_MODEL), lambda i, e: (e, 0)),
                  pl.BlockSpec((D_MODEL, ec), lambda i, e: (0, e)),
                  pl.BlockSpec((1, D_MODEL), lambda i, e: (0, 0)),
                  pl.BlockSpec((1, D_MODEL), lambda i, e: (0, 0))],
        out_specs=pl.BlockSpec((tm, D_MODEL), lambda i, e: (i, 0)),
        out_shape=jax.ShapeDtypeStruct((n, D_MODEL), F32),
        scratch_shapes=[pltpu.VMEM((D_MODEL, tm), BF16), pltpu.VMEM((D_MODEL, tm), F32)]
                       + [pltpu.VMEM((PEER_HEADS, nk, tm), F32)] * 4,
        compiler_params=_cparams(("parallel", "arbitrary")),
        name="peer_ln",
    )(x, wqt, sk, u_bf, vt_bf, g, b)


def _pick(n, pref):
    t = min(n, pref)
    while n % t:
        t //= 2
    return t


def kernel(x_prompt, x_sample, cache_a_k, cache_a_v, cache_a_logf, cache_b_k, cache_b_v, cache_b_idx_k, state_conv, page_table, w_in, b_fgate, conv_w, w_o, ln1_g, ln1_b, peer_wq, peer_subkeys, peer_u, peer_v, ln2_g, ln2_b):
    bp, s, d = x_prompt.shape
    nb, t, _ = x_sample.shape
    depth = w_in.shape[0]
    npool = cache_a_k.shape[1]
    npg = page_table.shape[1]
    past = npg * PAGE_SIZE
    nk = peer_subkeys.shape[3]
    topk_p = min(TOPK_MAX, s // 4)
    topk_s = min(TOPK_MAX, (past + t) // 4)

    n_p, n_s = bp * s, nb * t
    tm_p = _pick(n_p, 512)
    tm_s = _pick(n_s, 256)
    tq_fox = _pick(s, 256)
    tq_dsa = _pick(s, 128)
    kc_dsa = _pick(s, 512)
    tc = _pick(s, 256)
    tm_merge = _pick(s, 512)
    tm_peer_p = _pick(n_p, 256)
    tm_peer_s = _pick(n_s, 256)
    ec = _pick(nk * nk, max(1024, nk))
    pps_i = _pick(npg, 8)
    pps_a = _pick(npg, 4)

    cak = cache_a_k.reshape(depth, npool, PAGE_SIZE, A_WIDTH)
    cav = cache_a_v.reshape(depth, npool, PAGE_SIZE, A_WIDTH)
    cbk = cache_b_k.reshape(depth, npool, PAGE_SIZE, B_WIDTH)
    cbv = cache_b_v.reshape(depth, npool, PAGE_SIZE, B_WIDTH)
    clf = jnp.pad(jnp.swapaxes(cache_a_logf, 2, 3), ((0, 0), (0, 0), (0, 8 - A_HEADS), (0, 0)))

    xp = x_prompt.reshape(n_p, d)
    xs = x_sample.reshape(n_s, d)
    outs_p = [[] for _ in range(7)]
    outs_s = [[] for _ in range(7)]
    zero_buf = jnp.zeros((bp, CONV_W - 1, C_WIDTH), F32)
    row2 = lambda v: v.reshape(1, -1).astype(F32)

    for l in range(depth):
        wcat, bfrow = _build_wcat(w_in[l], b_fgate[l])
        wo_bf = w_o[l].astype(BF16)
        wqt = peer_wq[l].T.astype(BF16)
        sk = peer_subkeys[l].astype(BF16)
        u_bf = peer_u[l].astype(BF16)
        vt_bf = peer_v[l].T.astype(BF16)
        g1, b1, g2, b2 = row2(ln1_g[l]), row2(ln1_b[l]), row2(ln2_g[l]), row2(ln2_b[l])
        cw = conv_w[l].astype(F32)

        pr = _project(xp, wcat, bfrow, tm_p)
        r3 = lambda a: a.reshape(bp, s, a.shape[-1])
        cum, cumt = _cumsum(r3(pr["misc"]), tc)
        oa = _fox_prompt(r3(pr["qa"]), r3(pr["kab"]), r3(pr["vab"]), cum, cumt, tq_fox)
        ob = _dsa_prompt(r3(pr["qi"]), r3(pr["misc"]), r3(pr["miscb"]), r3(pr["qb"]), r3(pr["kbb"]),
                         r3(pr["vbb"]), tq_dsa, kc_dsa, topk_p)
        x1, buf_p = _merge(oa, ob, r3(pr["u"]), r3(pr["gb"]), r3(xp), zero_buf, cw, wo_bf, g1, b1, tm_merge)
        xp = _peer(x1.reshape(n_p, d), wqt, sk, u_bf, vt_bf, g2, b2, tm_peer_p, ec)
        for lst, val in zip(outs_p, (pr["ka"].reshape(bp, s, A_HEADS, HEAD_DIM), pr["va"].reshape(bp, s, A_HEADS, HEAD_DIM),
                                     pr["misc"][:, MISC_LF:MISC_WI].reshape(bp, s, A_HEADS),
                                     pr["kb"].reshape(bp, s, B_HEADS, HEAD_DIM), pr["vb"].reshape(bp, s, B_HEADS, HEAD_DIM),
                                     pr["misc"][:, :IDX_DIM].reshape(bp, s, IDX_DIM), buf_p)):
            lst.append(val)

        ps = _project(xs, wcat, bfrow, tm_s)
        q3 = lambda a: a.reshape(nb, t, a.shape[-1])
        padrows = lambda a: jnp.pad(q3(a), ((0, 0), (0, PAGE_SIZE - t), (0, 0)))
        ki_new = padrows(ps["misc"][:, :IDX_DIM])
        lf_new = jnp.swapaxes(q3(ps["misc"][:, MISC_LF:MISC_WI]), 1, 2)
        lf_new = jnp.pad(lf_new, ((0, 0), (0, 8 - A_HEADS), (0, PAGE_SIZE - t)))
        bias = _sample_index(page_table, q3(ps["qi"]), q3(ps["misc"]), cache_b_idx_k, l, ki_new, pps_i, topk_s)
        oa_s, ob_s = _sample_attn(page_table, q3(ps["qa"]), q3(ps["qb"]), bias, (cak, cav, clf, cbk, cbv), l,
                                  (padrows(ps["ka"]), padrows(ps["va"]), lf_new, padrows(ps["kb"]), padrows(ps["vb"])),
                                  pps_a)
        x1s, buf_s = _merge(oa_s, ob_s, q3(ps["u"]), q3(ps["gb"]), q3(xs), state_conv[l].astype(F32), cw, wo_bf,
                            g1, b1, t)
        xs = _peer(x1s.reshape(n_s, d), wqt, sk, u_bf, vt_bf, g2, b2, tm_peer_s, ec)
        for lst, val in zip(outs_s, (ps["ka"].reshape(nb, t, A_HEADS, HEAD_DIM), ps["va"].reshape(nb, t, A_HEADS, HEAD_DIM),
                                     ps["misc"][:, MISC_LF:MISC_WI].reshape(nb, t, A_HEADS),
                                     ps["kb"].reshape(nb, t, B_HEADS, HEAD_DIM), ps["vb"].reshape(nb, t, B_HEADS, HEAD_DIM),
                                     ps["misc"][:, :IDX_DIM].reshape(nb, t, IDX_DIM), buf_s)):
            lst.append(val)

    return ((xp.reshape(bp, s, d), xs.reshape(nb, t, d))
            + tuple(jnp.stack(v) for v in outs_p) + tuple(jnp.stack(v) for v in outs_s))
```

```python
import functools

import jax
import jax.numpy as jnp
from jax import lax
from jax.experimental import pallas as pl
from jax.experimental.pallas import tpu as pltpu

F32 = jnp.float32
BF16 = jnp.bfloat16
I32 = jnp.int32

D_MODEL = 1024
PAGE_SIZE = 128
HEAD_DIM = 64
A_HEADS = 6
B_HEADS = 6
A_WIDTH = A_HEADS * HEAD_DIM
B_WIDTH = B_HEADS * HEAD_DIM
C_WIDTH = 256
IDX_HEADS = 8
IDX_DIM = 64
TOPK_MAX = 256
CONV_W = 3
PEER_HEADS = 8
PEER_DKEY = 128
PEER_TOPK = 16
LN_EPS = 1e-5
DEPTH = 2
DN_ALPHA = (2 * DEPTH) ** 0.25
IN_WIDTHS = (A_WIDTH, A_WIDTH, A_WIDTH, A_HEADS,
             B_WIDTH, B_WIDTH, B_WIDTH, IDX_HEADS * IDX_DIM, IDX_DIM, IDX_HEADS,
             C_WIDTH, C_WIDTH, C_WIDTH)

LANES = 128
VMEM_LIMIT = 56 * 1024 * 1024
NEG = -1e30
INT_MIN = -2 ** 31

MISC_LF = IDX_DIM
MISC_WI = IDX_DIM + A_HEADS

O_QA, O_KA, O_VA = 0, 384, 768
O_QB, O_KB, O_VB = 1152, 1536, 1920
O_QI = 2304
O_CIN, O_GB, O_GC = 3328, 3584, 3840
O_MISC = 4096
W_TOTAL = 4224


def _nt(a, b):
    return lax.dot_general(a, b, (((1,), (1,)), ((), ())), preferred_element_type=F32)


def _cparams(sem):
    return pltpu.CompilerParams(dimension_semantics=sem, vmem_limit_bytes=VMEM_LIMIT)


def _sort_key(x):
    bits = pltpu.bitcast(x, I32)
    return bits ^ ((bits >> 31) & 0x7FFFFFFF)


def _layer_norm_rows(z, g, b):
    mu = jnp.mean(z, axis=-1, keepdims=True)
    zc = z - mu
    var = jnp.mean(zc * zc, axis=-1, keepdims=True)
    return zc * lax.rsqrt(var + LN_EPS) * g + b


def _proj_kernel(x_ref, w_ref, bf_ref, qa_ref, ka_ref, va_ref, kab_ref, vab_ref,
                 qb_ref, kb_ref, vb_ref, kbb_ref, vbb_ref, qi_ref, u_ref, gb_ref,
                 misc_ref, miscb_ref):
    xb = x_ref[...].astype(BF16)

    def seg(off, width):
        return jnp.dot(xb, w_ref[:, off:off + width], preferred_element_type=F32)

    qa_ref[...] = (seg(O_QA, A_WIDTH) * (HEAD_DIM ** -0.5)).astype(BF16)
    ka = seg(O_KA, A_WIDTH)
    ka_ref[...] = ka
    kab_ref[...] = ka.astype(BF16)
    va = seg(O_VA, A_WIDTH)
    va_ref[...] = va
    vab_ref[...] = va.astype(BF16)
    qb_ref[...] = (seg(O_QB, B_WIDTH) * (HEAD_DIM ** -0.5)).astype(BF16)
    kb = seg(O_KB, B_WIDTH)
    kb_ref[...] = kb
    kbb_ref[...] = kb.astype(BF16)
    vb = seg(O_VB, B_WIDTH)
    vb_ref[...] = vb
    vbb_ref[...] = vb.astype(BF16)
    qi_ref[...] = (seg(O_QI, IDX_HEADS * LANES) * (IDX_DIM ** -0.5)).astype(BF16)
    cin = seg(O_CIN, C_WIDTH)
    gb_ref[...] = seg(O_GB, C_WIDTH)
    gc = seg(O_GC, C_WIDTH)
    u_ref[...] = gc * cin
    misc = seg(O_MISC, LANES)
    lane = lax.broadcasted_iota(I32, misc.shape, 1)
    z = misc + bf_ref[...]
    logsig = jnp.minimum(z, 0.0) - jnp.log1p(jnp.exp(-jnp.abs(z)))
    misc = jnp.where((lane >= MISC_LF) & (lane < MISC_WI), logsig, misc)
    misc_ref[...] = misc
    miscb_ref[...] = misc.astype(BF16)


def _project(x, wcat, bfrow, tm):
    n = x.shape[0]
    row = lambda w: pl.BlockSpec((tm, w), lambda i: (i, 0))
    widths = [(A_WIDTH, BF16), (A_WIDTH, F32), (A_WIDTH, F32), (A_WIDTH, BF16), (A_WIDTH, BF16),
              (B_WIDTH, BF16), (B_WIDTH, F32), (B_WIDTH, F32), (B_WIDTH, BF16), (B_WIDTH, BF16),
              (IDX_HEADS * LANES, BF16), (C_WIDTH, F32), (C_WIDTH, F32), (LANES, F32), (LANES, BF16)]
    names = ["qa", "ka", "va", "kab", "vab", "qb", "kb", "vb", "kbb", "vbb", "qi", "u", "gb", "misc", "miscb"]
    outs = pl.pallas_call(
        _proj_kernel,
        grid=(n // tm,),
        in_specs=[row(D_MODEL),
                  pl.BlockSpec((D_MODEL, W_TOTAL), lambda i: (0, 0)),
                  pl.BlockSpec((1, LANES), lambda i: (0, 0))],
        out_specs=[row(w) for w, _ in widths],
        out_shape=[jax.ShapeDtypeStruct((n, w), dt) for w, dt in widths],
        compiler_params=_cparams(("parallel",)),
        name="proj",
    )(x, wcat, bfrow)
    return dict(zip(names, outs))


def _build_wcat(w_in, b_f):
    parts, off = [], 0
    for w in IN_WIDTHS:
        parts.append(w_in[:, off:off + w])
        off += w
    qa, ka, va, fa, qb, kb, vb, qi, ki, wi, cin, gb, gc = parts
    d = w_in.shape[0]
    qi_pad = jnp.pad(qi.reshape(d, IDX_HEADS, IDX_DIM), ((0, 0), (0, 0), (0, LANES - IDX_DIM)))
    qi_pad = qi_pad.reshape(d, IDX_HEADS * LANES)
    misc = jnp.concatenate([ki, fa, wi, jnp.zeros((d, LANES - MISC_WI - IDX_HEADS), w_in.dtype)], axis=1)
    wcat = jnp.concatenate([qa, ka, va, qb, kb, vb, qi_pad, cin, gb, gc, misc], axis=1).astype(BF16)
    bfrow = jnp.zeros((1, LANES), F32).at[0, MISC_LF:MISC_WI].set(b_f.astype(F32))
    return wcat, bfrow


def _cumsum_kernel(m_ref, cum_ref, cumt_ref, carry_ref, *, tc):
    @pl.when(pl.program_id(1) == 0)
    def _():
        carry_ref[...] = jnp.zeros_like(carry_ref)

    v = m_ref[0]
    r = lax.broadcasted_iota(I32, (tc, tc), 0)
    c = lax.broadcasted_iota(I32, (tc, tc), 1)
    tri = (c <= r).astype(F32)
    loc = jnp.dot(tri, v, precision=lax.Precision.HIGHEST, preferred_element_type=F32) + carry_ref[...]
    cum_ref[0] = loc
    cumt_ref[0] = loc.T
    carry_ref[...] = loc[tc - 1:tc, :]


def _cumsum(misc3, tc):
    b, s, _ = misc3.shape
    return pl.pallas_call(
        functools.partial(_cumsum_kernel, tc=tc),
        grid=(b, s // tc),
        in_specs=[pl.BlockSpec((1, tc, LANES), lambda i, j: (i, j, 0))],
        out_specs=[pl.BlockSpec((1, tc, LANES), lambda i, j: (i, j, 0)),
                   pl.BlockSpec((1, LANES, tc), lambda i, j: (i, 0, j))],
        out_shape=[jax.ShapeDtypeStruct((b, s, LANES), F32), jax.ShapeDtypeStruct((b, LANES, s), F32)],
        scratch_shapes=[pltpu.VMEM((1, LANES), F32)],
        compiler_params=_cparams(("parallel", "arbitrary")),
        name="logf_cumsum",
    )(misc3)


def _flash_update(carry, s, vblk, v_transposed=False):
    m, l, acc = carry
    m_new = jnp.maximum(m, jnp.max(s, axis=1, keepdims=True))
    alpha = jnp.exp(m - m_new)
    p = jnp.exp(s - m_new)
    l = alpha * l + jnp.sum(p, axis=1, keepdims=True)
    p16 = p.astype(BF16)
    pv = _nt(p16, vblk) if v_transposed else jnp.dot(p16, vblk, preferred_element_type=F32)
    return m_new, l, alpha * acc + pv


def _flash_init(rows, width):
    return (jnp.full((rows, 1), NEG, F32), jnp.zeros((rows, 1), F32), jnp.zeros((rows, width), F32))


def _split_pair(qp):
    lane = lax.broadcasted_iota(I32, qp.shape, 1)
    zero = jnp.zeros_like(qp)
    return jnp.where(lane < HEAD_DIM, qp, zero), jnp.where(lane >= HEAD_DIM, qp, zero)


def _join_pair(o0, o1):
    lane = lax.broadcasted_iota(I32, o0.shape, 1)
    return jnp.where(lane < HEAD_DIM, o0, o1)


def _fox_kernel(q_ref, k_ref, v_ref, cum_ref, cumt_ref, o_ref, *, tq, tk):
    p = pl.program_id(1)
    i = pl.program_id(2)
    qs = _split_pair(q_ref[0])
    lane = lax.broadcasted_iota(I32, (tq, LANES), 1)
    cumblk = cum_ref[0]
    cqs = [jnp.sum(jnp.where(lane == MISC_LF + 2 * p + hh, cumblk, 0.0), axis=1, keepdims=True)
           for hh in range(2)]
    row_g = i * tq + lax.broadcasted_iota(I32, (tq, tk), 0)
    col_l = lax.broadcasted_iota(I32, (tq, tk), 1)

    def qk(j):
        off = pl.multiple_of(j * tk, tk)
        kblk = k_ref[0, pl.ds(off, tk), :]
        return tuple(_nt(qs[hh], kblk) for hh in range(2))

    def step(j, state, qk_j, masked):
        off = pl.multiple_of(j * tk, tk)
        vblk = v_ref[0, pl.ds(off, tk), :]
        new = []
        for hh in range(2):
            ck = cumt_ref[0, pl.ds(2 * p + hh, 1), pl.ds(off, tk)]
            s = qk_j[hh] + (cqs[hh] - ck)
            if masked:
                s = jnp.where(off + col_l <= row_g, s, NEG)
            new.append(_flash_update(state[hh], s, vblk))
        return tuple(new)

    def body(j, carry):
        state, qk_j = carry
        qk_next = qk(j + 1)
        return step(j, state, qk_j, False), qk_next

    nfull = (i * tq) // tk
    init = (_flash_init(tq, LANES), _flash_init(tq, LANES))
    state, qk_last = lax.fori_loop(0, nfull, body, (init, qk(0)))
    (_, l0, acc0), (_, l1, acc1) = step(nfull, state, qk_last, True)
    o_ref[0] = _join_pair(acc0 / l0, acc1 / l1).astype(BF16)


def _fox_prompt(qa, kab, vab, cum, cumt, tq, tk):
    b, s, _ = qa.shape
    npairs = A_HEADS // 2
    return pl.pallas_call(
        functools.partial(_fox_kernel, tq=tq, tk=tk),
        grid=(b, npairs, s // tq),
        in_specs=[pl.BlockSpec((1, tq, LANES), lambda bb, p, i: (bb, i, p)),
                  pl.BlockSpec((1, s, LANES), lambda bb, p, i: (bb, 0, p)),
                  pl.BlockSpec((1, s, LANES), lambda bb, p, i: (bb, 0, p)),
                  pl.BlockSpec((1, tq, LANES), lambda bb, p, i: (bb, i, 0)),
                  pl.BlockSpec((1, 8, s), lambda bb, p, i: (bb, MISC_LF // 8, 0))],
        out_specs=pl.BlockSpec((1, tq, LANES), lambda bb, p, i: (bb, i, p)),
        out_shape=jax.ShapeDtypeStruct((b, s, A_WIDTH), BF16),
        compiler_params=_cparams(("parallel", "parallel", "arbitrary")),
        name="fox_prompt",
    )(qa, kab, vab, cum, cumt)


def _kth_threshold(count_ge, rows, k, total):
    def cond(state):
        it, _, cnt = state
        return (it < 32) & (jnp.max(cnt) > k)

    def body(state):
        it, t, cnt = state
        cand = t | lax.shift_left(jnp.int32(1), 31 - it)
        c = count_ge(cand ^ INT_MIN)
        ok = c >= k
        return it + 1, jnp.where(ok, cand, t), jnp.where(ok, c, cnt)

    init = (jnp.int32(0), jnp.zeros((rows, 1), I32), jnp.full((rows, 1), total, I32))
    _, t, cnt = lax.while_loop(cond, body, init)
    return t ^ INT_MIN, cnt


def _tie_cutoff(count_tie_below, rows, need, nbits):
    def body(it, j):
        cand = j | lax.shift_left(jnp.int32(1), nbits - 1 - it)
        cnt = count_tie_below(cand)
        return jnp.where(cnt <= need, cand, j)
    return lax.fori_loop(0, nbits, body, jnp.zeros((rows, 1), I32))


def _dsa_kernel(qi_ref, misc_ref, kib_ref, qb_ref, kb_ref, vb_ref, o_ref, key_ref, bias_ref,
                *, tq, kc, kcc, topk, nbits):
    i = pl.program_id(1)
    nch = (i * tq + tq + kc - 1) // kc
    ncc = (i * tq + tq + kcc - 1) // kcc
    nch1 = ncc * (kcc // kc)
    w8 = misc_ref[0][:, MISC_WI:MISC_WI + IDX_HEADS] * (IDX_HEADS ** -0.5)
    row_g = i * tq + lax.broadcasted_iota(I32, (tq, kc), 0)
    col_l = lax.broadcasted_iota(I32, (tq, kc), 1)
    col_c = lax.broadcasted_iota(I32, (tq, kcc), 1)

    def p1(c, carry):
        off = pl.multiple_of(c * kc, kc)
        kib = kib_ref[0, pl.ds(off, kc), :]
        acc = jnp.zeros((tq, kc), F32)
        for h in range(IDX_HEADS):
            r = _nt(qi_ref[0, :, h * LANES:(h + 1) * LANES], kib)
            acc = acc + w8[:, h:h + 1] * jnp.maximum(r, 0.0)
        acc = jnp.where(acc == 0.0, 0.0, acc)
        sc = jnp.where(off + col_l <= row_g, acc, -jnp.inf)
        key_ref[:, pl.ds(off, kc)] = _sort_key(sc)
        return carry

    lax.fori_loop(0, nch1, p1, 0)

    def lane_groups(x):
        tot = x[:, 0:LANES]
        for g in range(1, kcc // LANES):
            tot = tot + x[:, g * LANES:(g + 1) * LANES]
        return tot

    def count(pred):
        def body(c, cnt):
            off = pl.multiple_of(c * kcc, kcc)
            return cnt + lane_groups(pred(key_ref[:, pl.ds(off, kcc)], off).astype(I32))
        cnt = lax.fori_loop(0, ncc, body, jnp.zeros((tq, LANES), I32))
        return jnp.sum(cnt, axis=1, keepdims=True)

    thr, n_ge = _kth_threshold(lambda t: count(lambda kk, off: kk >= t), tq, topk, ncc * kcc)
    big = jnp.full((tq, 1), 2 ** nbits - 1, I32)

    def tie_cutoff():
        need = topk - count(lambda kk, off: kk > thr)
        return _tie_cutoff(lambda j: count(lambda kk, off: (kk == thr) & (off + col_c < j)), tq, need, nbits)

    jstar = lax.cond(jnp.max(n_ge) > topk, tie_cutoff, lambda: big)

    def p2(c, carry):
        off = pl.multiple_of(c * kc, kc)
        kk = key_ref[:, pl.ds(off, kc)]
        colg = off + col_l
        sel = (kk > thr) | ((kk == thr) & (colg < jstar))
        bias_ref[:, pl.ds(off, kc)] = jnp.where(sel & (colg <= row_g), 0.0, NEG)
        return carry

    lax.fori_loop(0, nch, p2, 0)

    for pr in range(B_HEADS // 2):
        qs = _split_pair(qb_ref[0, :, pr * LANES:(pr + 1) * LANES])

        def qk(c):
            off = pl.multiple_of(c * kc, kc)
            kblk = kb_ref[0, pl.ds(off, kc), pr * LANES:(pr + 1) * LANES]
            return tuple(_nt(qs[hh], kblk) for hh in range(2))

        def step(c, state, qk_c):
            off = pl.multiple_of(c * kc, kc)
            vblk = vb_ref[0, pl.ds(off, kc), pr * LANES:(pr + 1) * LANES]
            bias = bias_ref[:, pl.ds(off, kc)]
            return tuple(_flash_update(state[hh], qk_c[hh] + bias, vblk) for hh in range(2))

        def body(c, carry):
            state, qk_c = carry
            qk_next = qk(c + 1)
            return step(c, state, qk_c), qk_next

        init = (_flash_init(tq, LANES), _flash_init(tq, LANES))
        state, qk_last = lax.fori_loop(0, nch - 1, body, (init, qk(0)))
        (_, l0, acc0), (_, l1, acc1) = step(nch - 1, state, qk_last)
        o_ref[0, :, pr * LANES:(pr + 1) * LANES] = _join_pair(acc0 / l0, acc1 / l1).astype(BF16)


def _dsa_prompt(qi, misc, miscb, qb, kbb, vbb, tq, kc, topk):
    b, s, _ = qb.shape
    nbits = max(1, (s - 1).bit_length()) + 1
    kcc = 2 * kc if s % (2 * kc) == 0 else kc
    qspec = lambda w: pl.BlockSpec((1, tq, w), lambda bb, i: (bb, i, 0))
    full = lambda w: pl.BlockSpec((1, s, w), lambda bb, i: (bb, 0, 0))
    return pl.pallas_call(
        functools.partial(_dsa_kernel, tq=tq, kc=kc, kcc=kcc, topk=topk, nbits=nbits),
        grid=(b, s // tq),
        in_specs=[qspec(IDX_HEADS * LANES), qspec(LANES), full(LANES), qspec(B_WIDTH), full(B_WIDTH), full(B_WIDTH)],
        out_specs=qspec(B_WIDTH),
        out_shape=jax.ShapeDtypeStruct((b, s, B_WIDTH), BF16),
        scratch_shapes=[pltpu.VMEM((tq, s), I32), pltpu.VMEM((tq, s), F32)],
        compiler_params=_cparams(("parallel", "arbitrary")),
        name="dsa_prompt",
    )(qi, misc, miscb, qb, kbb, vbb)


def _sidx_kernel(pt_ref, qi_ref, misc_ref, *rest, pps, nps, past, topk, nbits):
    page_refs = rest[:pps]
    knew_ref, bias_ref, key_ref, qst_ref, wst_ref = rest[pps:]
    j = pl.program_id(1)
    t = qi_ref.shape[1]
    lp = key_ref.shape[1]

    @pl.when(j == 0)
    def _():
        qf = qi_ref[0].astype(F32)
        qst_ref[...] = jnp.concatenate([qf[:, h * LANES:h * LANES + IDX_DIM] for h in range(IDX_HEADS)],
                                       axis=0).astype(BF16)
        w8 = misc_ref[0][:, MISC_WI:MISC_WI + IDX_HEADS] * (IDX_HEADS ** -0.5)
        wst_ref[...] = jnp.concatenate([w8[:, h:h + 1] for h in range(IDX_HEADS)], axis=0)

    def score(kt):
        r = jnp.dot(qst_ref[...], kt.astype(BF16), preferred_element_type=F32)
        r = wst_ref[...] * jnp.maximum(r, 0.0)
        acc = r[0:t, :]
        for h in range(1, IDX_HEADS):
            acc = acc + r[h * t:(h + 1) * t, :]
        return jnp.where(acc == 0.0, 0.0, acc)

    @pl.when(j < nps)
    def _():
        kt = jnp.concatenate([page_refs[k][0, 0] for k in range(pps)], axis=1)
        off = pl.multiple_of(j * (pps * PAGE_SIZE), pps * PAGE_SIZE)
        key_ref[:, pl.ds(off, pps * PAGE_SIZE)] = _sort_key(score(kt))

    @pl.when(j == nps)
    def _():
        rowi = lax.broadcasted_iota(I32, (t, PAGE_SIZE), 0)
        coli = lax.broadcasted_iota(I32, (t, PAGE_SIZE), 1)
        sc = jnp.where(coli <= rowi, score(knew_ref[0]), -jnp.inf)
        key_ref[:, past:past + PAGE_SIZE] = _sort_key(sc)

        keys = key_ref[...]
        colg = lax.broadcasted_iota(I32, (t, lp), 1)
        count = lambda pred: jnp.sum(pred.astype(I32), axis=1, keepdims=True)
        thr, _ = _kth_threshold(lambda th: count(keys >= th), t, topk, lp)
        need = topk - count(keys > thr)
        jstar = _tie_cutoff(lambda jj: count((keys == thr) & (colg < jj)), t, need, nbits)
        sel = (keys > thr) | ((keys == thr) & (colg < jstar))
        rowg = past + lax.broadcasted_iota(I32, (t, lp), 0)
        bias_ref[0] = jnp.where(sel & (colg <= rowg), 0.0, NEG)


def _sample_index(page_table, qi3, misc3, cache_idx_t, layer, knew_t, pps, topk):
    nb, t, _ = qi3.shape
    npg = page_table.shape[1]
    nps = npg // pps
    past = npg * PAGE_SIZE
    lp = past + PAGE_SIZE
    nbits = lp.bit_length() + 1

    def page_map(k):
        return lambda b, j, pt: (layer, pt[b * npg + jnp.minimum(j, nps - 1) * pps + k], 0, 0)

    bmap = lambda b, j, pt: (b, 0, 0)
    grid_spec = pltpu.PrefetchScalarGridSpec(
        num_scalar_prefetch=1,
        grid=(nb, nps + 1),
        in_specs=[pl.BlockSpec((1, t, IDX_HEADS * LANES), bmap),
                  pl.BlockSpec((1, t, LANES), bmap)]
                 + [pl.BlockSpec((1, 1, IDX_DIM, PAGE_SIZE), page_map(k)) for k in range(pps)]
                 + [pl.BlockSpec((1, IDX_DIM, PAGE_SIZE), bmap)],
        out_specs=pl.BlockSpec((1, t, lp), bmap),
        scratch_shapes=[pltpu.VMEM((t, lp), I32), pltpu.VMEM((IDX_HEADS * t, IDX_DIM), BF16),
                        pltpu.VMEM((IDX_HEADS * t, 1), F32)],
    )
    return pl.pallas_call(
        functools.partial(_sidx_kernel, pps=pps, nps=nps, past=past, topk=topk, nbits=nbits),
        grid_spec=grid_spec,
        out_shape=jax.ShapeDtypeStruct((nb, t, lp), F32),
        compiler_params=_cparams(("parallel", "arbitrary")),
        name="sample_index",
    )(page_table.reshape(-1), qi3, misc3, *([cache_idx_t] * pps), knew_t)


def _sattn_kernel(pt_ref, qa_ref, qb_ref, bias_ref, *rest, pps, nps):
    grp = lambda g: rest[g * pps:(g + 1) * pps]
    ak_refs, av_refs, lf_refs, bk_refs, bv_refs = (grp(g) for g in range(5))
    (akn_ref, avn_ref, lfn_ref, bkn_ref, bvn_ref, oa_ref, ob_ref,
     qbd_a, qbd_b, ma, la, acca, mb, lb, accb, carry_ref) = rest[5 * pps:]
    j = pl.program_id(1)
    t = qa_ref.shape[1]
    rows = A_HEADS * t
    lane_q = lax.broadcasted_iota(I32, (t, A_WIDTH), 1)

    def block_diag(q):
        qf = q.astype(F32)
        parts = [jnp.where((lane_q >= h * HEAD_DIM) & (lane_q < (h + 1) * HEAD_DIM), qf, 0.0)
                 for h in range(A_HEADS)]
        return jnp.concatenate(parts, axis=0).astype(BF16)

    @pl.when(j == 0)
    def _():
        qbd_a[...] = block_diag(qa_ref[0])
        qbd_b[...] = block_diag(qb_ref[0])
        for r in (ma, mb):
            r[...] = jnp.full(r.shape, NEG, F32)
        for r in (la, lb, acca, accb, carry_ref):
            r[...] = jnp.zeros(r.shape, F32)

    r_i = lax.broadcasted_iota(I32, (PAGE_SIZE, PAGE_SIZE), 0)
    c_i = lax.broadcasted_iota(I32, (PAGE_SIZE, PAGE_SIZE), 1)
    triu = (r_i <= c_i).astype(F32)

    def cat(refs, dtype):
        pages = [r[0, 0] if len(r.shape) == 4 else r[0] for r in refs]
        x = pages[0] if len(pages) == 1 else jnp.concatenate(pages, axis=1)
        return x.astype(dtype)

    def update(m_ref, l_ref, acc_ref, s, vt):
        m, l, acc = _flash_update((m_ref[...], l_ref[...], acc_ref[...]), s, vt, v_transposed=True)
        m_ref[...] = m
        l_ref[...] = l
        acc_ref[...] = acc

    def process(ak, av, lf, bk, bv, bias8, mask):
        run = carry_ref[...]
        cls = []
        for r in lf:
            page = r[0, 0] if len(r.shape) == 4 else r[0]
            cl = jnp.dot(page, triu, precision=lax.Precision.HIGHEST, preferred_element_type=F32) + run
            run = jnp.broadcast_to(cl[:, PAGE_SIZE - 1:PAGE_SIZE], cl.shape)
            cls.append(cl)
        carry_ref[...] = run
        ck = cls[0] if len(cls) == 1 else jnp.concatenate(cls, axis=1)
        width = ck.shape[1]
        ck_rows = jnp.concatenate([jnp.broadcast_to(ck[h:h + 1, :], (t, width)) for h in range(A_HEADS)], axis=0)
        sa = jnp.dot(qbd_a[...], cat(ak, BF16), preferred_element_type=F32) - ck_rows
        if mask is not None:
            sa = jnp.where(mask, sa, NEG)
        update(ma, la, acca, sa, cat(av, BF16))
        sb = jnp.dot(qbd_b[...], cat(bk, BF16), preferred_element_type=F32)
        sb = sb + jnp.concatenate([bias8] * B_HEADS, axis=0)
        update(mb, lb, accb, sb, cat(bv, BF16))

    @pl.when(j < nps)
    def _():
        off = pl.multiple_of(j * (pps * PAGE_SIZE), pps * PAGE_SIZE)
        process(ak_refs, av_refs, lf_refs, bk_refs, bv_refs, bias_ref[0, :, pl.ds(off, pps * PAGE_SIZE)], None)

    @pl.when(j == nps)
    def _():
        past = nps * pps * PAGE_SIZE
        rowi = lax.broadcasted_iota(I32, (rows, PAGE_SIZE), 0)
        coli = lax.broadcasted_iota(I32, (rows, PAGE_SIZE), 1)
        mask = coli <= (rowi & (t - 1))
        process([akn_ref], [avn_ref], [lfn_ref], [bkn_ref], [bvn_ref],
                bias_ref[0, :, past:past + PAGE_SIZE], mask)

        def gather_heads(acc_ref, l_ref):
            o = jnp.zeros((t, A_WIDTH), F32)
            for h in range(A_HEADS):
                blk = acc_ref[h * t:(h + 1) * t, :] / l_ref[h * t:(h + 1) * t, :]
                o = jnp.where((lane_q >= h * HEAD_DIM) & (lane_q < (h + 1) * HEAD_DIM), blk, o)
            return o.astype(BF16)

        oa_ref[0] = gather_heads(acca, la)
        ob_ref[0] = gather_heads(accb, lb)


def _sample_attn(page_table, qa3, qb3, bias, caches, layer, news, pps):
    nb, t, _ = qa3.shape
    assert t & (t - 1) == 0
    npg = page_table.shape[1]
    nps = npg // pps
    lp = bias.shape[2]
    rows = A_HEADS * t

    def page_map(k):
        return lambda b, j, pt: (layer, pt[b * npg + jnp.minimum(j, nps - 1) * pps + k], 0, 0)

    bmap = lambda b, j, pt: (b, 0, 0)
    cache_specs, cache_args = [], []
    for c in caches:
        blk = (1, 1) + c.shape[2:]
        for k in range(pps):
            cache_specs.append(pl.BlockSpec(blk, page_map(k)))
            cache_args.append(c)
    new_specs = [pl.BlockSpec((1,) + a.shape[1:], bmap) for a in news]
    grid_spec = pltpu.PrefetchScalarGridSpec(
        num_scalar_prefetch=1,
        grid=(nb, nps + 1),
        in_specs=[pl.BlockSpec((1, t, A_WIDTH), bmap), pl.BlockSpec((1, t, B_WIDTH), bmap),
                  pl.BlockSpec((1, t, lp), bmap)] + cache_specs + new_specs,
        out_specs=[pl.BlockSpec((1, t, A_WIDTH), bmap), pl.BlockSpec((1, t, B_WIDTH), bmap)],
        scratch_shapes=[pltpu.VMEM((rows, A_WIDTH), BF16), pltpu.VMEM((rows, B_WIDTH), BF16),
                        pltpu.VMEM((rows, 1), F32), pltpu.VMEM((rows, 1), F32), pltpu.VMEM((rows, A_WIDTH), F32),
                        pltpu.VMEM((rows, 1), F32), pltpu.VMEM((rows, 1), F32), pltpu.VMEM((rows, B_WIDTH), F32),
                        pltpu.VMEM((8, PAGE_SIZE), F32)],
    )
    return pl.pallas_call(
        functools.partial(_sattn_kernel, pps=pps, nps=nps),
        grid_spec=grid_spec,
        out_shape=[jax.ShapeDtypeStruct((nb, t, A_WIDTH), BF16), jax.ShapeDtypeStruct((nb, t, B_WIDTH), BF16)],
        compiler_params=_cparams(("parallel", "arbitrary")),
        name="sample_attn",
    )(page_table.reshape(-1), qa3, qb3, bias, *cache_args, *news)


def _merge_kernel(oa_ref, ob_ref, u_ref, gb_ref, x_ref, buf_ref, cw_ref, wo_ref, g_ref, b_ref,
                  y_ref, nbuf_ref, carry_ref, *, tm):
    j = pl.program_id(1)

    @pl.when(j == 0)
    def _():
        carry_ref[...] = buf_ref[0]

    u = u_ref[0]
    rowi = lax.broadcasted_iota(I32, u.shape, 0)
    c0 = carry_ref[0:1, :]
    c1 = carry_ref[1:2, :]
    u1 = jnp.where(rowi == 0, c1, pltpu.roll(u, 1, 0))
    u2 = jnp.where(rowi == 0, c0, jnp.where(rowi == 1, c1, pltpu.roll(u, 2, 0)))
    yc = cw_ref[0:1, :] * u2 + cw_ref[1:2, :] * u1 + cw_ref[2:3, :] * u
    oc = (gb_ref[0] * yc).astype(BF16)
    new_carry = u[tm - 2:tm, :]
    carry_ref[...] = new_carry
    nbuf_ref[0] = new_carry
    mix = jnp.dot(oa_ref[0], wo_ref[0:A_WIDTH, :], preferred_element_type=F32)
    mix = mix + jnp.dot(ob_ref[0], wo_ref[A_WIDTH:A_WIDTH + B_WIDTH, :], preferred_element_type=F32)
    mix = mix + jnp.dot(oc, wo_ref[A_WIDTH + B_WIDTH:, :], preferred_element_type=F32)
    y_ref[0] = _layer_norm_rows(DN_ALPHA * x_ref[0] + mix, g_ref[...], b_ref[...])


def _merge(oa, ob, u, gb, x, buf, conv_w, wo_bf, g, b, tm):
    nseq, t, _ = x.shape
    blk = lambda w: pl.BlockSpec((1, tm, w), lambda s, j: (s, j, 0))
    const = lambda r, w: pl.BlockSpec((r, w), lambda s, j: (0, 0))
    return pl.pallas_call(
        functools.partial(_merge_kernel, tm=tm),
        grid=(nseq, t // tm),
        in_specs=[blk(A_WIDTH), blk(B_WIDTH), blk(C_WIDTH), blk(C_WIDTH), blk(D_MODEL),
                  pl.BlockSpec((1, CONV_W - 1, C_WIDTH), lambda s, j: (s, 0, 0)),
                  const(CONV_W, C_WIDTH), const(D_MODEL, D_MODEL), const(1, D_MODEL), const(1, D_MODEL)],
        out_specs=[blk(D_MODEL), pl.BlockSpec((1, CONV_W - 1, C_WIDTH), lambda s, j: (s, 0, 0))],
        out_shape=[jax.ShapeDtypeStruct((nseq, t, D_MODEL), F32),
                   jax.ShapeDtypeStruct((nseq, CONV_W - 1, C_WIDTH), F32)],
        scratch_shapes=[pltpu.VMEM((CONV_W - 1, C_WIDTH), F32)],
        compiler_params=_cparams(("parallel", "arbitrary")),
        name="merge_ln",
    )(oa, ob, u, gb, x, buf, conv_w, wo_bf, g, b)


def _top16_rows(s):
    nk, tm = s.shape
    idx = lax.broadcasted_iota(I32, (nk, tm), 0).astype(F32)
    work = s
    rank = jnp.full((nk, tm), 99.0, F32)
    vals = []
    for r in range(PEER_TOPK):
        m = jnp.max(work, axis=0, keepdims=True)
        first = jnp.min(jnp.where(work == m, idx, 1e9), axis=0, keepdims=True)
        hit = idx == first
        rank = jnp.where(hit, float(r), rank)
        work = jnp.where(hit, -jnp.inf, work)
        vals.append(m)
    return jnp.concatenate(vals, axis=0), rank


def _peer_kernel(x_ref, wqt_ref, sk_ref, u_ref, vt_ref, g_ref, b_ref, o_ref,
                 xt_s, outt_s, a_s, n1_s, bb_s, r2_s, *, tm, ec, nk):
    e = pl.program_id(1)
    ne = pl.num_programs(1)
    k = PEER_TOPK
    half = PEER_DKEY // 2

    @pl.when(e == 0)
    def _():
        xt = x_ref[...].T.astype(BF16)
        xt_s[...] = xt
        qt = jnp.dot(wqt_ref[...], xt, preferred_element_type=F32).astype(BF16)
        rho = lax.broadcasted_iota(I32, (80, tm), 0)
        mid = rho - 16
        r1 = jnp.where(rho < 16, rho, jnp.where(rho < 72, mid & 7, 0))
        r2 = jnp.where(rho < 16, 0, jnp.where(rho < 72, (mid >> 3) + 1, rho - 64))
        pos = (r1 * k + r2).astype(F32)
        valid = (r1 + 1) * (r2 + 1) <= k
        row8 = lax.broadcasted_iota(I32, (8, tm), 0)
        for h in range(PEER_HEADS):
            s1 = jnp.dot(sk_ref[h, 0], qt[(2 * h) * half:(2 * h + 1) * half, :], preferred_element_type=F32)
            s2 = jnp.dot(sk_ref[h, 1], qt[(2 * h + 1) * half:(2 * h + 2) * half, :], preferred_element_type=F32)
            v1, rank1 = _top16_rows(s1)
            v2, rank2 = _top16_rows(s2)
            slabs = [v1 + v2[0:1, :]]
            for j in range(1, 8):
                slabs.append(v1[0:8, :] + v2[j:j + 1, :])
            slabs.append(v1[0:1, :] + v2[8:16, :])
            cand = jnp.where(valid, jnp.concatenate(slabs, axis=0), -jnp.inf)
            cmax = v1[0:1, :] + v2[0:1, :]
            sel = jnp.zeros((80, tm), F32)
            z = jnp.zeros((1, tm), F32)
            for r in range(k):
                m = jnp.max(cand, axis=0, keepdims=True)
                first = jnp.min(jnp.where(cand == m, pos, 1e9), axis=0, keepdims=True)
                hit = pos == first
                sel = jnp.where(hit, 1.0, sel)
                cand = jnp.where(hit, -jnp.inf, cand)
                z = z + jnp.exp(m - cmax)
            top8 = sel[0:8, :]
            for j in range(1, 8):
                top8 = top8 + sel[8 + 8 * j:16 + 8 * j, :]
            extra = jnp.sum(sel[72:80, :], axis=0, keepdims=True)
            top8 = top8 + jnp.where(row8 == 0, extra, 0.0)
            ncount = jnp.concatenate([top8, sel[8:16, :]], axis=0)
            n1 = jnp.zeros((nk, tm), F32)
            for r in range(k):
                n1 = jnp.where(rank1 == float(r), ncount[r:r + 1, :], n1)
            a_s[h] = jnp.exp(s1 - v1[0:1, :])
            n1_s[h] = n1
            bb_s[h] = jnp.exp(s2 - v2[0:1, :]) / z
            r2_s[h] = rank2
        outt_s[...] = jnp.zeros_like(outt_s)

    ht = jnp.dot(u_ref[...], xt_s[...], preferred_element_type=F32)
    act = 0.5 * ht * (1.0 + lax.erf(ht * (0.5 ** 0.5)))
    per = ec // nk
    coefs = []
    for g in range(per):
        i1 = e * per + g
        gate = jnp.zeros((nk, tm), F32)
        for h in range(PEER_HEADS):
            arow = a_s[h, pl.ds(i1, 1), :]
            nrow = n1_s[h, pl.ds(i1, 1), :]
            gate = gate + jnp.where(r2_s[h] < nrow, bb_s[h], 0.0) * arow
        coefs.append((gate * act[g * nk:(g + 1) * nk, :]).astype(BF16))
    coef = jnp.concatenate(coefs, axis=0) if per > 1 else coefs[0]
    outt_s[...] += jnp.dot(vt_ref[...], coef, preferred_element_type=F32)

    @pl.when(e == ne - 1)
    def _():
        y = outt_s[...].T
        o_ref[...] = _layer_norm_rows(DN_ALPHA * x_ref[...] + y, g_ref[...], b_ref[...])


def _peer(x, wqt, sk, u_bf, vt_bf, g, b, tm, ec):
    n = x.shape[0]
    nk = sk.shape[2]
    ne = (nk * nk) // ec
    return pl.pallas_call(
        functools.partial(_peer_kernel, tm=tm, ec=ec, nk=nk),
        grid=(n // tm, ne),
        in_specs=[pl.BlockSpec((tm, D_MODEL), lambda i, e: (i, 0)),
                  pl.BlockSpec((D_MODEL, D_MODEL), lambda i, e: (0, 0)),
                  pl.BlockSpec(sk.shape, lambda i, e: (0, 0, 0, 0)),
                  pl.BlockSpec((ec, D_MODEL), lambda i, e: (e, 0)),
                  pl.BlockSpec((D_MODEL, ec), lambda i, e: (0, e)),
                  pl.BlockSpec((1, D_MODEL), lambda i, e: (0, 0)),
                  pl.BlockSpec((1, D_MODEL), lambda i, e: (0, 0))],
        out_specs=pl.BlockSpec((tm, D_MODEL), lambda i, e: (i, 0)),
        out_shape=jax.ShapeDtypeStruct((n, D_MODEL), F32),
        scratch_shapes=[pltpu.VMEM((D_MODEL, tm), BF16), pltpu.VMEM((D_MODEL, tm), F32)]
                       + [pltpu.VMEM((PEER_HEADS, nk, tm), F32)] * 4,
        compiler_params=_cparams(("parallel", "arbitrary")),
        name="peer_ln",
    )(x, wqt, sk, u_bf, vt_bf, g, b)


def _pick(n, pref):
    t = min(n, pref)
    while n % t:
        t //= 2
    return t


def kernel(x_prompt, x_sample, cache_a_k, cache_a_v, cache_a_logf, cache_b_k, cache_b_v, cache_b_idx_k, state_conv, page_table, w_in, b_fgate, conv_w, w_o, ln1_g, ln1_b, peer_wq, peer_subkeys, peer_u, peer_v, ln2_g, ln2_b):
    bp, s, d = x_prompt.shape
    nb, t, _ = x_sample.shape
    depth = w_in.shape[0]
    npool = cache_a_k.shape[1]
    npg = page_table.shape[1]
    past = npg * PAGE_SIZE
    nk = peer_subkeys.shape[3]
    topk_p = min(TOPK_MAX, s // 4)
    topk_s = min(TOPK_MAX, (past + t) // 4)

    n_p, n_s = bp * s, nb * t
    tm_p = _pick(n_p, 512)
    tm_s = _pick(n_s, 256)
    tq_fox = _pick(s, 128)
    tk_fox = _pick(s, 512)
    tq_dsa = _pick(s, 128)
    kc_dsa = _pick(s, 512)
    tc = _pick(s, 256)
    tm_merge = _pick(s, 512)
    tm_peer_p = _pick(n_p, 256)
    tm_peer_s = _pick(n_s, 256)
    ec = _pick(nk * nk, max(1024, nk))
    pps_i = _pick(npg, 8)
    pps_a = _pick(npg, 4)

    page_t = lambda c: jnp.transpose(c, (0, 1, 3, 4, 2)).reshape(depth, npool, -1, PAGE_SIZE)
    cak, cav, cbk, cbv = page_t(cache_a_k), page_t(cache_a_v), page_t(cache_b_k), page_t(cache_b_v)
    cidx = jnp.swapaxes(cache_b_idx_k, 2, 3)
    clf = jnp.pad(jnp.swapaxes(cache_a_logf, 2, 3), ((0, 0), (0, 0), (0, 8 - A_HEADS), (0, 0)))

    xp = x_prompt.reshape(n_p, d)
    xs = x_sample.reshape(n_s, d)
    outs_p = [[] for _ in range(7)]
    outs_s = [[] for _ in range(7)]
    zero_buf = jnp.zeros((bp, CONV_W - 1, C_WIDTH), F32)
    row2 = lambda v: v.reshape(1, -1).astype(F32)

    for l in range(depth):
        wcat, bfrow = _build_wcat(w_in[l], b_fgate[l])
        wo_bf = w_o[l].astype(BF16)
        wqt = peer_wq[l].T.astype(BF16)
        sk = peer_subkeys[l].astype(BF16)
        u_bf = peer_u[l].astype(BF16)
        vt_bf = peer_v[l].T.astype(BF16)
        g1, b1, g2, b2 = row2(ln1_g[l]), row2(ln1_b[l]), row2(ln2_g[l]), row2(ln2_b[l])
        cw = conv_w[l].astype(F32)

        pr = _project(xp, wcat, bfrow, tm_p)
        r3 = lambda a: a.reshape(bp, s, a.shape[-1])
        cum, cumt = _cumsum(r3(pr["misc"]), tc)
        oa = _fox_prompt(r3(pr["qa"]), r3(pr["kab"]), r3(pr["vab"]), cum, cumt, tq_fox, tk_fox)
        ob = _dsa_prompt(r3(pr["qi"]), r3(pr["misc"]), r3(pr["miscb"]), r3(pr["qb"]), r3(pr["kbb"]),
                         r3(pr["vbb"]), tq_dsa, kc_dsa, topk_p)
        x1, buf_p = _merge(oa, ob, r3(pr["u"]), r3(pr["gb"]), r3(xp), zero_buf, cw, wo_bf, g1, b1, tm_merge)
        xp = _peer(x1.reshape(n_p, d), wqt, sk, u_bf, vt_bf, g2, b2, tm_peer_p, ec)
        for lst, val in zip(outs_p, (pr["ka"].reshape(bp, s, A_HEADS, HEAD_DIM), pr["va"].reshape(bp, s, A_HEADS, HEAD_DIM),
                                     pr["misc"][:, MISC_LF:MISC_WI].reshape(bp, s, A_HEADS),
                                     pr["kb"].reshape(bp, s, B_HEADS, HEAD_DIM), pr["vb"].reshape(bp, s, B_HEADS, HEAD_DIM),
                                     pr["misc"][:, :IDX_DIM].reshape(bp, s, IDX_DIM), buf_p)):
            lst.append(val)

        ps = _project(xs, wcat, bfrow, tm_s)
        q3 = lambda a: a.reshape(nb, t, a.shape[-1])
        new_t = lambda a: jnp.pad(jnp.swapaxes(q3(a), 1, 2), ((0, 0), (0, 0), (0, PAGE_SIZE - t)))
        ki_new = new_t(ps["misc"][:, :IDX_DIM])
        lf_new = jnp.pad(new_t(ps["misc"][:, MISC_LF:MISC_WI]), ((0, 0), (0, 8 - A_HEADS), (0, 0)))
        bias = _sample_index(page_table, q3(ps["qi"]), q3(ps["misc"]), cidx, l, ki_new, pps_i, topk_s)
        oa_s, ob_s = _sample_attn(page_table, q3(ps["qa"]), q3(ps["qb"]), bias, (cak, cav, clf, cbk, cbv), l,
                                  (new_t(ps["ka"]), new_t(ps["va"]), lf_new, new_t(ps["kb"]), new_t(ps["vb"])),
                                  pps_a)
        x1s, buf_s = _merge(oa_s, ob_s, q3(ps["u"]), q3(ps["gb"]), q3(xs), state_conv[l].astype(F32), cw, wo_bf,
                            g1, b1, t)
        xs = _peer(x1s.reshape(n_s, d), wqt, sk, u_bf, vt_bf, g2, b2, tm_peer_s, ec)
        for lst, val in zip(outs_s, (ps["ka"].reshape(nb, t, A_HEADS, HEAD_DIM), ps["va"].reshape(nb, t, A_HEADS, HEAD_DIM),
                                     ps["misc"][:, MISC_LF:MISC_WI].reshape(nb, t, A_HEADS),
                                     ps["kb"].reshape(nb, t, B_HEADS, HEAD_DIM), ps["vb"].reshape(nb, t, B_HEADS, HEAD_DIM),
                                     ps["misc"][:, :IDX_DIM].reshape(nb, t, IDX_DIM), buf_s)):
            lst.append(val)

    return ((xp.reshape(bp, s, d), xs.reshape(nb, t, d))
            + tuple(jnp.stack(v) for v in outs_p) + tuple(jnp.stack(v) for v in outs_s))
```

```python
import functools

import jax
import jax.numpy as jnp
from jax import lax
from jax.experimental import pallas as pl
from jax.experimental.pallas import tpu as pltpu

F32 = jnp.float32
BF16 = jnp.bfloat16
I32 = jnp.int32

D_MODEL = 1024
PAGE_SIZE = 128
HEAD_DIM = 64
A_HEADS = 6
B_HEADS = 6
A_WIDTH = A_HEADS * HEAD_DIM
B_WIDTH = B_HEADS * HEAD_DIM
C_WIDTH = 256
IDX_HEADS = 8
IDX_DIM = 64
TOPK_MAX = 256
CONV_W = 3
PEER_HEADS = 8
PEER_DKEY = 128
PEER_TOPK = 16
LN_EPS = 1e-5
DEPTH = 2
DN_ALPHA = (2 * DEPTH) ** 0.25
IN_WIDTHS = (A_WIDTH, A_WIDTH, A_WIDTH, A_HEADS,
             B_WIDTH, B_WIDTH, B_WIDTH, IDX_HEADS * IDX_DIM, IDX_DIM, IDX_HEADS,
             C_WIDTH, C_WIDTH, C_WIDTH)

LANES = 128
MXU_DEPTH = 256
VMEM_LIMIT = 56 * 1024 * 1024
NEG = -1e30
INT_MIN = -2 ** 31

MISC_LF = IDX_DIM
MISC_WI = IDX_DIM + A_HEADS

O_QA, O_KA, O_VA = 0, 384, 768
O_QB, O_KB, O_VB = 1152, 1536, 1920
O_QI = 2304
O_CIN, O_GB, O_GC = 3328, 3584, 3840
O_MISC = 4096
W_TOTAL = 4224


def _nt(a, b):
    return lax.dot_general(a, b, (((1,), (1,)), ((), ())), preferred_element_type=F32)


def _cparams(sem):
    return pltpu.CompilerParams(dimension_semantics=sem, vmem_limit_bytes=VMEM_LIMIT)


def _sort_key(x):
    bits = pltpu.bitcast(x, I32)
    return bits ^ ((bits >> 31) & 0x7FFFFFFF)


def _layer_norm_rows(z, g, b):
    mu = jnp.mean(z, axis=-1, keepdims=True)
    zc = z - mu
    var = jnp.mean(zc * zc, axis=-1, keepdims=True)
    return zc * lax.rsqrt(var + LN_EPS) * g + b


def _proj_kernel(x_ref, w_ref, bf_ref, qa_ref, ka_ref, va_ref, kab_ref, vab_ref,
                 qb_ref, kb_ref, vb_ref, kbb_ref, vbb_ref, qi_ref, u_ref, gb_ref,
                 misc_ref, miscb_ref):
    xb = x_ref[...].astype(BF16)

    def seg(off, width):
        return jnp.dot(xb, w_ref[:, off:off + width], preferred_element_type=F32)

    qa_ref[...] = (seg(O_QA, A_WIDTH) * (HEAD_DIM ** -0.5)).astype(BF16)
    ka = seg(O_KA, A_WIDTH)
    ka_ref[...] = ka
    kab_ref[...] = ka.astype(BF16)
    va = seg(O_VA, A_WIDTH)
    va_ref[...] = va
    vab_ref[...] = va.astype(BF16)
    qb_ref[...] = (seg(O_QB, B_WIDTH) * (HEAD_DIM ** -0.5)).astype(BF16)
    kb = seg(O_KB, B_WIDTH)
    kb_ref[...] = kb
    kbb_ref[...] = kb.astype(BF16)
    vb = seg(O_VB, B_WIDTH)
    vb_ref[...] = vb
    vbb_ref[...] = vb.astype(BF16)
    qi_ref[...] = (seg(O_QI, IDX_HEADS * LANES) * (IDX_DIM ** -0.5)).astype(BF16)
    cin = seg(O_CIN, C_WIDTH)
    gb_ref[...] = seg(O_GB, C_WIDTH)
    gc = seg(O_GC, C_WIDTH)
    u_ref[...] = gc * cin
    misc = seg(O_MISC, LANES)
    lane = lax.broadcasted_iota(I32, misc.shape, 1)
    z = misc + bf_ref[...]
    logsig = jnp.minimum(z, 0.0) - jnp.log1p(jnp.exp(-jnp.abs(z)))
    misc = jnp.where((lane >= MISC_LF) & (lane < MISC_WI), logsig, misc)
    misc_ref[...] = misc
    miscb_ref[...] = misc.astype(BF16)


def _project(x, wcat, bfrow, tm):
    n = x.shape[0]
    row = lambda w: pl.BlockSpec((tm, w), lambda i: (i, 0))
    widths = [(A_WIDTH, BF16), (A_WIDTH, F32), (A_WIDTH, F32), (A_WIDTH, BF16), (A_WIDTH, BF16),
              (B_WIDTH, BF16), (B_WIDTH, F32), (B_WIDTH, F32), (B_WIDTH, BF16), (B_WIDTH, BF16),
              (IDX_HEADS * LANES, BF16), (C_WIDTH, F32), (C_WIDTH, F32), (LANES, F32), (LANES, BF16)]
    names = ["qa", "ka", "va", "kab", "vab", "qb", "kb", "vb", "kbb", "vbb", "qi", "u", "gb", "misc", "miscb"]
    outs = pl.pallas_call(
        _proj_kernel,
        grid=(n // tm,),
        in_specs=[row(D_MODEL),
                  pl.BlockSpec((D_MODEL, W_TOTAL), lambda i: (0, 0)),
                  pl.BlockSpec((1, LANES), lambda i: (0, 0))],
        out_specs=[row(w) for w, _ in widths],
        out_shape=[jax.ShapeDtypeStruct((n, w), dt) for w, dt in widths],
        compiler_params=_cparams(("parallel",)),
        name="proj",
    )(x, wcat, bfrow)
    return dict(zip(names, outs))


def _build_wcat(w_in, b_f):
    parts, off = [], 0
    for w in IN_WIDTHS:
        parts.append(w_in[:, off:off + w])
        off += w
    qa, ka, va, fa, qb, kb, vb, qi, ki, wi, cin, gb, gc = parts
    d = w_in.shape[0]
    qi_pad = jnp.pad(qi.reshape(d, IDX_HEADS, IDX_DIM), ((0, 0), (0, 0), (0, LANES - IDX_DIM)))
    qi_pad = qi_pad.reshape(d, IDX_HEADS * LANES)
    misc = jnp.concatenate([ki, fa, wi, jnp.zeros((d, LANES - MISC_WI - IDX_HEADS), w_in.dtype)], axis=1)
    wcat = jnp.concatenate([qa, ka, va, qb, kb, vb, qi_pad, cin, gb, gc, misc], axis=1).astype(BF16)
    bfrow = jnp.zeros((1, LANES), F32).at[0, MISC_LF:MISC_WI].set(b_f.astype(F32))
    return wcat, bfrow


def _cumsum_kernel(m_ref, cum_ref, cumt_ref, carry_ref, *, tc):
    @pl.when(pl.program_id(1) == 0)
    def _():
        carry_ref[...] = jnp.zeros_like(carry_ref)

    v = m_ref[0]
    r = lax.broadcasted_iota(I32, (tc, tc), 0)
    c = lax.broadcasted_iota(I32, (tc, tc), 1)
    tri = (c <= r).astype(F32)
    loc = jnp.dot(tri, v, precision=lax.Precision.HIGHEST, preferred_element_type=F32) + carry_ref[...]
    cum_ref[0] = loc
    cumt_ref[0] = loc.T
    carry_ref[...] = loc[tc - 1:tc, :]


def _cumsum(misc3, tc):
    b, s, _ = misc3.shape
    return pl.pallas_call(
        functools.partial(_cumsum_kernel, tc=tc),
        grid=(b, s // tc),
        in_specs=[pl.BlockSpec((1, tc, LANES), lambda i, j: (i, j, 0))],
        out_specs=[pl.BlockSpec((1, tc, LANES), lambda i, j: (i, j, 0)),
                   pl.BlockSpec((1, LANES, tc), lambda i, j: (i, 0, j))],
        out_shape=[jax.ShapeDtypeStruct((b, s, LANES), F32), jax.ShapeDtypeStruct((b, LANES, s), F32)],
        scratch_shapes=[pltpu.VMEM((1, LANES), F32)],
        compiler_params=_cparams(("parallel", "arbitrary")),
        name="logf_cumsum",
    )(misc3)


def _flash_update(carry, s, vblk, v_transposed=False):
    m, l, acc = carry
    m_new = jnp.maximum(m, jnp.max(s, axis=1, keepdims=True))
    alpha = jnp.exp(m - m_new)
    p = jnp.exp(s - m_new)
    l = alpha * l + jnp.sum(p, axis=1, keepdims=True)
    p16 = p.astype(BF16)
    pv = _nt(p16, vblk) if v_transposed else jnp.dot(p16, vblk, preferred_element_type=F32)
    return m_new, l, alpha * acc + pv


def _flash_init(rows, width):
    return (jnp.full((rows, 1), NEG, F32), jnp.zeros((rows, 1), F32), jnp.zeros((rows, width), F32))


def _split_pair(qp):
    lane = lax.broadcasted_iota(I32, qp.shape, 1)
    zero = jnp.zeros_like(qp)
    return jnp.where(lane < HEAD_DIM, qp, zero), jnp.where(lane >= HEAD_DIM, qp, zero)


def _join_pair(o0, o1):
    lane = lax.broadcasted_iota(I32, o0.shape, 1)
    return jnp.where(lane < HEAD_DIM, o0, o1)


def _fox_kernel(q_ref, k_ref, v_ref, cum_ref, cumt_ref, o_ref, *, tq, tk):
    p = pl.program_id(1)
    i = pl.program_id(2)
    qs = _split_pair(q_ref[0])
    lane = lax.broadcasted_iota(I32, (tq, LANES), 1)
    cumblk = cum_ref[0]
    cqs = [jnp.sum(jnp.where(lane == MISC_LF + 2 * p + hh, cumblk, 0.0), axis=1, keepdims=True)
           for hh in range(2)]
    row_g = i * tq + lax.broadcasted_iota(I32, (tq, tk), 0)
    col_l = lax.broadcasted_iota(I32, (tq, tk), 1)

    def qk(j):
        off = pl.multiple_of(j * tk, tk)
        kblk = k_ref[0, pl.ds(off, tk), :]
        return tuple(_nt(qs[hh], kblk) for hh in range(2))

    def step(j, state, qk_j, masked):
        off = pl.multiple_of(j * tk, tk)
        vblk = v_ref[0, pl.ds(off, tk), :]
        new = []
        for hh in range(2):
            ck = cumt_ref[0, pl.ds(2 * p + hh, 1), pl.ds(off, tk)]
            s = qk_j[hh] + (cqs[hh] - ck)
            if masked:
                s = jnp.where(off + col_l <= row_g, s, NEG)
            new.append(_flash_update(state[hh], s, vblk))
        return tuple(new)

    def body(j, carry):
        state, qk_j = carry
        qk_next = qk(j + 1)
        return step(j, state, qk_j, False), qk_next

    nfull = (i * tq) // tk
    init = (_flash_init(tq, LANES), _flash_init(tq, LANES))
    state, qk_last = lax.fori_loop(0, nfull, body, (init, qk(0)))
    (_, l0, acc0), (_, l1, acc1) = step(nfull, state, qk_last, True)
    o_ref[0] = _join_pair(acc0 / l0, acc1 / l1).astype(BF16)


def _fox_prompt(qa, kab, vab, cum, cumt, tq, tk):
    b, s, _ = qa.shape
    npairs = A_HEADS // 2
    return pl.pallas_call(
        functools.partial(_fox_kernel, tq=tq, tk=tk),
        grid=(b, npairs, s // tq),
        in_specs=[pl.BlockSpec((1, tq, LANES), lambda bb, p, i: (bb, i, p)),
                  pl.BlockSpec((1, s, LANES), lambda bb, p, i: (bb, 0, p)),
                  pl.BlockSpec((1, s, LANES), lambda bb, p, i: (bb, 0, p)),
                  pl.BlockSpec((1, tq, LANES), lambda bb, p, i: (bb, i, 0)),
                  pl.BlockSpec((1, 8, s), lambda bb, p, i: (bb, MISC_LF // 8, 0))],
        out_specs=pl.BlockSpec((1, tq, LANES), lambda bb, p, i: (bb, i, p)),
        out_shape=jax.ShapeDtypeStruct((b, s, A_WIDTH), BF16),
        compiler_params=_cparams(("parallel", "parallel", "arbitrary")),
        name="fox_prompt",
    )(qa, kab, vab, cum, cumt)


def _kth_threshold(count_ge, rows, k, total):
    def cond(state):
        it, _, cnt = state
        return (it < 32) & (jnp.max(cnt) > k)

    def body(state):
        it, t, cnt = state
        cand = t | lax.shift_left(jnp.int32(1), 31 - it)
        c = count_ge(cand ^ INT_MIN)
        ok = c >= k
        return it + 1, jnp.where(ok, cand, t), jnp.where(ok, c, cnt)

    init = (jnp.int32(0), jnp.zeros((rows, 1), I32), jnp.full((rows, 1), total, I32))
    _, t, cnt = lax.while_loop(cond, body, init)
    return t ^ INT_MIN, cnt


def _tie_cutoff(count_tie_below, rows, need, nbits):
    def body(it, j):
        cand = j | lax.shift_left(jnp.int32(1), nbits - 1 - it)
        cnt = count_tie_below(cand)
        return jnp.where(cnt <= need, cand, j)
    return lax.fori_loop(0, nbits, body, jnp.zeros((rows, 1), I32))


def _dsa_kernel(qi_ref, misc_ref, kib_ref, qb_ref, kb_ref, vb_ref, o_ref, key_ref, bias_ref,
                *, tq, kc, kcc, topk, nbits):
    i = pl.program_id(1)
    nch = (i * tq + tq + kc - 1) // kc
    ncc = (i * tq + tq + kcc - 1) // kcc
    nch1 = ncc * (kcc // kc)
    w8 = misc_ref[0][:, MISC_WI:MISC_WI + IDX_HEADS] * (IDX_HEADS ** -0.5)
    row_g = i * tq + lax.broadcasted_iota(I32, (tq, kc), 0)
    col_l = lax.broadcasted_iota(I32, (tq, kc), 1)
    col_c = lax.broadcasted_iota(I32, (tq, kcc), 1)

    def p1(c, carry):
        off = pl.multiple_of(c * kc, kc)
        kib = kib_ref[0, pl.ds(off, kc), :]
        acc = jnp.zeros((tq, kc), F32)
        for h in range(IDX_HEADS):
            r = _nt(qi_ref[0, :, h * LANES:(h + 1) * LANES], kib)
            acc = acc + w8[:, h:h + 1] * jnp.maximum(r, 0.0)
        acc = jnp.where(acc == 0.0, 0.0, acc)
        sc = jnp.where(off + col_l <= row_g, acc, -jnp.inf)
        key_ref[:, pl.ds(off, kc)] = _sort_key(sc)
        return carry

    lax.fori_loop(0, nch1, p1, 0)

    def lane_groups(x):
        tot = x[:, 0:LANES]
        for g in range(1, kcc // LANES):
            tot = tot + x[:, g * LANES:(g + 1) * LANES]
        return tot

    def count(pred):
        def body(c, cnt):
            off = pl.multiple_of(c * kcc, kcc)
            return cnt + lane_groups(pred(key_ref[:, pl.ds(off, kcc)], off).astype(I32))
        cnt = lax.fori_loop(0, ncc, body, jnp.zeros((tq, LANES), I32))
        return jnp.sum(cnt, axis=1, keepdims=True)

    thr, n_ge = _kth_threshold(lambda t: count(lambda kk, off: kk >= t), tq, topk, ncc * kcc)
    big = jnp.full((tq, 1), 2 ** nbits - 1, I32)

    def tie_cutoff():
        need = topk - count(lambda kk, off: kk > thr)
        return _tie_cutoff(lambda j: count(lambda kk, off: (kk == thr) & (off + col_c < j)), tq, need, nbits)

    jstar = lax.cond(jnp.max(n_ge) > topk, tie_cutoff, lambda: big)

    def p2(c, carry):
        off = pl.multiple_of(c * kc, kc)
        kk = key_ref[:, pl.ds(off, kc)]
        colg = off + col_l
        sel = (kk > thr) | ((kk == thr) & (colg < jstar))
        bias_ref[:, pl.ds(off, kc)] = jnp.where(sel & (colg <= row_g), 0.0, NEG)
        return carry

    lax.fori_loop(0, nch, p2, 0)

    for pr in range(B_HEADS // 2):
        qs = _split_pair(qb_ref[0, :, pr * LANES:(pr + 1) * LANES])

        def qk(c):
            off = pl.multiple_of(c * kc, kc)
            kblk = kb_ref[0, pl.ds(off, kc), pr * LANES:(pr + 1) * LANES]
            return tuple(_nt(qs[hh], kblk) for hh in range(2))

        def step(c, state, qk_c):
            off = pl.multiple_of(c * kc, kc)
            vblk = vb_ref[0, pl.ds(off, kc), pr * LANES:(pr + 1) * LANES]
            bias = bias_ref[:, pl.ds(off, kc)]
            return tuple(_flash_update(state[hh], qk_c[hh] + bias, vblk) for hh in range(2))

        def body(c, carry):
            state, qk_c = carry
            qk_next = qk(c + 1)
            return step(c, state, qk_c), qk_next

        init = (_flash_init(tq, LANES), _flash_init(tq, LANES))
        state, qk_last = lax.fori_loop(0, nch - 1, body, (init, qk(0)))
        (_, l0, acc0), (_, l1, acc1) = step(nch - 1, state, qk_last)
        o_ref[0, :, pr * LANES:(pr + 1) * LANES] = _join_pair(acc0 / l0, acc1 / l1).astype(BF16)


def _dsa_prompt(qi, misc, miscb, qb, kbb, vbb, tq, kc, topk):
    b, s, _ = qb.shape
    nbits = max(1, (s - 1).bit_length()) + 1
    kcc = 2 * kc if s % (2 * kc) == 0 else kc
    qspec = lambda w: pl.BlockSpec((1, tq, w), lambda bb, i: (bb, i, 0))
    full = lambda w: pl.BlockSpec((1, s, w), lambda bb, i: (bb, 0, 0))
    return pl.pallas_call(
        functools.partial(_dsa_kernel, tq=tq, kc=kc, kcc=kcc, topk=topk, nbits=nbits),
        grid=(b, s // tq),
        in_specs=[qspec(IDX_HEADS * LANES), qspec(LANES), full(LANES), qspec(B_WIDTH), full(B_WIDTH), full(B_WIDTH)],
        out_specs=qspec(B_WIDTH),
        out_shape=jax.ShapeDtypeStruct((b, s, B_WIDTH), BF16),
        scratch_shapes=[pltpu.VMEM((tq, s), I32), pltpu.VMEM((tq, s), F32)],
        compiler_params=_cparams(("parallel", "arbitrary")),
        name="dsa_prompt",
    )(qi, misc, miscb, qb, kbb, vbb)


def _sidx_kernel(pt_ref, qi_ref, misc_ref, *rest, pps, nps, past, topk, nbits):
    page_refs = rest[:pps]
    knew_ref, bias_ref, key_ref, qst_ref, wst_ref = rest[pps:]
    j = pl.program_id(1)
    t = qi_ref.shape[1]
    lp = key_ref.shape[1]

    @pl.when(j == 0)
    def _():
        qf = qi_ref[0].astype(F32)
        qst_ref[...] = jnp.concatenate([qf[:, h * LANES:h * LANES + IDX_DIM] for h in range(IDX_HEADS)],
                                       axis=0).astype(BF16)
        w8 = misc_ref[0][:, MISC_WI:MISC_WI + IDX_HEADS] * (IDX_HEADS ** -0.5)
        wst_ref[...] = jnp.concatenate([w8[:, h:h + 1] for h in range(IDX_HEADS)], axis=0)

    def score(kt):
        r = jnp.dot(qst_ref[...], kt.astype(BF16), preferred_element_type=F32)
        r = wst_ref[...] * jnp.maximum(r, 0.0)
        acc = r[0:t, :]
        for h in range(1, IDX_HEADS):
            acc = acc + r[h * t:(h + 1) * t, :]
        return jnp.where(acc == 0.0, 0.0, acc)

    @pl.when(j < nps)
    def _():
        kt = jnp.concatenate([page_refs[k][0, 0] for k in range(pps)], axis=1)
        off = pl.multiple_of(j * (pps * PAGE_SIZE), pps * PAGE_SIZE)
        key_ref[:, pl.ds(off, pps * PAGE_SIZE)] = _sort_key(score(kt))

    @pl.when(j == nps)
    def _():
        rowi = lax.broadcasted_iota(I32, (t, PAGE_SIZE), 0)
        coli = lax.broadcasted_iota(I32, (t, PAGE_SIZE), 1)
        sc = jnp.where(coli <= rowi, score(knew_ref[0]), -jnp.inf)
        key_ref[:, past:past + PAGE_SIZE] = _sort_key(sc)

        keys = key_ref[...]
        colg = lax.broadcasted_iota(I32, (t, lp), 1)
        count = lambda pred: jnp.sum(pred.astype(I32), axis=1, keepdims=True)
        thr, _ = _kth_threshold(lambda th: count(keys >= th), t, topk, lp)
        need = topk - count(keys > thr)
        jstar = _tie_cutoff(lambda jj: count((keys == thr) & (colg < jj)), t, need, nbits)
        sel = (keys > thr) | ((keys == thr) & (colg < jstar))
        rowg = past + lax.broadcasted_iota(I32, (t, lp), 0)
        bias_ref[0] = jnp.where(sel & (colg <= rowg), 0.0, NEG)


def _sample_index(page_table, qi3, misc3, cache_idx_t, layer, knew_t, pps, topk):
    nb, t, _ = qi3.shape
    npg = page_table.shape[1]
    nps = npg // pps
    past = npg * PAGE_SIZE
    lp = past + PAGE_SIZE
    nbits = lp.bit_length() + 1

    def page_map(k):
        return lambda b, j, pt: (layer, pt[b * npg + jnp.minimum(j, nps - 1) * pps + k], 0, 0)

    bmap = lambda b, j, pt: (b, 0, 0)
    grid_spec = pltpu.PrefetchScalarGridSpec(
        num_scalar_prefetch=1,
        grid=(nb, nps + 1),
        in_specs=[pl.BlockSpec((1, t, IDX_HEADS * LANES), bmap),
                  pl.BlockSpec((1, t, LANES), bmap)]
                 + [pl.BlockSpec((1, 1, IDX_DIM, PAGE_SIZE), page_map(k)) for k in range(pps)]
                 + [pl.BlockSpec((1, IDX_DIM, PAGE_SIZE), bmap)],
        out_specs=pl.BlockSpec((1, t, lp), bmap),
        scratch_shapes=[pltpu.VMEM((t, lp), I32), pltpu.VMEM((IDX_HEADS * t, IDX_DIM), BF16),
                        pltpu.VMEM((IDX_HEADS * t, 1), F32)],
    )
    return pl.pallas_call(
        functools.partial(_sidx_kernel, pps=pps, nps=nps, past=past, topk=topk, nbits=nbits),
        grid_spec=grid_spec,
        out_shape=jax.ShapeDtypeStruct((nb, t, lp), F32),
        compiler_params=_cparams(("parallel", "arbitrary")),
        name="sample_index",
    )(page_table.reshape(-1), qi3, misc3, *([cache_idx_t] * pps), knew_t)


def _sattn_kernel(pt_ref, qa_ref, qb_ref, bias_ref, *rest, pps, nps):
    grp = lambda g: rest[g * pps:(g + 1) * pps]
    ak_refs, av_refs, lf_refs, bk_refs, bv_refs = (grp(g) for g in range(5))
    (akn_ref, avn_ref, lfn_ref, bkn_ref, bvn_ref, oa_ref, ob_ref,
     qbd_a, qbd_b, ma, la, acca, mb, lb, accb, carry_ref) = rest[5 * pps:]
    j = pl.program_id(1)
    t = qa_ref.shape[1]
    rows = A_HEADS * t
    lane_q = lax.broadcasted_iota(I32, (t, A_WIDTH), 1)

    def block_diag(q):
        qf = q.astype(F32)
        parts = [jnp.where((lane_q >= h * HEAD_DIM) & (lane_q < (h + 1) * HEAD_DIM), qf, 0.0)
                 for h in range(A_HEADS)]
        return jnp.concatenate(parts, axis=0).astype(BF16)

    @pl.when(j == 0)
    def _():
        qbd_a[...] = block_diag(qa_ref[0])
        qbd_b[...] = block_diag(qb_ref[0])
        for r in (ma, mb):
            r[...] = jnp.full(r.shape, NEG, F32)
        for r in (la, lb, acca, accb, carry_ref):
            r[...] = jnp.zeros(r.shape, F32)

    r_i = lax.broadcasted_iota(I32, (PAGE_SIZE, PAGE_SIZE), 0)
    c_i = lax.broadcasted_iota(I32, (PAGE_SIZE, PAGE_SIZE), 1)
    triu = (r_i <= c_i).astype(F32)

    def cat(refs, dtype):
        pages = [r[0, 0] if len(r.shape) == 4 else r[0] for r in refs]
        x = pages[0] if len(pages) == 1 else jnp.concatenate(pages, axis=1)
        return x.astype(dtype)

    def update(m_ref, l_ref, acc_ref, s, vt):
        m, l, acc = _flash_update((m_ref[...], l_ref[...], acc_ref[...]), s, vt, v_transposed=True)
        m_ref[...] = m
        l_ref[...] = l
        acc_ref[...] = acc

    def process(ak, av, lf, bk, bv, bias8, mask):
        run = carry_ref[...]
        cls = []
        for r in lf:
            page = r[0, 0] if len(r.shape) == 4 else r[0]
            cl = jnp.dot(page, triu, precision=lax.Precision.HIGHEST, preferred_element_type=F32) + run
            run = jnp.broadcast_to(cl[:, PAGE_SIZE - 1:PAGE_SIZE], cl.shape)
            cls.append(cl)
        carry_ref[...] = run
        ck = cls[0] if len(cls) == 1 else jnp.concatenate(cls, axis=1)
        width = ck.shape[1]
        ck_rows = jnp.concatenate([jnp.broadcast_to(ck[h:h + 1, :], (t, width)) for h in range(A_HEADS)], axis=0)
        sa = jnp.dot(qbd_a[...], cat(ak, BF16), preferred_element_type=F32) - ck_rows
        if mask is not None:
            sa = jnp.where(mask, sa, NEG)
        update(ma, la, acca, sa, cat(av, BF16))
        sb = jnp.dot(qbd_b[...], cat(bk, BF16), preferred_element_type=F32)
        sb = sb + jnp.concatenate([bias8] * B_HEADS, axis=0)
        update(mb, lb, accb, sb, cat(bv, BF16))

    @pl.when(j < nps)
    def _():
        off = pl.multiple_of(j * (pps * PAGE_SIZE), pps * PAGE_SIZE)
        process(ak_refs, av_refs, lf_refs, bk_refs, bv_refs, bias_ref[0, :, pl.ds(off, pps * PAGE_SIZE)], None)

    @pl.when(j == nps)
    def _():
        past = nps * pps * PAGE_SIZE
        rowi = lax.broadcasted_iota(I32, (rows, PAGE_SIZE), 0)
        coli = lax.broadcasted_iota(I32, (rows, PAGE_SIZE), 1)
        mask = coli <= (rowi & (t - 1))
        process([akn_ref], [avn_ref], [lfn_ref], [bkn_ref], [bvn_ref],
                bias_ref[0, :, past:past + PAGE_SIZE], mask)

        def gather_heads(acc_ref, l_ref):
            o = jnp.zeros((t, A_WIDTH), F32)
            for h in range(A_HEADS):
                blk = acc_ref[h * t:(h + 1) * t, :] / l_ref[h * t:(h + 1) * t, :]
                o = jnp.where((lane_q >= h * HEAD_DIM) & (lane_q < (h + 1) * HEAD_DIM), blk, o)
            return o.astype(BF16)

        oa_ref[0] = gather_heads(acca, la)
        ob_ref[0] = gather_heads(accb, lb)


def _sample_attn(page_table, qa3, qb3, bias, caches, layer, news, pps):
    nb, t, _ = qa3.shape
    assert t & (t - 1) == 0
    npg = page_table.shape[1]
    nps = npg // pps
    lp = bias.shape[2]
    rows = A_HEADS * t

    def page_map(k):
        return lambda b, j, pt: (layer, pt[b * npg + jnp.minimum(j, nps - 1) * pps + k], 0, 0)

    bmap = lambda b, j, pt: (b, 0, 0)
    cache_specs, cache_args = [], []
    for c in caches:
        blk = (1, 1) + c.shape[2:]
        for k in range(pps):
            cache_specs.append(pl.BlockSpec(blk, page_map(k)))
            cache_args.append(c)
    new_specs = [pl.BlockSpec((1,) + a.shape[1:], bmap) for a in news]
    grid_spec = pltpu.PrefetchScalarGridSpec(
        num_scalar_prefetch=1,
        grid=(nb, nps + 1),
        in_specs=[pl.BlockSpec((1, t, A_WIDTH), bmap), pl.BlockSpec((1, t, B_WIDTH), bmap),
                  pl.BlockSpec((1, t, lp), bmap)] + cache_specs + new_specs,
        out_specs=[pl.BlockSpec((1, t, A_WIDTH), bmap), pl.BlockSpec((1, t, B_WIDTH), bmap)],
        scratch_shapes=[pltpu.VMEM((rows, A_WIDTH), BF16), pltpu.VMEM((rows, B_WIDTH), BF16),
                        pltpu.VMEM((rows, 1), F32), pltpu.VMEM((rows, 1), F32), pltpu.VMEM((rows, A_WIDTH), F32),
                        pltpu.VMEM((rows, 1), F32), pltpu.VMEM((rows, 1), F32), pltpu.VMEM((rows, B_WIDTH), F32),
                        pltpu.VMEM((8, PAGE_SIZE), F32)],
    )
    return pl.pallas_call(
        functools.partial(_sattn_kernel, pps=pps, nps=nps),
        grid_spec=grid_spec,
        out_shape=[jax.ShapeDtypeStruct((nb, t, A_WIDTH), BF16), jax.ShapeDtypeStruct((nb, t, B_WIDTH), BF16)],
        compiler_params=_cparams(("parallel", "arbitrary")),
        name="sample_attn",
    )(page_table.reshape(-1), qa3, qb3, bias, *cache_args, *news)


def _merge_kernel(oa_ref, ob_ref, u_ref, gb_ref, x_ref, buf_ref, cw_ref, wo_ref, g_ref, b_ref,
                  y_ref, nbuf_ref, carry_ref, *, tm):
    j = pl.program_id(1)

    @pl.when(j == 0)
    def _():
        carry_ref[...] = buf_ref[0]

    u = u_ref[0]
    rowi = lax.broadcasted_iota(I32, u.shape, 0)
    c0 = carry_ref[0:1, :]
    c1 = carry_ref[1:2, :]
    u1 = jnp.where(rowi == 0, c1, pltpu.roll(u, 1, 0))
    u2 = jnp.where(rowi == 0, c0, jnp.where(rowi == 1, c1, pltpu.roll(u, 2, 0)))
    yc = cw_ref[0:1, :] * u2 + cw_ref[1:2, :] * u1 + cw_ref[2:3, :] * u
    oc = (gb_ref[0] * yc).astype(BF16)
    new_carry = u[tm - 2:tm, :]
    carry_ref[...] = new_carry
    nbuf_ref[0] = new_carry
    mix = jnp.dot(oa_ref[0], wo_ref[0:A_WIDTH, :], preferred_element_type=F32)
    mix = mix + jnp.dot(ob_ref[0], wo_ref[A_WIDTH:A_WIDTH + B_WIDTH, :], preferred_element_type=F32)
    mix = mix + jnp.dot(oc, wo_ref[A_WIDTH + B_WIDTH:, :], preferred_element_type=F32)
    y_ref[0] = _layer_norm_rows(DN_ALPHA * x_ref[0] + mix, g_ref[...], b_ref[...])


def _merge(oa, ob, u, gb, x, buf, conv_w, wo_bf, g, b, tm):
    nseq, t, _ = x.shape
    blk = lambda w: pl.BlockSpec((1, tm, w), lambda s, j: (s, j, 0))
    const = lambda r, w: pl.BlockSpec((r, w), lambda s, j: (0, 0))
    return pl.pallas_call(
        functools.partial(_merge_kernel, tm=tm),
        grid=(nseq, t // tm),
        in_specs=[blk(A_WIDTH), blk(B_WIDTH), blk(C_WIDTH), blk(C_WIDTH), blk(D_MODEL),
                  pl.BlockSpec((1, CONV_W - 1, C_WIDTH), lambda s, j: (s, 0, 0)),
                  const(CONV_W, C_WIDTH), const(D_MODEL, D_MODEL), const(1, D_MODEL), const(1, D_MODEL)],
        out_specs=[blk(D_MODEL), pl.BlockSpec((1, CONV_W - 1, C_WIDTH), lambda s, j: (s, 0, 0))],
        out_shape=[jax.ShapeDtypeStruct((nseq, t, D_MODEL), F32),
                   jax.ShapeDtypeStruct((nseq, CONV_W - 1, C_WIDTH), F32)],
        scratch_shapes=[pltpu.VMEM((CONV_W - 1, C_WIDTH), F32)],
        compiler_params=_cparams(("parallel", "arbitrary")),
        name="merge_ln",
    )(oa, ob, u, gb, x, buf, conv_w, wo_bf, g, b)


def _top16_rows(s_list):
    nk, tm = s_list[0].shape

    def extract(s, first_occurrence):
        idx = lax.broadcasted_iota(I32, (nk, tm), 0).astype(F32)
        work = s
        rank = jnp.full((nk, tm), 99.0, F32)
        vals = []
        for r in range(PEER_TOPK):
            m = jnp.max(work, axis=0, keepdims=True)
            hit = work == m
            if first_occurrence:
                hit = idx == jnp.min(jnp.where(hit, idx, 1e9), axis=0, keepdims=True)
            rank = jnp.where(hit, float(r), rank)
            work = jnp.where(hit, -jnp.inf, work)
            vals.append(m)
        return jnp.concatenate(vals, axis=0), rank

    fast = [extract(s, False) for s in s_list]
    bad = jnp.zeros((1, tm), F32)
    for _, rank in fast:
        nsel = jnp.sum(jnp.where(rank < 99.0, 1.0, 0.0), axis=0, keepdims=True)
        bad = jnp.maximum(bad, jnp.where(nsel != float(PEER_TOPK), 1.0, 0.0))
    flat = lambda pairs: tuple(x for pair in pairs for x in pair)
    out = lax.cond(jnp.max(bad) > 0.0, lambda: flat([extract(s, True) for s in s_list]), lambda: flat(fast))
    return [(out[2 * i], out[2 * i + 1]) for i in range(len(s_list))]


def _peer_kernel(x_ref, wqt_ref, sk_ref, u_ref, vt_ref, g_ref, b_ref, o_ref,
                 xt_s, qt_s, outt_s, a_s, n1_s, bb_s, r2_s, *, tm, ec, nk):
    e = pl.program_id(1)
    ne = pl.num_programs(1)
    k = PEER_TOPK
    half = PEER_DKEY // 2

    @pl.when(e == 0)
    def _():
        xt = x_ref[...].T.astype(BF16)
        xt_s[...] = xt
        qt_s[...] = jnp.dot(wqt_ref[...], xt, preferred_element_type=F32).astype(BF16)
        rho = lax.broadcasted_iota(I32, (80, tm), 0)
        mid = rho - 16
        r1 = jnp.where(rho < 16, rho, jnp.where(rho < 72, mid & 7, 0))
        r2 = jnp.where(rho < 16, 0, jnp.where(rho < 72, (mid >> 3) + 1, rho - 64))
        pos = (r1 * k + r2).astype(F32)
        valid = (r1 + 1) * (r2 + 1) <= k
        row8 = lax.broadcasted_iota(I32, (8, tm), 0)
        def head(h, carry):
            q1 = qt_s[pl.ds(pl.multiple_of(2 * h * half, half), half), :]
            q2 = qt_s[pl.ds(pl.multiple_of((2 * h + 1) * half, half), half), :]
            s1 = jnp.dot(sk_ref[h, 0], q1, preferred_element_type=F32)
            s2 = jnp.dot(sk_ref[h, 1], q2, preferred_element_type=F32)
            (v1, rank1), (v2, rank2) = _top16_rows([s1, s2])
            slabs = [v1 + v2[0:1, :]]
            for j in range(1, 8):
                slabs.append(v1[0:8, :] + v2[j:j + 1, :])
            slabs.append(v1[0:1, :] + v2[8:16, :])
            cand = jnp.where(valid, jnp.concatenate(slabs, axis=0), -jnp.inf)
            cmax = v1[0:1, :] + v2[0:1, :]
            sel = jnp.zeros((80, tm), F32)
            z = jnp.zeros((1, tm), F32)
            for r in range(k):
                m = jnp.max(cand, axis=0, keepdims=True)
                first = jnp.min(jnp.where(cand == m, pos, 1e9), axis=0, keepdims=True)
                hit = pos == first
                sel = jnp.where(hit, 1.0, sel)
                cand = jnp.where(hit, -jnp.inf, cand)
                z = z + jnp.exp(m - cmax)
            top8 = sel[0:8, :]
            for j in range(1, 8):
                top8 = top8 + sel[8 + 8 * j:16 + 8 * j, :]
            extra = jnp.sum(sel[72:80, :], axis=0, keepdims=True)
            top8 = top8 + jnp.where(row8 == 0, extra, 0.0)
            ncount = jnp.concatenate([top8, sel[8:16, :]], axis=0)
            n1 = jnp.zeros((nk, tm), F32)
            for r in range(k):
                n1 = jnp.where(rank1 == float(r), ncount[r:r + 1, :], n1)
            a_s[h] = jnp.exp(s1 - v1[0:1, :])
            n1_s[h] = n1
            bb_s[h] = jnp.exp(s2 - v2[0:1, :]) / z
            r2_s[h] = rank2
            return carry

        lax.fori_loop(0, PEER_HEADS, head, 0)
        outt_s[...] = jnp.zeros_like(outt_s)

    per = ec // nk
    gps = max(1, min(per, MXU_DEPTH // nk))
    sl = gps * nk
    xt = xt_s[...]
    acc = None
    nsl = ec // sl
    hidden = lambda sb: jnp.dot(u_ref[sb * sl:(sb + 1) * sl, :], xt, preferred_element_type=F32)
    ht_next = hidden(0)
    for sb in range(nsl):
        ht = ht_next
        if sb + 1 < nsl:
            ht_next = hidden(sb + 1)
        act = 0.5 * ht * (1.0 + lax.erf(ht * (0.5 ** 0.5)))
        coefs = []
        for g in range(gps):
            i1 = e * per + sb * gps + g
            gate = jnp.zeros((nk, tm), F32)
            for h in range(PEER_HEADS):
                arow = a_s[h, pl.ds(i1, 1), :]
                nrow = n1_s[h, pl.ds(i1, 1), :]
                gate = gate + jnp.where(r2_s[h] < nrow, bb_s[h], 0.0) * arow
            coefs.append((gate * act[g * nk:(g + 1) * nk, :]).astype(BF16))
        coef = jnp.concatenate(coefs, axis=0) if gps > 1 else coefs[0]
        part = jnp.dot(vt_ref[:, sb * sl:(sb + 1) * sl], coef, preferred_element_type=F32)
        acc = part if acc is None else acc + part
    outt_s[...] += acc

    @pl.when(e == ne - 1)
    def _():
        y = outt_s[...].T
        o_ref[...] = _layer_norm_rows(DN_ALPHA * x_ref[...] + y, g_ref[...], b_ref[...])


def _peer(x, wqt, sk, u_bf, vt_bf, g, b, tm, ec):
    n = x.shape[0]
    nk = sk.shape[2]
    ne = (nk * nk) // ec
    return pl.pallas_call(
        functools.partial(_peer_kernel, tm=tm, ec=ec, nk=nk),
        grid=(n // tm, ne),
        in_specs=[pl.BlockSpec((tm, D_MODEL), lambda i, e: (i, 0)),
                  pl.BlockSpec((D_MODEL, D_MODEL), lambda i, e: (0, 0)),
                  pl.BlockSpec(sk.shape, lambda i, e: (0, 0, 0, 0)),
                  pl.BlockSpec((ec, D_MODEL), lambda i, e: (e, 0)),
                  pl.BlockSpec((D_MODEL, ec), lambda i, e: (0, e)),
                  pl.BlockSpec((1, D_MODEL), lambda i, e: (0, 0)),
                  pl.BlockSpec((1, D_MODEL), lambda i, e: (0, 0))],
        out_specs=pl.BlockSpec((tm, D_MODEL), lambda i, e: (i, 0)),
        out_shape=jax.ShapeDtypeStruct((n, D_MODEL), F32),
        scratch_shapes=[pltpu.VMEM((D_MODEL, tm), BF16), pltpu.VMEM((PEER_HEADS * PEER_DKEY, tm), BF16),
                        pltpu.VMEM((D_MODEL, tm), F32)]
                       + [pltpu.VMEM((PEER_HEADS, nk, tm), F32)] * 4,
        compiler_params=_cparams(("parallel", "arbitrary")),
        name="peer_ln",
    )(x, wqt, sk, u_bf, vt_bf, g, b)


def _pick(n, pref):
    t = min(n, pref)
    while n % t:
        t //= 2
    return t


def kernel(x_prompt, x_sample, cache_a_k, cache_a_v, cache_a_logf, cache_b_k, cache_b_v, cache_b_idx_k, state_conv, page_table, w_in, b_fgate, conv_w, w_o, ln1_g, ln1_b, peer_wq, peer_subkeys, peer_u, peer_v, ln2_g, ln2_b):
    bp, s, d = x_prompt.shape
    nb, t, _ = x_sample.shape
    depth = w_in.shape[0]
    npool = cache_a_k.shape[1]
    npg = page_table.shape[1]
    past = npg * PAGE_SIZE
    nk = peer_subkeys.shape[3]
    topk_p = min(TOPK_MAX, s // 4)
    topk_s = min(TOPK_MAX, (past + t) // 4)

    n_p, n_s = bp * s, nb * t
    tm_p = _pick(n_p, 512)
    tm_s = _pick(n_s, 256)
    tq_fox = _pick(s, 128)
    tk_fox = _pick(s, 512)
    tq_dsa = _pick(s, 128)
    kc_dsa = _pick(s, 512)
    tc = _pick(s, 256)
    tm_merge = _pick(s, 512)
    tm_peer_p = _pick(n_p, 256)
    tm_peer_s = _pick(n_s, 256)
    ec = _pick(nk * nk, max(1024, nk))
    pps_i = _pick(npg, 8)
    pps_a = _pick(npg, 8)

    page_t = lambda c: jnp.transpose(c, (0, 1, 3, 4, 2)).reshape(depth, npool, -1, PAGE_SIZE)
    cak, cav, cbk, cbv = page_t(cache_a_k), page_t(cache_a_v), page_t(cache_b_k), page_t(cache_b_v)
    cidx = jnp.swapaxes(cache_b_idx_k, 2, 3)
    clf = jnp.pad(jnp.swapaxes(cache_a_logf, 2, 3), ((0, 0), (0, 0), (0, 8 - A_HEADS), (0, 0)))

    xp = x_prompt.reshape(n_p, d)
    xs = x_sample.reshape(n_s, d)
    outs_p = [[] for _ in range(7)]
    outs_s = [[] for _ in range(7)]
    zero_buf = jnp.zeros((bp, CONV_W - 1, C_WIDTH), F32)
    row2 = lambda v: v.reshape(1, -1).astype(F32)

    for l in range(depth):
        wcat, bfrow = _build_wcat(w_in[l], b_fgate[l])
        wo_bf = w_o[l].astype(BF16)
        wqt = peer_wq[l].T.astype(BF16)
        sk = peer_subkeys[l].astype(BF16)
        u_bf = peer_u[l].astype(BF16)
        vt_bf = peer_v[l].T.astype(BF16)
        g1, b1, g2, b2 = row2(ln1_g[l]), row2(ln1_b[l]), row2(ln2_g[l]), row2(ln2_b[l])
        cw = conv_w[l].astype(F32)

        pr = _project(xp, wcat, bfrow, tm_p)
        r3 = lambda a: a.reshape(bp, s, a.shape[-1])
        cum, cumt = _cumsum(r3(pr["misc"]), tc)
        oa = _fox_prompt(r3(pr["qa"]), r3(pr["kab"]), r3(pr["vab"]), cum, cumt, tq_fox, tk_fox)
        ob = _dsa_prompt(r3(pr["qi"]), r3(pr["misc"]), r3(pr["miscb"]), r3(pr["qb"]), r3(pr["kbb"]),
                         r3(pr["vbb"]), tq_dsa, kc_dsa, topk_p)
        x1, buf_p = _merge(oa, ob, r3(pr["u"]), r3(pr["gb"]), r3(xp), zero_buf, cw, wo_bf, g1, b1, tm_merge)
        xp = _peer(x1.reshape(n_p, d), wqt, sk, u_bf, vt_bf, g2, b2, tm_peer_p, ec)
        for lst, val in zip(outs_p, (pr["ka"].reshape(bp, s, A_HEADS, HEAD_DIM), pr["va"].reshape(bp, s, A_HEADS, HEAD_DIM),
                                     pr["misc"][:, MISC_LF:MISC_WI].reshape(bp, s, A_HEADS),
                                     pr["kb"].reshape(bp, s, B_HEADS, HEAD_DIM), pr["vb"].reshape(bp, s, B_HEADS, HEAD_DIM),
                                     pr["misc"][:, :IDX_DIM].reshape(bp, s, IDX_DIM), buf_p)):
            lst.append(val)

        ps = _project(xs, wcat, bfrow, tm_s)
        q3 = lambda a: a.reshape(nb, t, a.shape[-1])
        new_t = lambda a: jnp.pad(jnp.swapaxes(q3(a), 1, 2), ((0, 0), (0, 0), (0, PAGE_SIZE - t)))
        ki_new = new_t(ps["misc"][:, :IDX_DIM])
        lf_new = jnp.pad(new_t(ps["misc"][:, MISC_LF:MISC_WI]), ((0, 0), (0, 8 - A_HEADS), (0, 0)))
        bias = _sample_index(page_table, q3(ps["qi"]), q3(ps["misc"]), cidx, l, ki_new, pps_i, topk_s)
        oa_s, ob_s = _sample_attn(page_table, q3(ps["qa"]), q3(ps["qb"]), bias, (cak, cav, clf, cbk, cbv), l,
                                  (new_t(ps["ka"]), new_t(ps["va"]), lf_new, new_t(ps["kb"]), new_t(ps["vb"])),
                                  pps_a)
        x1s, buf_s = _merge(oa_s, ob_s, q3(ps["u"]), q3(ps["gb"]), q3(xs), state_conv[l].astype(F32), cw, wo_bf,
                            g1, b1, t)
        xs = _peer(x1s.reshape(n_s, d), wqt, sk, u_bf, vt_bf, g2, b2, tm_peer_s, ec)
        for lst, val in zip(outs_s, (ps["ka"].reshape(nb, t, A_HEADS, HEAD_DIM), ps["va"].reshape(nb, t, A_HEADS, HEAD_DIM),
                                     ps["misc"][:, MISC_LF:MISC_WI].reshape(nb, t, A_HEADS),
                                     ps["kb"].reshape(nb, t, B_HEADS, HEAD_DIM), ps["vb"].reshape(nb, t, B_HEADS, HEAD_DIM),
                                     ps["misc"][:, :IDX_DIM].reshape(nb, t, IDX_DIM), buf_s)):
            lst.append(val)

    return ((xp.reshape(bp, s, d), xs.reshape(nb, t, d))
            + tuple(jnp.stack(v) for v in outs_p) + tuple(jnp.stack(v) for v in outs_s))
```

```python
import functools

import jax
import jax.numpy as jnp
from jax import lax
from jax.experimental import pallas as pl
from jax.experimental.pallas import tpu as pltpu

F32 = jnp.float32
BF16 = jnp.bfloat16
I32 = jnp.int32

D_MODEL = 1024
PAGE_SIZE = 128
HEAD_DIM = 64
A_HEADS = 6
B_HEADS = 6
A_WIDTH = A_HEADS * HEAD_DIM
B_WIDTH = B_HEADS * HEAD_DIM
C_WIDTH = 256
IDX_HEADS = 8
IDX_DIM = 64
TOPK_MAX = 256
CONV_W = 3
PEER_HEADS = 8
PEER_DKEY = 128
PEER_TOPK = 16
LN_EPS = 1e-5
DEPTH = 2
DN_ALPHA = (2 * DEPTH) ** 0.25
IN_WIDTHS = (A_WIDTH, A_WIDTH, A_WIDTH, A_HEADS,
             B_WIDTH, B_WIDTH, B_WIDTH, IDX_HEADS * IDX_DIM, IDX_DIM, IDX_HEADS,
             C_WIDTH, C_WIDTH, C_WIDTH)

LANES = 128
MXU_DEPTH = 256
BF16_ROWS = 16
VMEM_LIMIT = 56 * 1024 * 1024
NEG = -1e30
INT_MIN = -2 ** 31

MISC_LF = IDX_DIM
MISC_WI = IDX_DIM + A_HEADS

O_QA, O_KA, O_VA = 0, 384, 768
O_QB, O_KB, O_VB = 1152, 1536, 1920
O_QI = 2304
O_CIN, O_GB, O_GC = 3328, 3584, 3840
O_MISC = 4096
W_TOTAL = 4224


def _nt(a, b):
    return lax.dot_general(a, b, (((1,), (1,)), ((), ())), preferred_element_type=F32)


def _cparams(sem):
    return pltpu.CompilerParams(dimension_semantics=sem, vmem_limit_bytes=VMEM_LIMIT)


def _sort_key(x):
    bits = pltpu.bitcast(x, I32)
    return bits ^ ((bits >> 31) & 0x7FFFFFFF)


def _layer_norm_rows(z, g, b):
    mu = jnp.mean(z, axis=-1, keepdims=True)
    zc = z - mu
    var = jnp.mean(zc * zc, axis=-1, keepdims=True)
    return zc * lax.rsqrt(var + LN_EPS) * g + b


def _proj_kernel(x_ref, w_ref, bf_ref, qa_ref, ka_ref, va_ref, kab_ref, vab_ref,
                 qb_ref, kb_ref, vb_ref, kbb_ref, vbb_ref, qi_ref, u_ref, gb_ref,
                 misc_ref, miscb_ref):
    xb = x_ref[...].astype(BF16)

    def seg(off, width):
        return jnp.dot(xb, w_ref[:, off:off + width], preferred_element_type=F32)

    qa_ref[...] = (seg(O_QA, A_WIDTH) * (HEAD_DIM ** -0.5)).astype(BF16)
    ka = seg(O_KA, A_WIDTH)
    ka_ref[...] = ka
    kab_ref[...] = ka.astype(BF16)
    va = seg(O_VA, A_WIDTH)
    va_ref[...] = va
    vab_ref[...] = va.astype(BF16)
    qb_ref[...] = (seg(O_QB, B_WIDTH) * (HEAD_DIM ** -0.5)).astype(BF16)
    kb = seg(O_KB, B_WIDTH)
    kb_ref[...] = kb
    kbb_ref[...] = kb.astype(BF16)
    vb = seg(O_VB, B_WIDTH)
    vb_ref[...] = vb
    vbb_ref[...] = vb.astype(BF16)
    qi_ref[...] = (seg(O_QI, IDX_HEADS * LANES) * (IDX_DIM ** -0.5)).astype(BF16)
    cin = seg(O_CIN, C_WIDTH)
    gb_ref[...] = seg(O_GB, C_WIDTH)
    gc = seg(O_GC, C_WIDTH)
    u_ref[...] = gc * cin
    misc = seg(O_MISC, LANES)
    lane = lax.broadcasted_iota(I32, misc.shape, 1)
    z = misc + bf_ref[...]
    logsig = jnp.minimum(z, 0.0) - jnp.log1p(jnp.exp(-jnp.abs(z)))
    misc = jnp.where((lane >= MISC_LF) & (lane < MISC_WI), logsig, misc)
    misc_ref[...] = misc
    miscb_ref[...] = misc.astype(BF16)


def _project(x, wcat, bfrow, tm):
    n = x.shape[0]
    row = lambda w: pl.BlockSpec((tm, w), lambda i: (i, 0))
    widths = [(A_WIDTH, BF16), (A_WIDTH, F32), (A_WIDTH, F32), (A_WIDTH, BF16), (A_WIDTH, BF16),
              (B_WIDTH, BF16), (B_WIDTH, F32), (B_WIDTH, F32), (B_WIDTH, BF16), (B_WIDTH, BF16),
              (IDX_HEADS * LANES, BF16), (C_WIDTH, F32), (C_WIDTH, F32), (LANES, F32), (LANES, BF16)]
    names = ["qa", "ka", "va", "kab", "vab", "qb", "kb", "vb", "kbb", "vbb", "qi", "u", "gb", "misc", "miscb"]
    outs = pl.pallas_call(
        _proj_kernel,
        grid=(n // tm,),
        in_specs=[row(D_MODEL),
                  pl.BlockSpec((D_MODEL, W_TOTAL), lambda i: (0, 0)),
                  pl.BlockSpec((1, LANES), lambda i: (0, 0))],
        out_specs=[row(w) for w, _ in widths],
        out_shape=[jax.ShapeDtypeStruct((n, w), dt) for w, dt in widths],
        compiler_params=_cparams(("parallel",)),
        name="proj",
    )(x, wcat, bfrow)
    return dict(zip(names, outs))


def _build_wcat(w_in, b_f):
    parts, off = [], 0
    for w in IN_WIDTHS:
        parts.append(w_in[:, off:off + w])
        off += w
    qa, ka, va, fa, qb, kb, vb, qi, ki, wi, cin, gb, gc = parts
    d = w_in.shape[0]
    qi_pad = jnp.pad(qi.reshape(d, IDX_HEADS, IDX_DIM), ((0, 0), (0, 0), (0, LANES - IDX_DIM)))
    qi_pad = qi_pad.reshape(d, IDX_HEADS * LANES)
    misc = jnp.concatenate([ki, fa, wi, jnp.zeros((d, LANES - MISC_WI - IDX_HEADS), w_in.dtype)], axis=1)
    wcat = jnp.concatenate([qa, ka, va, qb, kb, vb, qi_pad, cin, gb, gc, misc], axis=1).astype(BF16)
    bfrow = jnp.zeros((1, LANES), F32).at[0, MISC_LF:MISC_WI].set(b_f.astype(F32))
    return wcat, bfrow


def _cumsum_kernel(m_ref, cum_ref, cumt_ref, carry_ref, *, tc):
    @pl.when(pl.program_id(1) == 0)
    def _():
        carry_ref[...] = jnp.zeros_like(carry_ref)

    v = m_ref[0]
    r = lax.broadcasted_iota(I32, (tc, tc), 0)
    c = lax.broadcasted_iota(I32, (tc, tc), 1)
    tri = (c <= r).astype(F32)
    loc = jnp.dot(tri, v, precision=lax.Precision.HIGHEST, preferred_element_type=F32) + carry_ref[...]
    cum_ref[0] = loc
    cumt_ref[0] = loc.T
    carry_ref[...] = loc[tc - 1:tc, :]


def _cumsum(misc3, tc):
    b, s, _ = misc3.shape
    return pl.pallas_call(
        functools.partial(_cumsum_kernel, tc=tc),
        grid=(b, s // tc),
        in_specs=[pl.BlockSpec((1, tc, LANES), lambda i, j: (i, j, 0))],
        out_specs=[pl.BlockSpec((1, tc, LANES), lambda i, j: (i, j, 0)),
                   pl.BlockSpec((1, LANES, tc), lambda i, j: (i, 0, j))],
        out_shape=[jax.ShapeDtypeStruct((b, s, LANES), F32), jax.ShapeDtypeStruct((b, LANES, s), F32)],
        scratch_shapes=[pltpu.VMEM((1, LANES), F32)],
        compiler_params=_cparams(("parallel", "arbitrary")),
        name="logf_cumsum",
    )(misc3)


def _flash_update(carry, s, vblk, v_transposed=False):
    m, l, acc = carry
    m_new = jnp.maximum(m, jnp.max(s, axis=1, keepdims=True))
    alpha = jnp.exp(m - m_new)
    p = jnp.exp(s - m_new)
    l = alpha * l + jnp.sum(p, axis=1, keepdims=True)
    p16 = p.astype(BF16)
    pv = _nt(p16, vblk) if v_transposed else jnp.dot(p16, vblk, preferred_element_type=F32)
    return m_new, l, alpha * acc + pv


def _flash_init(rows, width):
    return (jnp.full((rows, 1), NEG, F32), jnp.zeros((rows, 1), F32), jnp.zeros((rows, width), F32))


def _split_pair(qp):
    lane = lax.broadcasted_iota(I32, qp.shape, 1)
    zero = jnp.zeros_like(qp)
    return jnp.where(lane < HEAD_DIM, qp, zero), jnp.where(lane >= HEAD_DIM, qp, zero)


def _join_pair(o0, o1):
    lane = lax.broadcasted_iota(I32, o0.shape, 1)
    return jnp.where(lane < HEAD_DIM, o0, o1)


def _pipelined_blocks(nblk, store_qk, step, state):
    store_qk(0, 0)

    def pair(jj, st):
        store_qk(1, 2 * jj + 1)
        st = step(2 * jj, st, 0, False)
        store_qk(0, 2 * jj + 2)
        return step(2 * jj + 1, st, 1, False)

    npair = (nblk - 1) // 2
    st = lax.fori_loop(0, npair, pair, state)

    def one_left(st):
        return step(nblk - 1, st, 0, True)

    def two_left(st):
        store_qk(1, nblk - 1)
        st = step(nblk - 2, st, 0, False)
        return step(nblk - 1, st, 1, True)

    return lax.cond(nblk - 2 * npair == 2, two_left, one_left, st)


def _fox_kernel(q_ref, k_ref, v_ref, cum_ref, cumt_ref, o_ref, qk_s, *, tq, tk):
    p = pl.program_id(1)
    i = pl.program_id(2)
    qs = _split_pair(q_ref[0])
    lane = lax.broadcasted_iota(I32, (tq, LANES), 1)
    cumblk = cum_ref[0]
    cqs = [jnp.sum(jnp.where(lane == MISC_LF + 2 * p + hh, cumblk, 0.0), axis=1, keepdims=True)
           for hh in range(2)]
    row_g = i * tq + lax.broadcasted_iota(I32, (tq, tk), 0)
    col_l = lax.broadcasted_iota(I32, (tq, tk), 1)

    def store_qk(slot, j):
        off = pl.multiple_of(j * tk, tk)
        kblk = k_ref[0, pl.ds(off, tk), :]
        for hh in range(2):
            qk_s[slot, hh] = _nt(qs[hh], kblk)

    def step(j, state, slot, last):
        off = pl.multiple_of(j * tk, tk)
        vblk = v_ref[0, pl.ds(off, tk), :]
        new = []
        for hh in range(2):
            ck = cumt_ref[0, pl.ds(2 * p + hh, 1), pl.ds(off, tk)]
            s = qk_s[slot, hh] + (cqs[hh] - ck)
            if last:
                s = jnp.where(off + col_l <= row_g, s, NEG)
            new.append(_flash_update(state[hh], s, vblk))
        return tuple(new)

    nblk = (i * tq) // tk + 1
    init = (_flash_init(tq, LANES), _flash_init(tq, LANES))
    (_, l0, acc0), (_, l1, acc1) = _pipelined_blocks(nblk, store_qk, step, init)
    o_ref[0] = _join_pair(acc0 / l0, acc1 / l1).astype(BF16)


def _fox_prompt(qa, kab, vab, cum, cumt, tq, tk):
    b, s, _ = qa.shape
    npairs = A_HEADS // 2
    return pl.pallas_call(
        functools.partial(_fox_kernel, tq=tq, tk=tk),
        grid=(b, npairs, s // tq),
        in_specs=[pl.BlockSpec((1, tq, LANES), lambda bb, p, i: (bb, i, p)),
                  pl.BlockSpec((1, s, LANES), lambda bb, p, i: (bb, 0, p)),
                  pl.BlockSpec((1, s, LANES), lambda bb, p, i: (bb, 0, p)),
                  pl.BlockSpec((1, tq, LANES), lambda bb, p, i: (bb, i, 0)),
                  pl.BlockSpec((1, 8, s), lambda bb, p, i: (bb, MISC_LF // 8, 0))],
        out_specs=pl.BlockSpec((1, tq, LANES), lambda bb, p, i: (bb, i, p)),
        out_shape=jax.ShapeDtypeStruct((b, s, A_WIDTH), BF16),
        scratch_shapes=[pltpu.VMEM((2, 2, tq, tk), F32)],
        compiler_params=_cparams(("parallel", "parallel", "arbitrary")),
        name="fox_prompt",
    )(qa, kab, vab, cum, cumt)


def _kth_threshold(count_ge, rows, k, total):
    def cond(state):
        it, _, cnt = state
        return (it < 32) & (jnp.max(cnt) > k)

    def body(state):
        it, t, cnt = state
        cand = t | lax.shift_left(jnp.int32(1), 31 - it)
        c = count_ge(cand ^ INT_MIN)
        ok = c >= k
        return it + 1, jnp.where(ok, cand, t), jnp.where(ok, c, cnt)

    init = (jnp.int32(0), jnp.zeros((rows, 1), I32), jnp.full((rows, 1), total, I32))
    _, t, cnt = lax.while_loop(cond, body, init)
    return t ^ INT_MIN, cnt


def _tie_cutoff(count_tie_below, rows, need, nbits):
    def body(it, j):
        cand = j | lax.shift_left(jnp.int32(1), nbits - 1 - it)
        cnt = count_tie_below(cand)
        return jnp.where(cnt <= need, cand, j)
    return lax.fori_loop(0, nbits, body, jnp.zeros((rows, 1), I32))


def _dsa_kernel(qi_ref, misc_ref, kib_ref, qb_ref, kb_ref, vb_ref, o_ref, key_ref, bias_ref, qk_s,
                *, tq, kc, kcc, topk, nbits):
    i = pl.program_id(1)
    nch = (i * tq + tq + kc - 1) // kc
    ncc = (i * tq + tq + kcc - 1) // kcc
    nch1 = ncc * (kcc // kc)
    w8 = misc_ref[0][:, MISC_WI:MISC_WI + IDX_HEADS] * (IDX_HEADS ** -0.5)
    row_g = i * tq + lax.broadcasted_iota(I32, (tq, kc), 0)
    col_l = lax.broadcasted_iota(I32, (tq, kc), 1)
    col_c = lax.broadcasted_iota(I32, (tq, kcc), 1)

    def p1(c, carry):
        off = pl.multiple_of(c * kc, kc)
        kib = kib_ref[0, pl.ds(off, kc), :]
        acc = jnp.zeros((tq, kc), F32)
        for h in range(IDX_HEADS):
            r = _nt(qi_ref[0, :, h * LANES:(h + 1) * LANES], kib)
            acc = acc + w8[:, h:h + 1] * jnp.maximum(r, 0.0)
        acc = jnp.where(acc == 0.0, 0.0, acc)
        sc = jnp.where(off + col_l <= row_g, acc, -jnp.inf)
        key_ref[:, pl.ds(off, kc)] = _sort_key(sc)
        return carry

    lax.fori_loop(0, nch1, p1, 0)

    def lane_groups(x):
        tot = x[:, 0:LANES]
        for g in range(1, kcc // LANES):
            tot = tot + x[:, g * LANES:(g + 1) * LANES]
        return tot

    def count(pred):
        def body(c, cnt):
            off = pl.multiple_of(c * kcc, kcc)
            return cnt + lane_groups(pred(key_ref[:, pl.ds(off, kcc)], off).astype(I32))
        cnt = lax.fori_loop(0, ncc, body, jnp.zeros((tq, LANES), I32))
        return jnp.sum(cnt, axis=1, keepdims=True)

    thr, n_ge = _kth_threshold(lambda t: count(lambda kk, off: kk >= t), tq, topk, ncc * kcc)
    big = jnp.full((tq, 1), 2 ** nbits - 1, I32)

    def tie_cutoff():
        need = topk - count(lambda kk, off: kk > thr)
        return _tie_cutoff(lambda j: count(lambda kk, off: (kk == thr) & (off + col_c < j)), tq, need, nbits)

    jstar = lax.cond(jnp.max(n_ge) > topk, tie_cutoff, lambda: big)

    def p2(c, carry):
        off = pl.multiple_of(c * kc, kc)
        kk = key_ref[:, pl.ds(off, kc)]
        colg = off + col_l
        sel = (kk > thr) | ((kk == thr) & (colg < jstar))
        bias_ref[:, pl.ds(off, kc)] = jnp.where(sel & (colg <= row_g), 0.0, NEG)
        return carry

    lax.fori_loop(0, nch, p2, 0)

    for pr in range(B_HEADS // 2):
        qs = _split_pair(qb_ref[0, :, pr * LANES:(pr + 1) * LANES])

        def store_qk(slot, c):
            off = pl.multiple_of(c * kc, kc)
            kblk = kb_ref[0, pl.ds(off, kc), pr * LANES:(pr + 1) * LANES]
            for hh in range(2):
                qk_s[slot, hh] = _nt(qs[hh], kblk)

        def step(c, state, slot, last):
            off = pl.multiple_of(c * kc, kc)
            vblk = vb_ref[0, pl.ds(off, kc), pr * LANES:(pr + 1) * LANES]
            bias = bias_ref[:, pl.ds(off, kc)]
            return tuple(_flash_update(state[hh], qk_s[slot, hh] + bias, vblk) for hh in range(2))

        init = (_flash_init(tq, LANES), _flash_init(tq, LANES))
        (_, l0, acc0), (_, l1, acc1) = _pipelined_blocks(nch, store_qk, step, init)
        o_ref[0, :, pr * LANES:(pr + 1) * LANES] = _join_pair(acc0 / l0, acc1 / l1).astype(BF16)


def _dsa_prompt(qi, misc, miscb, qb, kbb, vbb, tq, kc, topk):
    b, s, _ = qb.shape
    nbits = max(1, (s - 1).bit_length()) + 1
    kcc = 2 * kc if s % (2 * kc) == 0 else kc
    qspec = lambda w: pl.BlockSpec((1, tq, w), lambda bb, i: (bb, i, 0))
    full = lambda w: pl.BlockSpec((1, s, w), lambda bb, i: (bb, 0, 0))
    return pl.pallas_call(
        functools.partial(_dsa_kernel, tq=tq, kc=kc, kcc=kcc, topk=topk, nbits=nbits),
        grid=(b, s // tq),
        in_specs=[qspec(IDX_HEADS * LANES), qspec(LANES), full(LANES), qspec(B_WIDTH), full(B_WIDTH), full(B_WIDTH)],
        out_specs=qspec(B_WIDTH),
        out_shape=jax.ShapeDtypeStruct((b, s, B_WIDTH), BF16),
        scratch_shapes=[pltpu.VMEM((tq, s), I32), pltpu.VMEM((tq, s), F32), pltpu.VMEM((2, 2, tq, kc), F32)],
        compiler_params=_cparams(("parallel", "arbitrary")),
        name="dsa_prompt",
    )(qi, misc, miscb, qb, kbb, vbb)


def _sidx_kernel(pt_ref, qi_ref, misc_ref, *rest, pps, nps, past, topk, nbits):
    page_refs = rest[:pps]
    knew_ref, bias_ref, key_ref, qst_ref, wst_ref = rest[pps:]
    j = pl.program_id(1)
    t = qi_ref.shape[1]
    lp = key_ref.shape[1]

    @pl.when(j == 0)
    def _():
        qf = qi_ref[0].astype(F32)
        qst_ref[...] = jnp.concatenate([qf[:, h * LANES:h * LANES + IDX_DIM] for h in range(IDX_HEADS)],
                                       axis=0).astype(BF16)
        w8 = misc_ref[0][:, MISC_WI:MISC_WI + IDX_HEADS] * (IDX_HEADS ** -0.5)
        wst_ref[...] = jnp.concatenate([w8[:, h:h + 1] for h in range(IDX_HEADS)], axis=0)

    def score(kt):
        r = jnp.dot(qst_ref[...], kt.astype(BF16), preferred_element_type=F32)
        r = wst_ref[...] * jnp.maximum(r, 0.0)
        acc = r[0:t, :]
        for h in range(1, IDX_HEADS):
            acc = acc + r[h * t:(h + 1) * t, :]
        return jnp.where(acc == 0.0, 0.0, acc)

    @pl.when(j < nps)
    def _():
        kt = jnp.concatenate([page_refs[k][0, 0] for k in range(pps)], axis=1)
        off = pl.multiple_of(j * (pps * PAGE_SIZE), pps * PAGE_SIZE)
        key_ref[:, pl.ds(off, pps * PAGE_SIZE)] = _sort_key(score(kt))

    @pl.when(j == nps)
    def _():
        rowi = lax.broadcasted_iota(I32, (t, PAGE_SIZE), 0)
        coli = lax.broadcasted_iota(I32, (t, PAGE_SIZE), 1)
        sc = jnp.where(coli <= rowi, score(knew_ref[0]), -jnp.inf)
        key_ref[:, past:past + PAGE_SIZE] = _sort_key(sc)

        keys = key_ref[...]
        colg = lax.broadcasted_iota(I32, (t, lp), 1)
        count = lambda pred: jnp.sum(pred.astype(I32), axis=1, keepdims=True)
        thr, _ = _kth_threshold(lambda th: count(keys >= th), t, topk, lp)
        need = topk - count(keys > thr)
        jstar = _tie_cutoff(lambda jj: count((keys == thr) & (colg < jj)), t, need, nbits)
        sel = (keys > thr) | ((keys == thr) & (colg < jstar))
        rowg = past + lax.broadcasted_iota(I32, (t, lp), 0)
        bias_ref[0] = jnp.where(sel & (colg <= rowg), 0.0, NEG)


def _sample_index(page_table, qi3, misc3, cache_idx_t, layer, knew_t, pps, topk):
    nb, t, _ = qi3.shape
    npg = page_table.shape[1]
    nps = npg // pps
    past = npg * PAGE_SIZE
    lp = past + PAGE_SIZE
    nbits = lp.bit_length() + 1

    def page_map(k):
        return lambda b, j, pt: (layer, pt[b * npg + jnp.minimum(j, nps - 1) * pps + k], 0, 0)

    bmap = lambda b, j, pt: (b, 0, 0)
    grid_spec = pltpu.PrefetchScalarGridSpec(
        num_scalar_prefetch=1,
        grid=(nb, nps + 1),
        in_specs=[pl.BlockSpec((1, t, IDX_HEADS * LANES), bmap),
                  pl.BlockSpec((1, t, LANES), bmap)]
                 + [pl.BlockSpec((1, 1, IDX_DIM, PAGE_SIZE), page_map(k)) for k in range(pps)]
                 + [pl.BlockSpec((1, IDX_DIM, PAGE_SIZE), bmap)],
        out_specs=pl.BlockSpec((1, t, lp), bmap),
        scratch_shapes=[pltpu.VMEM((t, lp), I32), pltpu.VMEM((IDX_HEADS * t, IDX_DIM), BF16),
                        pltpu.VMEM((IDX_HEADS * t, 1), F32)],
    )
    return pl.pallas_call(
        functools.partial(_sidx_kernel, pps=pps, nps=nps, past=past, topk=topk, nbits=nbits),
        grid_spec=grid_spec,
        out_shape=jax.ShapeDtypeStruct((nb, t, lp), F32),
        compiler_params=_cparams(("parallel", "arbitrary")),
        name="sample_index",
    )(page_table.reshape(-1), qi3, misc3, *([cache_idx_t] * pps), knew_t)


def _sattn_kernel(pt_ref, qa_ref, qb_ref, bias_ref, *rest, pps, nps):
    grp = lambda g: rest[g * pps:(g + 1) * pps]
    ak_refs, av_refs, lf_refs, bk_refs, bv_refs = (grp(g) for g in range(5))
    (akn_ref, avn_ref, lfn_ref, bkn_ref, bvn_ref, oa_ref, ob_ref,
     qbd_a, qbd_b, ma, la, acca, mb, lb, accb, carry_ref) = rest[5 * pps:]
    j = pl.program_id(1)
    t = qa_ref.shape[1]
    rows = A_HEADS * t
    lane_q = lax.broadcasted_iota(I32, (t, A_WIDTH), 1)

    def block_diag(q):
        qf = q.astype(F32)
        parts = [jnp.where((lane_q >= h * HEAD_DIM) & (lane_q < (h + 1) * HEAD_DIM), qf, 0.0)
                 for h in range(A_HEADS)]
        return jnp.concatenate(parts, axis=0).astype(BF16)

    @pl.when(j == 0)
    def _():
        qbd_a[...] = block_diag(qa_ref[0])
        qbd_b[...] = block_diag(qb_ref[0])
        for r in (ma, mb):
            r[...] = jnp.full(r.shape, NEG, F32)
        for r in (la, lb, acca, accb, carry_ref):
            r[...] = jnp.zeros(r.shape, F32)

    r_i = lax.broadcasted_iota(I32, (PAGE_SIZE, PAGE_SIZE), 0)
    c_i = lax.broadcasted_iota(I32, (PAGE_SIZE, PAGE_SIZE), 1)
    triu = (r_i <= c_i).astype(F32)

    def cat(refs, dtype):
        pages = [r[0, 0] if len(r.shape) == 4 else r[0] for r in refs]
        x = pages[0] if len(pages) == 1 else jnp.concatenate(pages, axis=1)
        return x.astype(dtype)

    def update(m_ref, l_ref, acc_ref, s, vt):
        m, l, acc = _flash_update((m_ref[...], l_ref[...], acc_ref[...]), s, vt, v_transposed=True)
        m_ref[...] = m
        l_ref[...] = l
        acc_ref[...] = acc

    def process(ak, av, lf, bk, bv, bias8, mask):
        run = carry_ref[...]
        cls = []
        for r in lf:
            page = r[0, 0] if len(r.shape) == 4 else r[0]
            cl = jnp.dot(page, triu, precision=lax.Precision.HIGHEST, preferred_element_type=F32) + run
            run = jnp.broadcast_to(cl[:, PAGE_SIZE - 1:PAGE_SIZE], cl.shape)
            cls.append(cl)
        carry_ref[...] = run
        ck = cls[0] if len(cls) == 1 else jnp.concatenate(cls, axis=1)
        width = ck.shape[1]
        ck_rows = jnp.concatenate([jnp.broadcast_to(ck[h:h + 1, :], (t, width)) for h in range(A_HEADS)], axis=0)
        sa = jnp.dot(qbd_a[...], cat(ak, BF16), preferred_element_type=F32) - ck_rows
        if mask is not None:
            sa = jnp.where(mask, sa, NEG)
        update(ma, la, acca, sa, cat(av, BF16))
        sb = jnp.dot(qbd_b[...], cat(bk, BF16), preferred_element_type=F32)
        sb = sb + jnp.concatenate([bias8] * B_HEADS, axis=0)
        update(mb, lb, accb, sb, cat(bv, BF16))

    @pl.when(j < nps)
    def _():
        off = pl.multiple_of(j * (pps * PAGE_SIZE), pps * PAGE_SIZE)
        process(ak_refs, av_refs, lf_refs, bk_refs, bv_refs, bias_ref[0, :, pl.ds(off, pps * PAGE_SIZE)], None)

    @pl.when(j == nps)
    def _():
        past = nps * pps * PAGE_SIZE
        rowi = lax.broadcasted_iota(I32, (rows, PAGE_SIZE), 0)
        coli = lax.broadcasted_iota(I32, (rows, PAGE_SIZE), 1)
        mask = coli <= (rowi & (t - 1))
        process([akn_ref], [avn_ref], [lfn_ref], [bkn_ref], [bvn_ref],
                bias_ref[0, :, past:past + PAGE_SIZE], mask)

        def gather_heads(acc_ref, l_ref):
            o = jnp.zeros((t, A_WIDTH), F32)
            for h in range(A_HEADS):
                blk = acc_ref[h * t:(h + 1) * t, :] / l_ref[h * t:(h + 1) * t, :]
                o = jnp.where((lane_q >= h * HEAD_DIM) & (lane_q < (h + 1) * HEAD_DIM), blk, o)
            return o.astype(BF16)

        oa_ref[0] = gather_heads(acca, la)
        ob_ref[0] = gather_heads(accb, lb)


def _sample_attn(page_table, qa3, qb3, bias, caches, layer, news, pps):
    nb, t, _ = qa3.shape
    assert t & (t - 1) == 0
    npg = page_table.shape[1]
    nps = npg // pps
    lp = bias.shape[2]
    rows = A_HEADS * t

    def page_map(k):
        return lambda b, j, pt: (layer, pt[b * npg + jnp.minimum(j, nps - 1) * pps + k], 0, 0)

    bmap = lambda b, j, pt: (b, 0, 0)
    cache_specs, cache_args = [], []
    for c in caches:
        blk = (1, 1) + c.shape[2:]
        for k in range(pps):
            cache_specs.append(pl.BlockSpec(blk, page_map(k)))
            cache_args.append(c)
    new_specs = [pl.BlockSpec((1,) + a.shape[1:], bmap) for a in news]
    grid_spec = pltpu.PrefetchScalarGridSpec(
        num_scalar_prefetch=1,
        grid=(nb, nps + 1),
        in_specs=[pl.BlockSpec((1, t, A_WIDTH), bmap), pl.BlockSpec((1, t, B_WIDTH), bmap),
                  pl.BlockSpec((1, t, lp), bmap)] + cache_specs + new_specs,
        out_specs=[pl.BlockSpec((1, t, A_WIDTH), bmap), pl.BlockSpec((1, t, B_WIDTH), bmap)],
        scratch_shapes=[pltpu.VMEM((rows, A_WIDTH), BF16), pltpu.VMEM((rows, B_WIDTH), BF16),
                        pltpu.VMEM((rows, 1), F32), pltpu.VMEM((rows, 1), F32), pltpu.VMEM((rows, A_WIDTH), F32),
                        pltpu.VMEM((rows, 1), F32), pltpu.VMEM((rows, 1), F32), pltpu.VMEM((rows, B_WIDTH), F32),
                        pltpu.VMEM((8, PAGE_SIZE), F32)],
    )
    return pl.pallas_call(
        functools.partial(_sattn_kernel, pps=pps, nps=nps),
        grid_spec=grid_spec,
        out_shape=[jax.ShapeDtypeStruct((nb, t, A_WIDTH), BF16), jax.ShapeDtypeStruct((nb, t, B_WIDTH), BF16)],
        compiler_params=_cparams(("parallel", "arbitrary")),
        name="sample_attn",
    )(page_table.reshape(-1), qa3, qb3, bias, *cache_args, *news)


def _merge_kernel(oa_ref, ob_ref, u_ref, gb_ref, x_ref, buf_ref, cw_ref, wo_ref, g_ref, b_ref,
                  y_ref, nbuf_ref, carry_ref, *, tm):
    j = pl.program_id(1)

    @pl.when(j == 0)
    def _():
        carry_ref[...] = buf_ref[0]

    u = u_ref[0]
    rowi = lax.broadcasted_iota(I32, u.shape, 0)
    c0 = carry_ref[0:1, :]
    c1 = carry_ref[1:2, :]
    u1 = jnp.where(rowi == 0, c1, pltpu.roll(u, 1, 0))
    u2 = jnp.where(rowi == 0, c0, jnp.where(rowi == 1, c1, pltpu.roll(u, 2, 0)))
    yc = cw_ref[0:1, :] * u2 + cw_ref[1:2, :] * u1 + cw_ref[2:3, :] * u
    oc = (gb_ref[0] * yc).astype(BF16)
    new_carry = u[tm - 2:tm, :]
    carry_ref[...] = new_carry
    nbuf_ref[0] = new_carry
    mix = jnp.dot(oa_ref[0], wo_ref[0:A_WIDTH, :], preferred_element_type=F32)
    mix = mix + jnp.dot(ob_ref[0], wo_ref[A_WIDTH:A_WIDTH + B_WIDTH, :], preferred_element_type=F32)
    mix = mix + jnp.dot(oc, wo_ref[A_WIDTH + B_WIDTH:, :], preferred_element_type=F32)
    y_ref[0] = _layer_norm_rows(DN_ALPHA * x_ref[0] + mix, g_ref[...], b_ref[...])


def _merge(oa, ob, u, gb, x, buf, conv_w, wo_bf, g, b, tm):
    nseq, t, _ = x.shape
    blk = lambda w: pl.BlockSpec((1, tm, w), lambda s, j: (s, j, 0))
    const = lambda r, w: pl.BlockSpec((r, w), lambda s, j: (0, 0))
    return pl.pallas_call(
        functools.partial(_merge_kernel, tm=tm),
        grid=(nseq, t // tm),
        in_specs=[blk(A_WIDTH), blk(B_WIDTH), blk(C_WIDTH), blk(C_WIDTH), blk(D_MODEL),
                  pl.BlockSpec((1, CONV_W - 1, C_WIDTH), lambda s, j: (s, 0, 0)),
                  const(CONV_W, C_WIDTH), const(D_MODEL, D_MODEL), const(1, D_MODEL), const(1, D_MODEL)],
        out_specs=[blk(D_MODEL), pl.BlockSpec((1, CONV_W - 1, C_WIDTH), lambda s, j: (s, 0, 0))],
        out_shape=[jax.ShapeDtypeStruct((nseq, t, D_MODEL), F32),
                   jax.ShapeDtypeStruct((nseq, CONV_W - 1, C_WIDTH), F32)],
        scratch_shapes=[pltpu.VMEM((CONV_W - 1, C_WIDTH), F32)],
        compiler_params=_cparams(("parallel", "arbitrary")),
        name="merge_ln",
    )(oa, ob, u, gb, x, buf, conv_w, wo_bf, g, b)


def _top16_rows(s_list):
    nk, tm = s_list[0].shape

    def extract(s, first_occurrence):
        idx = lax.broadcasted_iota(I32, (nk, tm), 0).astype(F32)
        work = s
        rank = jnp.full((nk, tm), 99.0, F32)
        vals = []
        for r in range(PEER_TOPK):
            m = jnp.max(work, axis=0, keepdims=True)
            hit = work == m
            if first_occurrence:
                hit = idx == jnp.min(jnp.where(hit, idx, 1e9), axis=0, keepdims=True)
            rank = jnp.where(hit, float(r), rank)
            work = jnp.where(hit, -jnp.inf, work)
            vals.append(m)
        return jnp.concatenate(vals, axis=0), rank

    fast = [extract(s, False) for s in s_list]
    bad = jnp.zeros((1, tm), F32)
    for _, rank in fast:
        nsel = jnp.sum(jnp.where(rank < 99.0, 1.0, 0.0), axis=0, keepdims=True)
        bad = jnp.maximum(bad, jnp.where(nsel != float(PEER_TOPK), 1.0, 0.0))
    flat = lambda pairs: tuple(x for pair in pairs for x in pair)
    out = lax.cond(jnp.max(bad) > 0.0, lambda: flat([extract(s, True) for s in s_list]), lambda: flat(fast))
    return [(out[2 * i], out[2 * i + 1]) for i in range(len(s_list))]


def _peer_kernel(x_ref, wqt_ref, sk_ref, u_ref, vt_ref, g_ref, b_ref, o_ref,
                 xt_s, qt_s, outt_s, a_s, n1_s, bb_s, r2_s, *, tm, ec, nk):
    e = pl.program_id(1)
    ne = pl.num_programs(1)
    k = PEER_TOPK
    half = PEER_DKEY // 2

    @pl.when(e == 0)
    def _():
        xt = x_ref[...].T.astype(BF16)
        xt_s[...] = xt
        qt_s[...] = jnp.dot(wqt_ref[...], xt, preferred_element_type=F32).astype(BF16)
        rho = lax.broadcasted_iota(I32, (80, tm), 0)
        mid = rho - 16
        r1 = jnp.where(rho < 16, rho, jnp.where(rho < 72, mid & 7, 0))
        r2 = jnp.where(rho < 16, 0, jnp.where(rho < 72, (mid >> 3) + 1, rho - 64))
        pos = (r1 * k + r2).astype(F32)
        valid = (r1 + 1) * (r2 + 1) <= k
        row8 = lax.broadcasted_iota(I32, (8, tm), 0)
        def head(h, carry):
            q1 = qt_s[pl.ds(pl.multiple_of(2 * h * half, half), half), :]
            q2 = qt_s[pl.ds(pl.multiple_of((2 * h + 1) * half, half), half), :]
            s1 = jnp.dot(sk_ref[h, 0], q1, preferred_element_type=F32)
            s2 = jnp.dot(sk_ref[h, 1], q2, preferred_element_type=F32)
            (v1, rank1), (v2, rank2) = _top16_rows([s1, s2])
            slabs = [v1 + v2[0:1, :]]
            for j in range(1, 8):
                slabs.append(v1[0:8, :] + v2[j:j + 1, :])
            slabs.append(v1[0:1, :] + v2[8:16, :])
            cand = jnp.where(valid, jnp.concatenate(slabs, axis=0), -jnp.inf)
            cmax = v1[0:1, :] + v2[0:1, :]
            sel = jnp.zeros((80, tm), F32)
            z = jnp.zeros((1, tm), F32)
            for r in range(k):
                m = jnp.max(cand, axis=0, keepdims=True)
                first = jnp.min(jnp.where(cand == m, pos, 1e9), axis=0, keepdims=True)
                hit = pos == first
                sel = jnp.where(hit, 1.0, sel)
                cand = jnp.where(hit, -jnp.inf, cand)
                z = z + jnp.exp(m - cmax)
            top8 = sel[0:8, :]
            for j in range(1, 8):
                top8 = top8 + sel[8 + 8 * j:16 + 8 * j, :]
            extra = jnp.sum(sel[72:80, :], axis=0, keepdims=True)
            top8 = top8 + jnp.where(row8 == 0, extra, 0.0)
            ncount = jnp.concatenate([top8, sel[8:16, :]], axis=0)
            n1 = jnp.zeros((nk, tm), F32)
            for r in range(k):
                n1 = jnp.where(rank1 == float(r), ncount[r:r + 1, :], n1)
            a_s[h] = jnp.exp(s1 - v1[0:1, :])
            n1_s[h] = n1
            bb_s[h] = (jnp.exp(s2 - v2[0:1, :]) / z).astype(BF16)
            r2_s[h] = rank2.astype(BF16)
            return carry

        lax.fori_loop(0, PEER_HEADS, head, 0)
        outt_s[...] = jnp.zeros_like(outt_s)

    per = ec // nk
    gps = max(1, min(per, MXU_DEPTH // nk))
    sl = gps * nk
    xt = xt_s[...]
    acc = None
    nsl = ec // sl

    def rows16(row):
        if nk % BF16_ROWS:
            return jnp.broadcast_to(row, (nk, tm)).astype(BF16)
        tile = jnp.broadcast_to(row, (BF16_ROWS, tm)).astype(BF16)
        return jnp.concatenate([tile] * (nk // BF16_ROWS), axis=0) if nk > BF16_ROWS else tile

    hidden = lambda sb: jnp.dot(u_ref[sb * sl:(sb + 1) * sl, :], xt, preferred_element_type=F32)
    ht_next = hidden(0)
    for sb in range(nsl):
        ht = ht_next
        if sb + 1 < nsl:
            ht_next = hidden(sb + 1)
        act = (0.5 * ht * (1.0 + lax.erf(ht * (0.5 ** 0.5)))).astype(BF16)
        coefs = []
        for g in range(gps):
            i1 = e * per + sb * gps + g
            gate = jnp.zeros((nk, tm), BF16)
            for h in range(PEER_HEADS):
                arow = rows16(a_s[h, pl.ds(i1, 1), :])
                nrow = rows16(n1_s[h, pl.ds(i1, 1), :])
                gate = gate + jnp.where(r2_s[h] < nrow, bb_s[h], jnp.zeros((), BF16)) * arow
            coefs.append(gate * act[g * nk:(g + 1) * nk, :])
        coef = jnp.concatenate(coefs, axis=0) if gps > 1 else coefs[0]
        part = jnp.dot(vt_ref[:, sb * sl:(sb + 1) * sl], coef, preferred_element_type=F32)
        acc = part if acc is None else acc + part
    outt_s[...] += acc

    @pl.when(e == ne - 1)
    def _():
        y = outt_s[...].T
        o_ref[...] = _layer_norm_rows(DN_ALPHA * x_ref[...] + y, g_ref[...], b_ref[...])


def _peer(x, wqt, sk, u_bf, vt_bf, g, b, tm, ec):
    n = x.shape[0]
    nk = sk.shape[2]
    ne = (nk * nk) // ec
    return pl.pallas_call(
        functools.partial(_peer_kernel, tm=tm, ec=ec, nk=nk),
        grid=(n // tm, ne),
        in_specs=[pl.BlockSpec((tm, D_MODEL), lambda i, e: (i, 0)),
                  pl.BlockSpec((D_MODEL, D_MODEL), lambda i, e: (0, 0)),
                  pl.BlockSpec(sk.shape, lambda i, e: (0, 0, 0, 0)),
                  pl.BlockSpec((ec, D_MODEL), lambda i, e: (e, 0)),
                  pl.BlockSpec((D_MODEL, ec), lambda i, e: (0, e)),
                  pl.BlockSpec((1, D_MODEL), lambda i, e: (0, 0)),
                  pl.BlockSpec((1, D_MODEL), lambda i, e: (0, 0))],
        out_specs=pl.BlockSpec((tm, D_MODEL), lambda i, e: (i, 0)),
        out_shape=jax.ShapeDtypeStruct((n, D_MODEL), F32),
        scratch_shapes=[pltpu.VMEM((D_MODEL, tm), BF16), pltpu.VMEM((PEER_HEADS * PEER_DKEY, tm), BF16),
                        pltpu.VMEM((D_MODEL, tm), F32)]
                       + [pltpu.VMEM((PEER_HEADS, nk, tm), F32)] * 2
                       + [pltpu.VMEM((PEER_HEADS, nk, tm), BF16)] * 2,
        compiler_params=_cparams(("parallel", "arbitrary")),
        name="peer_ln",
    )(x, wqt, sk, u_bf, vt_bf, g, b)


def _pick(n, pref):
    t = min(n, pref)
    while n % t:
        t //= 2
    return t


def kernel(x_prompt, x_sample, cache_a_k, cache_a_v, cache_a_logf, cache_b_k, cache_b_v, cache_b_idx_k, state_conv, page_table, w_in, b_fgate, conv_w, w_o, ln1_g, ln1_b, peer_wq, peer_subkeys, peer_u, peer_v, ln2_g, ln2_b):
    bp, s, d = x_prompt.shape
    nb, t, _ = x_sample.shape
    depth = w_in.shape[0]
    npool = cache_a_k.shape[1]
    npg = page_table.shape[1]
    past = npg * PAGE_SIZE
    nk = peer_subkeys.shape[3]
    topk_p = min(TOPK_MAX, s // 4)
    topk_s = min(TOPK_MAX, (past + t) // 4)

    n_p, n_s = bp * s, nb * t
    tm_p = _pick(n_p, 512)
    tm_s = _pick(n_s, 256)
    tq_fox = _pick(s, 128)
    tk_fox = _pick(s, 512)
    tq_dsa = _pick(s, 128)
    kc_dsa = _pick(s, 512)
    tc = _pick(s, 256)
    tm_merge = _pick(s, 512)
    tm_peer_p = _pick(n_p, 256)
    tm_peer_s = _pick(n_s, 256)
    ec = _pick(nk * nk, max(2048, nk))
    pps_i = _pick(npg, 8)
    pps_a = _pick(npg, 8)

    page_t = lambda c: jnp.transpose(c, (0, 1, 3, 4, 2)).reshape(depth, npool, -1, PAGE_SIZE)
    cak, cav, cbk, cbv = page_t(cache_a_k), page_t(cache_a_v), page_t(cache_b_k), page_t(cache_b_v)
    cidx = jnp.swapaxes(cache_b_idx_k, 2, 3)
    clf = jnp.pad(jnp.swapaxes(cache_a_logf, 2, 3), ((0, 0), (0, 0), (0, 8 - A_HEADS), (0, 0)))

    xp = x_prompt.reshape(n_p, d)
    xs = x_sample.reshape(n_s, d)
    outs_p = [[] for _ in range(7)]
    outs_s = [[] for _ in range(7)]
    zero_buf = jnp.zeros((bp, CONV_W - 1, C_WIDTH), F32)
    row2 = lambda v: v.reshape(1, -1).astype(F32)

    for l in range(depth):
        wcat, bfrow = _build_wcat(w_in[l], b_fgate[l])
        wo_bf = w_o[l].astype(BF16)
        wqt = peer_wq[l].T.astype(BF16)
        sk = peer_subkeys[l].astype(BF16)
        u_bf = peer_u[l].astype(BF16)
        vt_bf = peer_v[l].T.astype(BF16)
        g1, b1, g2, b2 = row2(ln1_g[l]), row2(ln1_b[l]), row2(ln2_g[l]), row2(ln2_b[l])
        cw = conv_w[l].astype(F32)

        pr = _project(xp, wcat, bfrow, tm_p)
        r3 = lambda a: a.reshape(bp, s, a.shape[-1])
        cum, cumt = _cumsum(r3(pr["misc"]), tc)
        oa = _fox_prompt(r3(pr["qa"]), r3(pr["kab"]), r3(pr["vab"]), cum, cumt, tq_fox, tk_fox)
        ob = _dsa_prompt(r3(pr["qi"]), r3(pr["misc"]), r3(pr["miscb"]), r3(pr["qb"]), r3(pr["kbb"]),
                         r3(pr["vbb"]), tq_dsa, kc_dsa, topk_p)
        x1, buf_p = _merge(oa, ob, r3(pr["u"]), r3(pr["gb"]), r3(xp), zero_buf, cw, wo_bf, g1, b1, tm_merge)
        xp = _peer(x1.reshape(n_p, d), wqt, sk, u_bf, vt_bf, g2, b2, tm_peer_p, ec)
        for lst, val in zip(outs_p, (pr["ka"].reshape(bp, s, A_HEADS, HEAD_DIM), pr["va"].reshape(bp, s, A_HEADS, HEAD_DIM),
                                     pr["misc"][:, MISC_LF:MISC_WI].reshape(bp, s, A_HEADS),
                                     pr["kb"].reshape(bp, s, B_HEADS, HEAD_DIM), pr["vb"].reshape(bp, s, B_HEADS, HEAD_DIM),
                                     pr["misc"][:, :IDX_DIM].reshape(bp, s, IDX_DIM), buf_p)):
            lst.append(val)

        ps = _project(xs, wcat, bfrow, tm_s)
        q3 = lambda a: a.reshape(nb, t, a.shape[-1])
        new_t = lambda a: jnp.pad(jnp.swapaxes(q3(a), 1, 2), ((0, 0), (0, 0), (0, PAGE_SIZE - t)))
        ki_new = new_t(ps["misc"][:, :IDX_DIM])
        lf_new = jnp.pad(new_t(ps["misc"][:, MISC_LF:MISC_WI]), ((0, 0), (0, 8 - A_HEADS), (0, 0)))
        bias = _sample_index(page_table, q3(ps["qi"]), q3(ps["misc"]), cidx, l, ki_new, pps_i, topk_s)
        oa_s, ob_s = _sample_attn(page_table, q3(ps["qa"]), q3(ps["qb"]), bias, (cak, cav, clf, cbk, cbv), l,
                                  (new_t(ps["ka"]), new_t(ps["va"]), lf_new, new_t(ps["kb"]), new_t(ps["vb"])),
                                  pps_a)
        x1s, buf_s = _merge(oa_s, ob_s, q3(ps["u"]), q3(ps["gb"]), q3(xs), state_conv[l].astype(F32), cw, wo_bf,
                            g1, b1, t)
        xs = _peer(x1s.reshape(n_s, d), wqt, sk, u_bf, vt_bf, g2, b2, tm_peer_s, ec)
        for lst, val in zip(outs_s, (ps["ka"].reshape(nb, t, A_HEADS, HEAD_DIM), ps["va"].reshape(nb, t, A_HEADS, HEAD_DIM),
                                     ps["misc"][:, MISC_LF:MISC_WI].reshape(nb, t, A_HEADS),
                                     ps["kb"].reshape(nb, t, B_HEADS, HEAD_DIM), ps["vb"].reshape(nb, t, B_HEADS, HEAD_DIM),
                                     ps["misc"][:, :IDX_DIM].reshape(nb, t, IDX_DIM), buf_s)):
            lst.append(val)

    return ((xp.reshape(bp, s, d), xs.reshape(nb, t, d))
            + tuple(jnp.stack(v) for v in outs_p) + tuple(jnp.stack(v) for v in outs_s))
```

```python
import functools

import jax
import jax.numpy as jnp
from jax import lax
from jax.experimental import pallas as pl
from jax.experimental.pallas import tpu as pltpu

F32 = jnp.float32
BF16 = jnp.bfloat16
I32 = jnp.int32
I16 = jnp.int16
HALF16 = 2 ** 15

D_MODEL = 1024
PAGE_SIZE = 128
HEAD_DIM = 64
A_HEADS = 6
B_HEADS = 6
A_WIDTH = A_HEADS * HEAD_DIM
B_WIDTH = B_HEADS * HEAD_DIM
C_WIDTH = 256
IDX_HEADS = 8
IDX_DIM = 64
TOPK_MAX = 256
CONV_W = 3
PEER_HEADS = 8
PEER_DKEY = 128
PEER_TOPK = 16
LN_EPS = 1e-5
DEPTH = 2
DN_ALPHA = (2 * DEPTH) ** 0.25
IN_WIDTHS = (A_WIDTH, A_WIDTH, A_WIDTH, A_HEADS,
             B_WIDTH, B_WIDTH, B_WIDTH, IDX_HEADS * IDX_DIM, IDX_DIM, IDX_HEADS,
             C_WIDTH, C_WIDTH, C_WIDTH)

LANES = 128
MXU_DEPTH = 256
BF16_ROWS = 16
VMEM_LIMIT = 56 * 1024 * 1024
NEG = -1e30
INT_MIN = -2 ** 31

MISC_LF = IDX_DIM
MISC_WI = IDX_DIM + A_HEADS

O_QA, O_KA, O_VA = 0, 384, 768
O_QB, O_KB, O_VB = 1152, 1536, 1920
O_QI = 2304
O_CIN, O_GB, O_GC = 3328, 3584, 3840
O_MISC = 4096
W_TOTAL = 4224


def _nt(a, b):
    return lax.dot_general(a, b, (((1,), (1,)), ((), ())), preferred_element_type=F32)


def _cparams(sem):
    return pltpu.CompilerParams(dimension_semantics=sem, vmem_limit_bytes=VMEM_LIMIT)


def _sort_key(x):
    bits = pltpu.bitcast(x, I32)
    return bits ^ ((bits >> 31) & 0x7FFFFFFF)


def _layer_norm_rows(z, g, b):
    mu = jnp.mean(z, axis=-1, keepdims=True)
    zc = z - mu
    var = jnp.mean(zc * zc, axis=-1, keepdims=True)
    return zc * lax.rsqrt(var + LN_EPS) * g + b


def _proj_kernel(x_ref, w_ref, bf_ref, qa_ref, ka_ref, va_ref, kab_ref, vab_ref,
                 qb_ref, kb_ref, vb_ref, kbb_ref, vbb_ref, qi_ref, u_ref, gb_ref,
                 misc_ref, miscb_ref):
    xb = x_ref[...].astype(BF16)

    def seg(off, width):
        return jnp.dot(xb, w_ref[:, off:off + width], preferred_element_type=F32)

    qa_ref[...] = (seg(O_QA, A_WIDTH) * (HEAD_DIM ** -0.5)).astype(BF16)
    ka = seg(O_KA, A_WIDTH)
    ka_ref[...] = ka
    kab_ref[...] = ka.astype(BF16)
    va = seg(O_VA, A_WIDTH)
    va_ref[...] = va
    vab_ref[...] = va.astype(BF16)
    qb_ref[...] = (seg(O_QB, B_WIDTH) * (HEAD_DIM ** -0.5)).astype(BF16)
    kb = seg(O_KB, B_WIDTH)
    kb_ref[...] = kb
    kbb_ref[...] = kb.astype(BF16)
    vb = seg(O_VB, B_WIDTH)
    vb_ref[...] = vb
    vbb_ref[...] = vb.astype(BF16)
    qi_ref[...] = (seg(O_QI, IDX_HEADS * LANES) * (IDX_DIM ** -0.5)).astype(BF16)
    cin = seg(O_CIN, C_WIDTH)
    gb_ref[...] = seg(O_GB, C_WIDTH)
    gc = seg(O_GC, C_WIDTH)
    u_ref[...] = gc * cin
    misc = seg(O_MISC, LANES)
    lane = lax.broadcasted_iota(I32, misc.shape, 1)
    z = misc + bf_ref[...]
    logsig = jnp.minimum(z, 0.0) - jnp.log1p(jnp.exp(-jnp.abs(z)))
    misc = jnp.where((lane >= MISC_LF) & (lane < MISC_WI), logsig, misc)
    misc_ref[...] = misc
    miscb_ref[...] = misc.astype(BF16)


def _project(x, wcat, bfrow, tm):
    n = x.shape[0]
    row = lambda w: pl.BlockSpec((tm, w), lambda i: (i, 0))
    widths = [(A_WIDTH, BF16), (A_WIDTH, F32), (A_WIDTH, F32), (A_WIDTH, BF16), (A_WIDTH, BF16),
              (B_WIDTH, BF16), (B_WIDTH, F32), (B_WIDTH, F32), (B_WIDTH, BF16), (B_WIDTH, BF16),
              (IDX_HEADS * LANES, BF16), (C_WIDTH, F32), (C_WIDTH, F32), (LANES, F32), (LANES, BF16)]
    names = ["qa", "ka", "va", "kab", "vab", "qb", "kb", "vb", "kbb", "vbb", "qi", "u", "gb", "misc", "miscb"]
    outs = pl.pallas_call(
        _proj_kernel,
        grid=(n // tm,),
        in_specs=[row(D_MODEL),
                  pl.BlockSpec((D_MODEL, W_TOTAL), lambda i: (0, 0)),
                  pl.BlockSpec((1, LANES), lambda i: (0, 0))],
        out_specs=[row(w) for w, _ in widths],
        out_shape=[jax.ShapeDtypeStruct((n, w), dt) for w, dt in widths],
        compiler_params=_cparams(("parallel",)),
        name="proj",
    )(x, wcat, bfrow)
    return dict(zip(names, outs))


def _build_wcat(w_in, b_f):
    parts, off = [], 0
    for w in IN_WIDTHS:
        parts.append(w_in[:, off:off + w])
        off += w
    qa, ka, va, fa, qb, kb, vb, qi, ki, wi, cin, gb, gc = parts
    d = w_in.shape[0]
    qi_pad = jnp.pad(qi.reshape(d, IDX_HEADS, IDX_DIM), ((0, 0), (0, 0), (0, LANES - IDX_DIM)))
    qi_pad = qi_pad.reshape(d, IDX_HEADS * LANES)
    misc = jnp.concatenate([ki, fa, wi, jnp.zeros((d, LANES - MISC_WI - IDX_HEADS), w_in.dtype)], axis=1)
    wcat = jnp.concatenate([qa, ka, va, qb, kb, vb, qi_pad, cin, gb, gc, misc], axis=1).astype(BF16)
    bfrow = jnp.zeros((1, LANES), F32).at[0, MISC_LF:MISC_WI].set(b_f.astype(F32))
    return wcat, bfrow


def _cumsum_kernel(m_ref, cum_ref, cumt_ref, carry_ref, *, tc):
    @pl.when(pl.program_id(1) == 0)
    def _():
        carry_ref[...] = jnp.zeros_like(carry_ref)

    v = m_ref[0]
    r = lax.broadcasted_iota(I32, (tc, tc), 0)
    c = lax.broadcasted_iota(I32, (tc, tc), 1)
    tri = (c <= r).astype(F32)
    loc = jnp.dot(tri, v, precision=lax.Precision.HIGHEST, preferred_element_type=F32) + carry_ref[...]
    cum_ref[0] = loc
    cumt_ref[0] = loc.T
    carry_ref[...] = loc[tc - 1:tc, :]


def _cumsum(misc3, tc):
    b, s, _ = misc3.shape
    return pl.pallas_call(
        functools.partial(_cumsum_kernel, tc=tc),
        grid=(b, s // tc),
        in_specs=[pl.BlockSpec((1, tc, LANES), lambda i, j: (i, j, 0))],
        out_specs=[pl.BlockSpec((1, tc, LANES), lambda i, j: (i, j, 0)),
                   pl.BlockSpec((1, LANES, tc), lambda i, j: (i, 0, j))],
        out_shape=[jax.ShapeDtypeStruct((b, s, LANES), F32), jax.ShapeDtypeStruct((b, LANES, s), F32)],
        scratch_shapes=[pltpu.VMEM((1, LANES), F32)],
        compiler_params=_cparams(("parallel", "arbitrary")),
        name="logf_cumsum",
    )(misc3)


def _flash_update(carry, s, vblk, v_transposed=False):
    m, l, acc = carry
    m_new = jnp.maximum(m, jnp.max(s, axis=1, keepdims=True))
    alpha = jnp.exp(m - m_new)
    p = jnp.exp(s - m_new)
    l = alpha * l + jnp.sum(p, axis=1, keepdims=True)
    p16 = p.astype(BF16)
    pv = _nt(p16, vblk) if v_transposed else jnp.dot(p16, vblk, preferred_element_type=F32)
    return m_new, l, alpha * acc + pv


def _flash_init(rows, width):
    return (jnp.full((rows, 1), NEG, F32), jnp.zeros((rows, 1), F32), jnp.zeros((rows, width), F32))


def _split_pair(qp):
    lane = lax.broadcasted_iota(I32, qp.shape, 1)
    zero = jnp.zeros_like(qp)
    return jnp.where(lane < HEAD_DIM, qp, zero), jnp.where(lane >= HEAD_DIM, qp, zero)


def _join_pair(o0, o1):
    lane = lax.broadcasted_iota(I32, o0.shape, 1)
    return jnp.where(lane < HEAD_DIM, o0, o1)


def _pipelined_blocks(nblk, store_qk, step, state):
    store_qk(0, 0)

    def pair(jj, st):
        store_qk(1, 2 * jj + 1)
        st = step(2 * jj, st, 0, False)
        store_qk(0, 2 * jj + 2)
        return step(2 * jj + 1, st, 1, False)

    npair = (nblk - 1) // 2
    st = lax.fori_loop(0, npair, pair, state)

    def one_left(st):
        return step(nblk - 1, st, 0, True)

    def two_left(st):
        store_qk(1, nblk - 1)
        st = step(nblk - 2, st, 0, False)
        return step(nblk - 1, st, 1, True)

    return lax.cond(nblk - 2 * npair == 2, two_left, one_left, st)


def _fox_kernel(q_ref, k_ref, v_ref, cum_ref, cumt_ref, o_ref, qk_s, *, tq, tk):
    p = pl.program_id(1)
    i = pl.program_id(2)
    qs = _split_pair(q_ref[0])
    lane = lax.broadcasted_iota(I32, (tq, LANES), 1)
    cumblk = cum_ref[0]
    cqs = [jnp.sum(jnp.where(lane == MISC_LF + 2 * p + hh, cumblk, 0.0), axis=1, keepdims=True)
           for hh in range(2)]
    row_g = i * tq + lax.broadcasted_iota(I32, (tq, tk), 0)
    col_l = lax.broadcasted_iota(I32, (tq, tk), 1)

    def store_qk(slot, j):
        off = pl.multiple_of(j * tk, tk)
        kblk = k_ref[0, pl.ds(off, tk), :]
        for hh in range(2):
            qk_s[slot, hh] = _nt(qs[hh], kblk)

    def step(j, state, slot, last):
        off = pl.multiple_of(j * tk, tk)
        vblk = v_ref[0, pl.ds(off, tk), :]
        new = []
        for hh in range(2):
            ck = cumt_ref[0, pl.ds(2 * p + hh, 1), pl.ds(off, tk)]
            s = qk_s[slot, hh] + (cqs[hh] - ck)
            if last:
                s = jnp.where(off + col_l <= row_g, s, NEG)
            new.append(_flash_update(state[hh], s, vblk))
        return tuple(new)

    nblk = (i * tq) // tk + 1
    init = (_flash_init(tq, LANES), _flash_init(tq, LANES))
    (_, l0, acc0), (_, l1, acc1) = _pipelined_blocks(nblk, store_qk, step, init)
    o_ref[0] = _join_pair(acc0 / l0, acc1 / l1).astype(BF16)


def _fox_prompt(qa, kab, vab, cum, cumt, tq, tk):
    b, s, _ = qa.shape
    npairs = A_HEADS // 2
    return pl.pallas_call(
        functools.partial(_fox_kernel, tq=tq, tk=tk),
        grid=(b, npairs, s // tq),
        in_specs=[pl.BlockSpec((1, tq, LANES), lambda bb, p, i: (bb, i, p)),
                  pl.BlockSpec((1, s, LANES), lambda bb, p, i: (bb, 0, p)),
                  pl.BlockSpec((1, s, LANES), lambda bb, p, i: (bb, 0, p)),
                  pl.BlockSpec((1, tq, LANES), lambda bb, p, i: (bb, i, 0)),
                  pl.BlockSpec((1, 8, s), lambda bb, p, i: (bb, MISC_LF // 8, 0))],
        out_specs=pl.BlockSpec((1, tq, LANES), lambda bb, p, i: (bb, i, p)),
        out_shape=jax.ShapeDtypeStruct((b, s, A_WIDTH), BF16),
        scratch_shapes=[pltpu.VMEM((2, 2, tq, tk), F32)],
        compiler_params=_cparams(("parallel", "parallel", "arbitrary")),
        name="fox_prompt",
    )(qa, kab, vab, cum, cumt)


def _kth_threshold(count_ge, rows, k, total):
    def cond(state):
        it, _, cnt = state
        return (it < 32) & (jnp.max(cnt) > k)

    def body(state):
        it, t, cnt = state
        cand = t | lax.shift_left(jnp.int32(1), 31 - it)
        c = count_ge(cand ^ INT_MIN)
        ok = c >= k
        return it + 1, jnp.where(ok, cand, t), jnp.where(ok, c, cnt)

    init = (jnp.int32(0), jnp.zeros((rows, 1), I32), jnp.full((rows, 1), total, I32))
    _, t, cnt = lax.while_loop(cond, body, init)
    return t ^ INT_MIN, cnt


def _tie_cutoff(count_tie_below, rows, need, nbits):
    def body(it, j):
        cand = j | lax.shift_left(jnp.int32(1), nbits - 1 - it)
        cnt = count_tie_below(cand)
        return jnp.where(cnt <= need, cand, j)
    return lax.fori_loop(0, nbits, body, jnp.zeros((rows, 1), I32))


def _dsa_kernel(qi_ref, misc_ref, kib_ref, qb_ref, kb_ref, vb_ref, o_ref, key_ref, bias_ref, qk_s,
                khi_ref, klo_ref, *, tq, kc, kcc, topk, nbits):
    i = pl.program_id(1)
    nch = (i * tq + tq + kc - 1) // kc
    ncc = (i * tq + tq + kcc - 1) // kcc
    nch1 = ncc * (kcc // kc)
    w8 = misc_ref[0][:, MISC_WI:MISC_WI + IDX_HEADS] * (IDX_HEADS ** -0.5)
    row_g = i * tq + lax.broadcasted_iota(I32, (tq, kc), 0)
    col_l = lax.broadcasted_iota(I32, (tq, kc), 1)
    col_c = lax.broadcasted_iota(I32, (tq, kcc), 1)

    def p1(c, carry):
        off = pl.multiple_of(c * kc, kc)
        kib = kib_ref[0, pl.ds(off, kc), :]
        acc = jnp.zeros((tq, kc), F32)
        for h in range(IDX_HEADS):
            r = _nt(qi_ref[0, :, h * LANES:(h + 1) * LANES], kib)
            acc = acc + w8[:, h:h + 1] * jnp.maximum(r, 0.0)
        acc = jnp.where(acc == 0.0, 0.0, acc)
        sc = jnp.where(off + col_l <= row_g, acc, -jnp.inf)
        key = _sort_key(sc)
        key_ref[:, pl.ds(off, kc)] = key
        khi_ref[:, pl.ds(off, kc)] = (key >> 16).astype(I16)
        return carry

    lax.fori_loop(0, nch1, p1, 0)

    def lane_groups(x):
        tot = x[:, 0:LANES]
        for g in range(1, kcc // LANES):
            tot = tot + x[:, g * LANES:(g + 1) * LANES]
        return tot

    def count(pred):
        def body(c, cnt):
            off = pl.multiple_of(c * kcc, kcc)
            return cnt + lane_groups(pred(key_ref[:, pl.ds(off, kcc)], off).astype(I32))
        cnt = lax.fori_loop(0, ncc, body, jnp.zeros((tq, LANES), I32))
        return jnp.sum(cnt, axis=1, keepdims=True)

    def count16(ref16, pred):
        def body(c, cnt):
            off = pl.multiple_of(c * kcc, kcc)
            hit = jnp.where(pred(ref16[:, pl.ds(off, kcc)]), jnp.int16(1), jnp.int16(0))
            return cnt + lane_groups(hit)
        cnt = lax.fori_loop(0, ncc, body, jnp.zeros((tq, LANES), I16))
        return jnp.sum(cnt.astype(I32), axis=1, keepdims=True)

    def bit_step(t, cnt, it, count_ge16, base):
        cand = t | lax.shift_left(jnp.int32(1), 15 - it)
        c = base + count_ge16((cand - HALF16).astype(I16))
        ok = c >= topk
        return jnp.where(ok, cand, t), jnp.where(ok, c, cnt)

    zero = jnp.zeros((tq, 1), I32)
    t_hi, n_hi_ge = lax.fori_loop(
        0, 16,
        lambda it, st: bit_step(st[0], st[1], it, lambda th: count16(khi_ref, lambda kk: kk >= th), 0),
        (zero, jnp.full((tq, 1), ncc * kcc, I32)))
    th_s = t_hi - HALF16

    def low_half():
        th16 = th_s.astype(I16)
        n_hi_gt = count16(khi_ref, lambda kk: kk > th16)

        def build(c, carry):
            off = pl.multiple_of(c * kcc, kcc)
            lo = ((key_ref[:, pl.ds(off, kcc)] & 0xFFFF) - HALF16).astype(I16)
            klo_ref[:, pl.ds(off, kcc)] = jnp.where(khi_ref[:, pl.ds(off, kcc)] == th16, lo, jnp.int16(-HALF16))
            return carry

        lax.fori_loop(0, ncc, build, 0)

        def cond(st):
            it, _, cnt = st
            return (it < 16) & (jnp.max(cnt) > topk)

        def body(st):
            it, t, cnt = st
            t, cnt = bit_step(t, cnt, it, lambda th: count16(klo_ref, lambda kk: kk >= th), n_hi_gt)
            return it + 1, t, cnt

        _, t_lo, cnt = lax.while_loop(cond, body, (jnp.int32(0), zero, n_hi_ge))
        return t_lo, cnt

    t_lo, n_ge = lax.cond(jnp.max(n_hi_ge) > topk, low_half, lambda: (zero, n_hi_ge))
    thr = lax.shift_left(th_s, 16) | t_lo
    big = jnp.full((tq, 1), 2 ** nbits - 1, I32)

    def tie_cutoff():
        need = topk - count(lambda kk, off: kk > thr)
        return _tie_cutoff(lambda j: count(lambda kk, off: (kk == thr) & (off + col_c < j)), tq, need, nbits)

    jstar = lax.cond(jnp.max(n_ge) > topk, tie_cutoff, lambda: big)

    def p2(c, carry):
        off = pl.multiple_of(c * kc, kc)
        kk = key_ref[:, pl.ds(off, kc)]
        colg = off + col_l
        sel = (kk > thr) | ((kk == thr) & (colg < jstar))
        bias_ref[:, pl.ds(off, kc)] = jnp.where(sel & (colg <= row_g), 0.0, NEG)
        return carry

    lax.fori_loop(0, nch, p2, 0)

    for pr in range(B_HEADS // 2):
        qs = _split_pair(qb_ref[0, :, pr * LANES:(pr + 1) * LANES])

        def store_qk(slot, c):
            off = pl.multiple_of(c * kc, kc)
            kblk = kb_ref[0, pl.ds(off, kc), pr * LANES:(pr + 1) * LANES]
            for hh in range(2):
                qk_s[slot, hh] = _nt(qs[hh], kblk)

        def step(c, state, slot, last):
            off = pl.multiple_of(c * kc, kc)
            vblk = vb_ref[0, pl.ds(off, kc), pr * LANES:(pr + 1) * LANES]
            bias = bias_ref[:, pl.ds(off, kc)]
            return tuple(_flash_update(state[hh], qk_s[slot, hh] + bias, vblk) for hh in range(2))

        init = (_flash_init(tq, LANES), _flash_init(tq, LANES))
        (_, l0, acc0), (_, l1, acc1) = _pipelined_blocks(nch, store_qk, step, init)
        o_ref[0, :, pr * LANES:(pr + 1) * LANES] = _join_pair(acc0 / l0, acc1 / l1).astype(BF16)


def _dsa_prompt(qi, misc, miscb, qb, kbb, vbb, tq, kc, topk):
    b, s, _ = qb.shape
    nbits = max(1, (s - 1).bit_length()) + 1
    kcc = 2 * kc if s % (2 * kc) == 0 else kc
    qspec = lambda w: pl.BlockSpec((1, tq, w), lambda bb, i: (bb, i, 0))
    full = lambda w: pl.BlockSpec((1, s, w), lambda bb, i: (bb, 0, 0))
    return pl.pallas_call(
        functools.partial(_dsa_kernel, tq=tq, kc=kc, kcc=kcc, topk=topk, nbits=nbits),
        grid=(b, s // tq),
        in_specs=[qspec(IDX_HEADS * LANES), qspec(LANES), full(LANES), qspec(B_WIDTH), full(B_WIDTH), full(B_WIDTH)],
        out_specs=qspec(B_WIDTH),
        out_shape=jax.ShapeDtypeStruct((b, s, B_WIDTH), BF16),
        scratch_shapes=[pltpu.VMEM((tq, s), I32), pltpu.VMEM((tq, s), F32), pltpu.VMEM((2, 2, tq, kc), F32),
                        pltpu.VMEM((tq, s), I16), pltpu.VMEM((tq, s), I16)],
        compiler_params=_cparams(("parallel", "arbitrary")),
        name="dsa_prompt",
    )(qi, misc, miscb, qb, kbb, vbb)


def _sidx_kernel(pt_ref, qi_ref, misc_ref, *rest, pps, nps, past, topk, nbits):
    page_refs = rest[:pps]
    knew_ref, bias_ref, key_ref, qst_ref, wst_ref = rest[pps:]
    j = pl.program_id(1)
    t = qi_ref.shape[1]
    lp = key_ref.shape[1]

    @pl.when(j == 0)
    def _():
        qf = qi_ref[0].astype(F32)
        qst_ref[...] = jnp.concatenate([qf[:, h * LANES:h * LANES + IDX_DIM] for h in range(IDX_HEADS)],
                                       axis=0).astype(BF16)
        w8 = misc_ref[0][:, MISC_WI:MISC_WI + IDX_HEADS] * (IDX_HEADS ** -0.5)
        wst_ref[...] = jnp.concatenate([w8[:, h:h + 1] for h in range(IDX_HEADS)], axis=0)

    def score(kt):
        r = jnp.dot(qst_ref[...], kt.astype(BF16), preferred_element_type=F32)
        r = wst_ref[...] * jnp.maximum(r, 0.0)
        acc = r[0:t, :]
        for h in range(1, IDX_HEADS):
            acc = acc + r[h * t:(h + 1) * t, :]
        return jnp.where(acc == 0.0, 0.0, acc)

    @pl.when(j < nps)
    def _():
        kt = jnp.concatenate([page_refs[k][0, 0] for k in range(pps)], axis=1)
        off = pl.multiple_of(j * (pps * PAGE_SIZE), pps * PAGE_SIZE)
        key_ref[:, pl.ds(off, pps * PAGE_SIZE)] = _sort_key(score(kt))

    @pl.when(j == nps)
    def _():
        rowi = lax.broadcasted_iota(I32, (t, PAGE_SIZE), 0)
        coli = lax.broadcasted_iota(I32, (t, PAGE_SIZE), 1)
        sc = jnp.where(coli <= rowi, score(knew_ref[0]), -jnp.inf)
        key_ref[:, past:past + PAGE_SIZE] = _sort_key(sc)

        keys = key_ref[...]
        colg = lax.broadcasted_iota(I32, (t, lp), 1)
        count = lambda pred: jnp.sum(pred.astype(I32), axis=1, keepdims=True)
        thr, _ = _kth_threshold(lambda th: count(keys >= th), t, topk, lp)
        need = topk - count(keys > thr)
        jstar = _tie_cutoff(lambda jj: count((keys == thr) & (colg < jj)), t, need, nbits)
        sel = (keys > thr) | ((keys == thr) & (colg < jstar))
        rowg = past + lax.broadcasted_iota(I32, (t, lp), 0)
        bias_ref[0] = jnp.where(sel & (colg <= rowg), 0.0, NEG)


def _sample_index(page_table, qi3, misc3, cache_idx_t, layer, knew_t, pps, topk):
    nb, t, _ = qi3.shape
    npg = page_table.shape[1]
    nps = npg // pps
    past = npg * PAGE_SIZE
    lp = past + PAGE_SIZE
    nbits = lp.bit_length() + 1

    def page_map(k):
        return lambda b, j, pt: (layer, pt[b * npg + jnp.minimum(j, nps - 1) * pps + k], 0, 0)

    bmap = lambda b, j, pt: (b, 0, 0)
    grid_spec = pltpu.PrefetchScalarGridSpec(
        num_scalar_prefetch=1,
        grid=(nb, nps + 1),
        in_specs=[pl.BlockSpec((1, t, IDX_HEADS * LANES), bmap),
                  pl.BlockSpec((1, t, LANES), bmap)]
                 + [pl.BlockSpec((1, 1, IDX_DIM, PAGE_SIZE), page_map(k)) for k in range(pps)]
                 + [pl.BlockSpec((1, IDX_DIM, PAGE_SIZE), bmap)],
        out_specs=pl.BlockSpec((1, t, lp), bmap),
        scratch_shapes=[pltpu.VMEM((t, lp), I32), pltpu.VMEM((IDX_HEADS * t, IDX_DIM), BF16),
                        pltpu.VMEM((IDX_HEADS * t, 1), F32)],
    )
    return pl.pallas_call(
        functools.partial(_sidx_kernel, pps=pps, nps=nps, past=past, topk=topk, nbits=nbits),
        grid_spec=grid_spec,
        out_shape=jax.ShapeDtypeStruct((nb, t, lp), F32),
        compiler_params=_cparams(("parallel", "arbitrary")),
        name="sample_index",
    )(page_table.reshape(-1), qi3, misc3, *([cache_idx_t] * pps), knew_t)


def _sattn_kernel(pt_ref, qa_ref, qb_ref, bias_ref, *rest, pps, nps):
    grp = lambda g: rest[g * pps:(g + 1) * pps]
    ak_refs, av_refs, lf_refs, bk_refs, bv_refs = (grp(g) for g in range(5))
    (akn_ref, avn_ref, lfn_ref, bkn_ref, bvn_ref, oa_ref, ob_ref,
     qbd_a, qbd_b, ma, la, acca, mb, lb, accb, carry_ref) = rest[5 * pps:]
    j = pl.program_id(1)
    t = qa_ref.shape[1]
    rows = A_HEADS * t
    lane_q = lax.broadcasted_iota(I32, (t, A_WIDTH), 1)

    def block_diag(q):
        qf = q.astype(F32)
        parts = [jnp.where((lane_q >= h * HEAD_DIM) & (lane_q < (h + 1) * HEAD_DIM), qf, 0.0)
                 for h in range(A_HEADS)]
        return jnp.concatenate(parts, axis=0).astype(BF16)

    @pl.when(j == 0)
    def _():
        qbd_a[...] = block_diag(qa_ref[0])
        qbd_b[...] = block_diag(qb_ref[0])
        for r in (ma, mb):
            r[...] = jnp.full(r.shape, NEG, F32)
        for r in (la, lb, acca, accb, carry_ref):
            r[...] = jnp.zeros(r.shape, F32)

    r_i = lax.broadcasted_iota(I32, (PAGE_SIZE, PAGE_SIZE), 0)
    c_i = lax.broadcasted_iota(I32, (PAGE_SIZE, PAGE_SIZE), 1)
    triu = (r_i <= c_i).astype(F32)

    def cat(refs, dtype):
        pages = [r[0, 0] if len(r.shape) == 4 else r[0] for r in refs]
        x = pages[0] if len(pages) == 1 else jnp.concatenate(pages, axis=1)
        return x.astype(dtype)

    def update(m_ref, l_ref, acc_ref, s, vt):
        m, l, acc = _flash_update((m_ref[...], l_ref[...], acc_ref[...]), s, vt, v_transposed=True)
        m_ref[...] = m
        l_ref[...] = l
        acc_ref[...] = acc

    def process(ak, av, lf, bk, bv, bias8, mask):
        run = carry_ref[...]
        cls = []
        for r in lf:
            page = r[0, 0] if len(r.shape) == 4 else r[0]
            cl = jnp.dot(page, triu, precision=lax.Precision.HIGHEST, preferred_element_type=F32) + run
            run = jnp.broadcast_to(cl[:, PAGE_SIZE - 1:PAGE_SIZE], cl.shape)
            cls.append(cl)
        carry_ref[...] = run
        ck = cls[0] if len(cls) == 1 else jnp.concatenate(cls, axis=1)
        width = ck.shape[1]
        ck_rows = jnp.concatenate([jnp.broadcast_to(ck[h:h + 1, :], (t, width)) for h in range(A_HEADS)], axis=0)
        sa = jnp.dot(qbd_a[...], cat(ak, BF16), preferred_element_type=F32) - ck_rows
        if mask is not None:
            sa = jnp.where(mask, sa, NEG)
        update(ma, la, acca, sa, cat(av, BF16))
        sb = jnp.dot(qbd_b[...], cat(bk, BF16), preferred_element_type=F32)
        sb = sb + jnp.concatenate([bias8] * B_HEADS, axis=0)
        update(mb, lb, accb, sb, cat(bv, BF16))

    @pl.when(j < nps)
    def _():
        off = pl.multiple_of(j * (pps * PAGE_SIZE), pps * PAGE_SIZE)
        process(ak_refs, av_refs, lf_refs, bk_refs, bv_refs, bias_ref[0, :, pl.ds(off, pps * PAGE_SIZE)], None)

    @pl.when(j == nps)
    def _():
        past = nps * pps * PAGE_SIZE
        rowi = lax.broadcasted_iota(I32, (rows, PAGE_SIZE), 0)
        coli = lax.broadcasted_iota(I32, (rows, PAGE_SIZE), 1)
        mask = coli <= (rowi & (t - 1))
        process([akn_ref], [avn_ref], [lfn_ref], [bkn_ref], [bvn_ref],
                bias_ref[0, :, past:past + PAGE_SIZE], mask)

        def gather_heads(acc_ref, l_ref):
            o = jnp.zeros((t, A_WIDTH), F32)
            for h in range(A_HEADS):
                blk = acc_ref[h * t:(h + 1) * t, :] / l_ref[h * t:(h + 1) * t, :]
                o = jnp.where((lane_q >= h * HEAD_DIM) & (lane_q < (h + 1) * HEAD_DIM), blk, o)
            return o.astype(BF16)

        oa_ref[0] = gather_heads(acca, la)
        ob_ref[0] = gather_heads(accb, lb)


def _sample_attn(page_table, qa3, qb3, bias, caches, layer, news, pps):
    nb, t, _ = qa3.shape
    assert t & (t - 1) == 0
    npg = page_table.shape[1]
    nps = npg // pps
    lp = bias.shape[2]
    rows = A_HEADS * t

    def page_map(k):
        return lambda b, j, pt: (layer, pt[b * npg + jnp.minimum(j, nps - 1) * pps + k], 0, 0)

    bmap = lambda b, j, pt: (b, 0, 0)
    cache_specs, cache_args = [], []
    for c in caches:
        blk = (1, 1) + c.shape[2:]
        for k in range(pps):
            cache_specs.append(pl.BlockSpec(blk, page_map(k)))
            cache_args.append(c)
    new_specs = [pl.BlockSpec((1,) + a.shape[1:], bmap) for a in news]
    grid_spec = pltpu.PrefetchScalarGridSpec(
        num_scalar_prefetch=1,
        grid=(nb, nps + 1),
        in_specs=[pl.BlockSpec((1, t, A_WIDTH), bmap), pl.BlockSpec((1, t, B_WIDTH), bmap),
                  pl.BlockSpec((1, t, lp), bmap)] + cache_specs + new_specs,
        out_specs=[pl.BlockSpec((1, t, A_WIDTH), bmap), pl.BlockSpec((1, t, B_WIDTH), bmap)],
        scratch_shapes=[pltpu.VMEM((rows, A_WIDTH), BF16), pltpu.VMEM((rows, B_WIDTH), BF16),
                        pltpu.VMEM((rows, 1), F32), pltpu.VMEM((rows, 1), F32), pltpu.VMEM((rows, A_WIDTH), F32),
                        pltpu.VMEM((rows, 1), F32), pltpu.VMEM((rows, 1), F32), pltpu.VMEM((rows, B_WIDTH), F32),
                        pltpu.VMEM((8, PAGE_SIZE), F32)],
    )
    return pl.pallas_call(
        functools.partial(_sattn_kernel, pps=pps, nps=nps),
        grid_spec=grid_spec,
        out_shape=[jax.ShapeDtypeStruct((nb, t, A_WIDTH), BF16), jax.ShapeDtypeStruct((nb, t, B_WIDTH), BF16)],
        compiler_params=_cparams(("parallel", "arbitrary")),
        name="sample_attn",
    )(page_table.reshape(-1), qa3, qb3, bias, *cache_args, *news)


def _merge_kernel(oa_ref, ob_ref, u_ref, gb_ref, x_ref, buf_ref, cw_ref, wo_ref, g_ref, b_ref,
                  y_ref, nbuf_ref, carry_ref, *, tm):
    j = pl.program_id(1)

    @pl.when(j == 0)
    def _():
        carry_ref[...] = buf_ref[0]

    u = u_ref[0]
    rowi = lax.broadcasted_iota(I32, u.shape, 0)
    c0 = carry_ref[0:1, :]
    c1 = carry_ref[1:2, :]
    u1 = jnp.where(rowi == 0, c1, pltpu.roll(u, 1, 0))
    u2 = jnp.where(rowi == 0, c0, jnp.where(rowi == 1, c1, pltpu.roll(u, 2, 0)))
    yc = cw_ref[0:1, :] * u2 + cw_ref[1:2, :] * u1 + cw_ref[2:3, :] * u
    oc = (gb_ref[0] * yc).astype(BF16)
    new_carry = u[tm - 2:tm, :]
    carry_ref[...] = new_carry
    nbuf_ref[0] = new_carry
    mix = jnp.dot(oa_ref[0], wo_ref[0:A_WIDTH, :], preferred_element_type=F32)
    mix = mix + jnp.dot(ob_ref[0], wo_ref[A_WIDTH:A_WIDTH + B_WIDTH, :], preferred_element_type=F32)
    mix = mix + jnp.dot(oc, wo_ref[A_WIDTH + B_WIDTH:, :], preferred_element_type=F32)
    y_ref[0] = _layer_norm_rows(DN_ALPHA * x_ref[0] + mix, g_ref[...], b_ref[...])


def _merge(oa, ob, u, gb, x, buf, conv_w, wo_bf, g, b, tm):
    nseq, t, _ = x.shape
    blk = lambda w: pl.BlockSpec((1, tm, w), lambda s, j: (s, j, 0))
    const = lambda r, w: pl.BlockSpec((r, w), lambda s, j: (0, 0))
    return pl.pallas_call(
        functools.partial(_merge_kernel, tm=tm),
        grid=(nseq, t // tm),
        in_specs=[blk(A_WIDTH), blk(B_WIDTH), blk(C_WIDTH), blk(C_WIDTH), blk(D_MODEL),
                  pl.BlockSpec((1, CONV_W - 1, C_WIDTH), lambda s, j: (s, 0, 0)),
                  const(CONV_W, C_WIDTH), const(D_MODEL, D_MODEL), const(1, D_MODEL), const(1, D_MODEL)],
        out_specs=[blk(D_MODEL), pl.BlockSpec((1, CONV_W - 1, C_WIDTH), lambda s, j: (s, 0, 0))],
        out_shape=[jax.ShapeDtypeStruct((nseq, t, D_MODEL), F32),
                   jax.ShapeDtypeStruct((nseq, CONV_W - 1, C_WIDTH), F32)],
        scratch_shapes=[pltpu.VMEM((CONV_W - 1, C_WIDTH), F32)],
        compiler_params=_cparams(("parallel", "arbitrary")),
        name="merge_ln",
    )(oa, ob, u, gb, x, buf, conv_w, wo_bf, g, b)


def _top16_rows(s_list):
    nk, tm = s_list[0].shape

    def extract(s, first_occurrence):
        idx = lax.broadcasted_iota(I32, (nk, tm), 0).astype(F32)
        work = s
        rank = jnp.full((nk, tm), 99.0, F32)
        vals = []
        for r in range(PEER_TOPK):
            m = jnp.max(work, axis=0, keepdims=True)
            hit = work == m
            if first_occurrence:
                hit = idx == jnp.min(jnp.where(hit, idx, 1e9), axis=0, keepdims=True)
            rank = jnp.where(hit, float(r), rank)
            work = jnp.where(hit, -jnp.inf, work)
            vals.append(m)
        return jnp.concatenate(vals, axis=0), rank

    fast = [extract(s, False) for s in s_list]
    bad = jnp.zeros((1, tm), F32)
    for _, rank in fast:
        nsel = jnp.sum(jnp.where(rank < 99.0, 1.0, 0.0), axis=0, keepdims=True)
        bad = jnp.maximum(bad, jnp.where(nsel != float(PEER_TOPK), 1.0, 0.0))
    flat = lambda pairs: tuple(x for pair in pairs for x in pair)
    out = lax.cond(jnp.max(bad) > 0.0, lambda: flat([extract(s, True) for s in s_list]), lambda: flat(fast))
    return [(out[2 * i], out[2 * i + 1]) for i in range(len(s_list))]


def _peer_kernel(x_ref, wqt_ref, sk_ref, u_ref, vt_ref, g_ref, b_ref, o_ref,
                 xt_s, qt_s, outt_s, a_s, n1_s, bb_s, r2_s, *, tm, ec, nk):
    e = pl.program_id(1)
    ne = pl.num_programs(1)
    k = PEER_TOPK
    half = PEER_DKEY // 2

    @pl.when(e == 0)
    def _():
        xt = x_ref[...].T.astype(BF16)
        xt_s[...] = xt
        qt_s[...] = jnp.dot(wqt_ref[...], xt, preferred_element_type=F32).astype(BF16)
        rho = lax.broadcasted_iota(I32, (80, tm), 0)
        mid = rho - 16
        r1 = jnp.where(rho < 16, rho, jnp.where(rho < 72, mid & 7, 0))
        r2 = jnp.where(rho < 16, 0, jnp.where(rho < 72, (mid >> 3) + 1, rho - 64))
        pos = (r1 * k + r2).astype(F32)
        valid = (r1 + 1) * (r2 + 1) <= k
        row8 = lax.broadcasted_iota(I32, (8, tm), 0)
        def head(h, carry):
            q1 = qt_s[pl.ds(pl.multiple_of(2 * h * half, half), half), :]
            q2 = qt_s[pl.ds(pl.multiple_of((2 * h + 1) * half, half), half), :]
            s1 = jnp.dot(sk_ref[h, 0], q1, preferred_element_type=F32)
            s2 = jnp.dot(sk_ref[h, 1], q2, preferred_element_type=F32)
            (v1, rank1), (v2, rank2) = _top16_rows([s1, s2])
            slabs = [v1 + v2[0:1, :]]
            for j in range(1, 8):
                slabs.append(v1[0:8, :] + v2[j:j + 1, :])
            slabs.append(v1[0:1, :] + v2[8:16, :])
            cand = jnp.where(valid, jnp.concatenate(slabs, axis=0), -jnp.inf)
            cmax = v1[0:1, :] + v2[0:1, :]
            sel = jnp.zeros((80, tm), F32)
            z = jnp.zeros((1, tm), F32)
            for r in range(k):
                m = jnp.max(cand, axis=0, keepdims=True)
                first = jnp.min(jnp.where(cand == m, pos, 1e9), axis=0, keepdims=True)
                hit = pos == first
                sel = jnp.where(hit, 1.0, sel)
                cand = jnp.where(hit, -jnp.inf, cand)
                z = z + jnp.exp(m - cmax)
            top8 = sel[0:8, :]
            for j in range(1, 8):
                top8 = top8 + sel[8 + 8 * j:16 + 8 * j, :]
            extra = jnp.sum(sel[72:80, :], axis=0, keepdims=True)
            top8 = top8 + jnp.where(row8 == 0, extra, 0.0)
            ncount = jnp.concatenate([top8, sel[8:16, :]], axis=0)
            n1 = jnp.zeros((nk, tm), F32)
            for r in range(k):
                n1 = jnp.where(rank1 == float(r), ncount[r:r + 1, :], n1)
            a_s[h] = jnp.exp(s1 - v1[0:1, :])
            n1_s[h] = n1
            bb_s[h] = (jnp.exp(s2 - v2[0:1, :]) / z).astype(BF16)
            r2_s[h] = rank2.astype(BF16)
            return carry

        lax.fori_loop(0, PEER_HEADS, head, 0)
        outt_s[...] = jnp.zeros_like(outt_s)

    per = ec // nk
    gps = max(1, min(per, MXU_DEPTH // nk))
    sl = gps * nk
    xt = xt_s[...]
    acc = None
    nsl = ec // sl

    def rows16(row):
        if nk % BF16_ROWS:
            return jnp.broadcast_to(row, (nk, tm)).astype(BF16)
        tile = jnp.broadcast_to(row, (BF16_ROWS, tm)).astype(BF16)
        return jnp.concatenate([tile] * (nk // BF16_ROWS), axis=0) if nk > BF16_ROWS else tile

    hidden = lambda sb: jnp.dot(u_ref[sb * sl:(sb + 1) * sl, :], xt, preferred_element_type=F32)
    ht_next = hidden(0)
    for sb in range(nsl):
        ht = ht_next
        if sb + 1 < nsl:
            ht_next = hidden(sb + 1)
        act = (0.5 * ht * (1.0 + lax.erf(ht * (0.5 ** 0.5)))).astype(BF16)
        coefs = []
        for g in range(gps):
            i1 = e * per + sb * gps + g
            gate = jnp.zeros((nk, tm), BF16)
            for h in range(PEER_HEADS):
                arow = rows16(a_s[h, pl.ds(i1, 1), :])
                nrow = rows16(n1_s[h, pl.ds(i1, 1), :])
                gate = gate + jnp.where(r2_s[h] < nrow, bb_s[h], jnp.zeros((), BF16)) * arow
            coefs.append(gate * act[g * nk:(g + 1) * nk, :])
        coef = jnp.concatenate(coefs, axis=0) if gps > 1 else coefs[0]
        part = jnp.dot(vt_ref[:, sb * sl:(sb + 1) * sl], coef, preferred_element_type=F32)
        acc = part if acc is None else acc + part
    outt_s[...] += acc

    @pl.when(e == ne - 1)
    def _():
        y = outt_s[...].T
        o_ref[...] = _layer_norm_rows(DN_ALPHA * x_ref[...] + y, g_ref[...], b_ref[...])


def _peer(x, wqt, sk, u_bf, vt_bf, g, b, tm, ec):
    n = x.shape[0]
    nk = sk.shape[2]
    ne = (nk * nk) // ec
    return pl.pallas_call(
        functools.partial(_peer_kernel, tm=tm, ec=ec, nk=nk),
        grid=(n // tm, ne),
        in_specs=[pl.BlockSpec((tm, D_MODEL), lambda i, e: (i, 0)),
                  pl.BlockSpec((D_MODEL, D_MODEL), lambda i, e: (0, 0)),
                  pl.BlockSpec(sk.shape, lambda i, e: (0, 0, 0, 0)),
                  pl.BlockSpec((ec, D_MODEL), lambda i, e: (e, 0)),
                  pl.BlockSpec((D_MODEL, ec), lambda i, e: (0, e)),
                  pl.BlockSpec((1, D_MODEL), lambda i, e: (0, 0)),
                  pl.BlockSpec((1, D_MODEL), lambda i, e: (0, 0))],
        out_specs=pl.BlockSpec((tm, D_MODEL), lambda i, e: (i, 0)),
        out_shape=jax.ShapeDtypeStruct((n, D_MODEL), F32),
        scratch_shapes=[pltpu.VMEM((D_MODEL, tm), BF16), pltpu.VMEM((PEER_HEADS * PEER_DKEY, tm), BF16),
                        pltpu.VMEM((D_MODEL, tm), F32)]
                       + [pltpu.VMEM((PEER_HEADS, nk, tm), F32)] * 2
                       + [pltpu.VMEM((PEER_HEADS, nk, tm), BF16)] * 2,
        compiler_params=_cparams(("parallel", "arbitrary")),
        name="peer_ln",
    )(x, wqt, sk, u_bf, vt_bf, g, b)


def _pick(n, pref):
    t = min(n, pref)
    while n % t:
        t //= 2
    return t


def kernel(x_prompt, x_sample, cache_a_k, cache_a_v, cache_a_logf, cache_b_k, cache_b_v, cache_b_idx_k, state_conv, page_table, w_in, b_fgate, conv_w, w_o, ln1_g, ln1_b, peer_wq, peer_subkeys, peer_u, peer_v, ln2_g, ln2_b):
    bp, s, d = x_prompt.shape
    nb, t, _ = x_sample.shape
    depth = w_in.shape[0]
    npool = cache_a_k.shape[1]
    npg = page_table.shape[1]
    past = npg * PAGE_SIZE
    nk = peer_subkeys.shape[3]
    topk_p = min(TOPK_MAX, s // 4)
    topk_s = min(TOPK_MAX, (past + t) // 4)

    n_p, n_s = bp * s, nb * t
    tm_p = _pick(n_p, 512)
    tm_s = _pick(n_s, 256)
    tq_fox = _pick(s, 128)
    tk_fox = _pick(s, 512)
    tq_dsa = _pick(s, 128)
    kc_dsa = _pick(s, 512)
    tc = _pick(s, 256)
    tm_merge = _pick(s, 512)
    tm_peer_p = _pick(n_p, 256)
    tm_peer_s = _pick(n_s, 256)
    ec = _pick(nk * nk, max(2048, nk))
    pps_i = _pick(npg, 8)
    pps_a = _pick(npg, 16)

    page_t = lambda c: jnp.transpose(c, (0, 1, 3, 4, 2)).reshape(depth, npool, -1, PAGE_SIZE)
    cak, cav, cbk, cbv = page_t(cache_a_k), page_t(cache_a_v), page_t(cache_b_k), page_t(cache_b_v)
    cidx = jnp.swapaxes(cache_b_idx_k, 2, 3)
    clf = jnp.pad(jnp.swapaxes(cache_a_logf, 2, 3), ((0, 0), (0, 0), (0, 8 - A_HEADS), (0, 0)))

    xp = x_prompt.reshape(n_p, d)
    xs = x_sample.reshape(n_s, d)
    outs_p = [[] for _ in range(7)]
    outs_s = [[] for _ in range(7)]
    zero_buf = jnp.zeros((bp, CONV_W - 1, C_WIDTH), F32)
    row2 = lambda v: v.reshape(1, -1).astype(F32)

    for l in range(depth):
        wcat, bfrow = _build_wcat(w_in[l], b_fgate[l])
        wo_bf = w_o[l].astype(BF16)
        wqt = peer_wq[l].T.astype(BF16)
        sk = peer_subkeys[l].astype(BF16)
        u_bf = peer_u[l].astype(BF16)
        vt_bf = peer_v[l].T.astype(BF16)
        g1, b1, g2, b2 = row2(ln1_g[l]), row2(ln1_b[l]), row2(ln2_g[l]), row2(ln2_b[l])
        cw = conv_w[l].astype(F32)

        pr = _project(xp, wcat, bfrow, tm_p)
        r3 = lambda a: a.reshape(bp, s, a.shape[-1])
        cum, cumt = _cumsum(r3(pr["misc"]), tc)
        oa = _fox_prompt(r3(pr["qa"]), r3(pr["kab"]), r3(pr["vab"]), cum, cumt, tq_fox, tk_fox)
        ob = _dsa_prompt(r3(pr["qi"]), r3(pr["misc"]), r3(pr["miscb"]), r3(pr["qb"]), r3(pr["kbb"]),
                         r3(pr["vbb"]), tq_dsa, kc_dsa, topk_p)
        x1, buf_p = _merge(oa, ob, r3(pr["u"]), r3(pr["gb"]), r3(xp), zero_buf, cw, wo_bf, g1, b1, tm_merge)
        xp = _peer(x1.reshape(n_p, d), wqt, sk, u_bf, vt_bf, g2, b2, tm_peer_p, ec)
        for lst, val in zip(outs_p, (pr["ka"].reshape(bp, s, A_HEADS, HEAD_DIM), pr["va"].reshape(bp, s, A_HEADS, HEAD_DIM),
                                     pr["misc"][:, MISC_LF:MISC_WI].reshape(bp, s, A_HEADS),
                                     pr["kb"].reshape(bp, s, B_HEADS, HEAD_DIM), pr["vb"].reshape(bp, s, B_HEADS, HEAD_DIM),
                                     pr["misc"][:, :IDX_DIM].reshape(bp, s, IDX_DIM), buf_p)):
            lst.append(val)

        ps = _project(xs, wcat, bfrow, tm_s)
        q3 = lambda a: a.reshape(nb, t, a.shape[-1])
        new_t = lambda a: jnp.pad(jnp.swapaxes(q3(a), 1, 2), ((0, 0), (0, 0), (0, PAGE_SIZE - t)))
        ki_new = new_t(ps["misc"][:, :IDX_DIM])
        lf_new = jnp.pad(new_t(ps["misc"][:, MISC_LF:MISC_WI]), ((0, 0), (0, 8 - A_HEADS), (0, 0)))
        bias = _sample_index(page_table, q3(ps["qi"]), q3(ps["misc"]), cidx, l, ki_new, pps_i, topk_s)
        oa_s, ob_s = _sample_attn(page_table, q3(ps["qa"]), q3(ps["qb"]), bias, (cak, cav, clf, cbk, cbv), l,
                                  (new_t(ps["ka"]), new_t(ps["va"]), lf_new, new_t(ps["kb"]), new_t(ps["vb"])),
                                  pps_a)
        x1s, buf_s = _merge(oa_s, ob_s, q3(ps["u"]), q3(ps["gb"]), q3(xs), state_conv[l].astype(F32), cw, wo_bf,
                            g1, b1, t)
        xs = _peer(x1s.reshape(n_s, d), wqt, sk, u_bf, vt_bf, g2, b2, tm_peer_s, ec)
        for lst, val in zip(outs_s, (ps["ka"].reshape(nb, t, A_HEADS, HEAD_DIM), ps["va"].reshape(nb, t, A_HEADS, HEAD_DIM),
                                     ps["misc"][:, MISC_LF:MISC_WI].reshape(nb, t, A_HEADS),
                                     ps["kb"].reshape(nb, t, B_HEADS, HEAD_DIM), ps["vb"].reshape(nb, t, B_HEADS, HEAD_DIM),
                                     ps["misc"][:, :IDX_DIM].reshape(nb, t, IDX_DIM), buf_s)):
            lst.append(val)

    return ((xp.reshape(bp, s, d), xs.reshape(nb, t, d))
            + tuple(jnp.stack(v) for v in outs_p) + tuple(jnp.stack(v) for v in outs_s))
```

```python
import functools

import jax
import jax.numpy as jnp
from jax import lax
from jax.experimental import pallas as pl
from jax.experimental.pallas import tpu as pltpu

F32 = jnp.float32
BF16 = jnp.bfloat16
I32 = jnp.int32
I16 = jnp.int16
HALF16 = 2 ** 15

D_MODEL = 1024
PAGE_SIZE = 128
HEAD_DIM = 64
A_HEADS = 6
B_HEADS = 6
A_WIDTH = A_HEADS * HEAD_DIM
B_WIDTH = B_HEADS * HEAD_DIM
C_WIDTH = 256
IDX_HEADS = 8
IDX_DIM = 64
TOPK_MAX = 256
CONV_W = 3
PEER_HEADS = 8
PEER_DKEY = 128
PEER_TOPK = 16
LN_EPS = 1e-5
DEPTH = 2
DN_ALPHA = (2 * DEPTH) ** 0.25
IN_WIDTHS = (A_WIDTH, A_WIDTH, A_WIDTH, A_HEADS,
             B_WIDTH, B_WIDTH, B_WIDTH, IDX_HEADS * IDX_DIM, IDX_DIM, IDX_HEADS,
             C_WIDTH, C_WIDTH, C_WIDTH)

LANES = 128
MXU_DEPTH = 256
BF16_ROWS = 16
VMEM_LIMIT = 56 * 1024 * 1024
NEG = -1e30
INT_MIN = -2 ** 31

MISC_LF = IDX_DIM
MISC_WI = IDX_DIM + A_HEADS

O_QA, O_KA, O_VA = 0, 384, 768
O_QB, O_KB, O_VB = 1152, 1536, 1920
O_QI = 2304
O_CIN, O_GB, O_GC = 3328, 3584, 3840
O_MISC = 4096
W_TOTAL = 4224


def _nt(a, b):
    return lax.dot_general(a, b, (((1,), (1,)), ((), ())), preferred_element_type=F32)


def _cparams(sem):
    return pltpu.CompilerParams(dimension_semantics=sem, vmem_limit_bytes=VMEM_LIMIT)


def _sort_key(x):
    bits = pltpu.bitcast(x, I32)
    return bits ^ ((bits >> 31) & 0x7FFFFFFF)


def _layer_norm_rows(z, g, b):
    mu = jnp.mean(z, axis=-1, keepdims=True)
    zc = z - mu
    var = jnp.mean(zc * zc, axis=-1, keepdims=True)
    return zc * lax.rsqrt(var + LN_EPS) * g + b


def _proj_kernel(x_ref, w_ref, bf_ref, qa_ref, ka_ref, va_ref, kab_ref, vab_ref,
                 qb_ref, kb_ref, vb_ref, kbb_ref, vbb_ref, qi_ref, u_ref, gb_ref,
                 misc_ref, miscb_ref, *maybe_misct_ref, transposed):
    xb = x_ref[...].astype(BF16)

    def seg(off, width):
        return jnp.dot(xb, w_ref[:, off:off + width], preferred_element_type=F32)

    def put(ref, val):
        if transposed:
            ref[0] = val.T
        else:
            ref[...] = val

    qa_ref[...] = (seg(O_QA, A_WIDTH) * (HEAD_DIM ** -0.5)).astype(BF16)
    ka = seg(O_KA, A_WIDTH)
    put(ka_ref, ka)
    kab_ref[...] = ka.astype(BF16)
    va = seg(O_VA, A_WIDTH)
    put(va_ref, va)
    vab_ref[...] = va.astype(BF16)
    qb_ref[...] = (seg(O_QB, B_WIDTH) * (HEAD_DIM ** -0.5)).astype(BF16)
    kb = seg(O_KB, B_WIDTH)
    put(kb_ref, kb)
    kbb_ref[...] = kb.astype(BF16)
    vb = seg(O_VB, B_WIDTH)
    put(vb_ref, vb)
    vbb_ref[...] = vb.astype(BF16)
    qi_ref[...] = (seg(O_QI, IDX_HEADS * LANES) * (IDX_DIM ** -0.5)).astype(BF16)
    cin = seg(O_CIN, C_WIDTH)
    gb_ref[...] = seg(O_GB, C_WIDTH)
    gc = seg(O_GC, C_WIDTH)
    u_ref[...] = gc * cin
    misc = seg(O_MISC, LANES)
    lane = lax.broadcasted_iota(I32, misc.shape, 1)
    z = misc + bf_ref[...]
    logsig = jnp.minimum(z, 0.0) - jnp.log1p(jnp.exp(-jnp.abs(z)))
    misc = jnp.where((lane >= MISC_LF) & (lane < MISC_WI), logsig, misc)
    misc_ref[...] = misc
    miscb_ref[...] = misc.astype(BF16)
    if transposed:
        maybe_misct_ref[0][0] = misc.T


def _project(x, wcat, bfrow, tm, seq=None):
    n = x.shape[0]
    transposed = seq is not None
    row = lambda w: pl.BlockSpec((tm, w), lambda i: (i, 0))
    widths = [(A_WIDTH, BF16), (A_WIDTH, F32), (A_WIDTH, F32), (A_WIDTH, BF16), (A_WIDTH, BF16),
              (B_WIDTH, BF16), (B_WIDTH, F32), (B_WIDTH, F32), (B_WIDTH, BF16), (B_WIDTH, BF16),
              (IDX_HEADS * LANES, BF16), (C_WIDTH, F32), (C_WIDTH, F32), (LANES, F32), (LANES, BF16)]
    names = ["qa", "ka", "va", "kab", "vab", "qb", "kb", "vb", "kbb", "vbb", "qi", "u", "gb", "misc", "miscb"]
    out_specs = [row(w) for w, _ in widths]
    out_shape = [jax.ShapeDtypeStruct((n, w), dt) for w, dt in widths]
    if transposed:
        per = seq // tm
        tspec = lambda w: pl.BlockSpec((1, w, tm), lambda i: (i // per, 0, i % per))
        tshape = lambda w: jax.ShapeDtypeStruct((n // seq, w, seq), F32)
        for name in ("ka", "va", "kb", "vb"):
            k = names.index(name)
            out_specs[k], out_shape[k] = tspec(widths[k][0]), tshape(widths[k][0])
        names.append("misct")
        out_specs.append(tspec(LANES))
        out_shape.append(tshape(LANES))
    outs = pl.pallas_call(
        functools.partial(_proj_kernel, transposed=transposed),
        grid=(n // tm,),
        in_specs=[row(D_MODEL),
                  pl.BlockSpec((D_MODEL, W_TOTAL), lambda i: (0, 0)),
                  pl.BlockSpec((1, LANES), lambda i: (0, 0))],
        out_specs=out_specs,
        out_shape=out_shape,
        compiler_params=_cparams(("parallel",)),
        name="proj",
    )(x, wcat, bfrow)
    return dict(zip(names, outs))


def _wt_kernel(w_ref, o_ref):
    o_ref[...] = w_ref[...].T.astype(BF16)


def _build_wcat(w_in_t, b_f):
    parts, off = [], 0
    for w in IN_WIDTHS:
        parts.append(w_in_t[off:off + w, :])
        off += w
    qa, ka, va, fa, qb, kb, vb, qi, ki, wi, cin, gb, gc = parts
    d = w_in_t.shape[1]
    qi_pad = jnp.pad(qi.reshape(IDX_HEADS, IDX_DIM, d), ((0, 0), (0, LANES - IDX_DIM), (0, 0)))
    qi_pad = qi_pad.reshape(IDX_HEADS * LANES, d)
    misc = jnp.concatenate([ki, fa, wi, jnp.zeros((LANES - MISC_WI - IDX_HEADS, d), w_in_t.dtype)], axis=0)
    wcat_t = jnp.concatenate([qa, ka, va, qb, kb, vb, qi_pad, cin, gb, gc, misc], axis=0)
    wcat = pl.pallas_call(
        _wt_kernel,
        grid=(W_TOTAL // LANES,),
        in_specs=[pl.BlockSpec((LANES, d), lambda i: (i, 0))],
        out_specs=pl.BlockSpec((d, LANES), lambda i: (0, i)),
        out_shape=jax.ShapeDtypeStruct((d, W_TOTAL), BF16),
        compiler_params=_cparams(("parallel",)),
        name="w_in_transpose",
    )(wcat_t)
    bfrow = jnp.zeros((1, LANES), F32).at[0, MISC_LF:MISC_WI].set(b_f.astype(F32))
    return wcat, bfrow


def _cumsum_kernel(m_ref, cum_ref, cumt_ref, carry_ref, *, tc):
    @pl.when(pl.program_id(1) == 0)
    def _():
        carry_ref[...] = jnp.zeros_like(carry_ref)

    v = m_ref[0]
    r = lax.broadcasted_iota(I32, (tc, tc), 0)
    c = lax.broadcasted_iota(I32, (tc, tc), 1)
    tri = (c <= r).astype(F32)
    loc = jnp.dot(tri, v, precision=lax.Precision.HIGHEST, preferred_element_type=F32) + carry_ref[...]
    cum_ref[0] = loc
    cumt_ref[0] = loc.T
    carry_ref[...] = loc[tc - 1:tc, :]


def _cumsum(misc3, tc):
    b, s, _ = misc3.shape
    return pl.pallas_call(
        functools.partial(_cumsum_kernel, tc=tc),
        grid=(b, s // tc),
        in_specs=[pl.BlockSpec((1, tc, LANES), lambda i, j: (i, j, 0))],
        out_specs=[pl.BlockSpec((1, tc, LANES), lambda i, j: (i, j, 0)),
                   pl.BlockSpec((1, LANES, tc), lambda i, j: (i, 0, j))],
        out_shape=[jax.ShapeDtypeStruct((b, s, LANES), F32), jax.ShapeDtypeStruct((b, LANES, s), F32)],
        scratch_shapes=[pltpu.VMEM((1, LANES), F32)],
        compiler_params=_cparams(("parallel", "arbitrary")),
        name="logf_cumsum",
    )(misc3)


def _flash_update(carry, s, vblk, v_transposed=False):
    m, l, acc = carry
    m_new = jnp.maximum(m, jnp.max(s, axis=1, keepdims=True))
    alpha = jnp.exp(m - m_new)
    p = jnp.exp(s - m_new)
    l = alpha * l + jnp.sum(p, axis=1, keepdims=True)
    p16 = p.astype(BF16)
    pv = _nt(p16, vblk) if v_transposed else jnp.dot(p16, vblk, preferred_element_type=F32)
    return m_new, l, alpha * acc + pv


def _flash_init(rows, width):
    return (jnp.full((rows, 1), NEG, F32), jnp.zeros((rows, 1), F32), jnp.zeros((rows, width), F32))


def _split_pair(qp):
    lane = lax.broadcasted_iota(I32, qp.shape, 1)
    zero = jnp.zeros_like(qp)
    return jnp.where(lane < HEAD_DIM, qp, zero), jnp.where(lane >= HEAD_DIM, qp, zero)


def _join_pair(o0, o1):
    lane = lax.broadcasted_iota(I32, o0.shape, 1)
    return jnp.where(lane < HEAD_DIM, o0, o1)


def _pipelined_blocks(nblk, store_qk, step, state):
    store_qk(0, 0)

    def pair(jj, st):
        store_qk(1, 2 * jj + 1)
        st = step(2 * jj, st, 0, False)
        store_qk(0, 2 * jj + 2)
        return step(2 * jj + 1, st, 1, False)

    npair = (nblk - 1) // 2
    st = lax.fori_loop(0, npair, pair, state)

    def one_left(st):
        return step(nblk - 1, st, 0, True)

    def two_left(st):
        store_qk(1, nblk - 1)
        st = step(nblk - 2, st, 0, False)
        return step(nblk - 1, st, 1, True)

    return lax.cond(nblk - 2 * npair == 2, two_left, one_left, st)


def _fox_kernel(q_ref, k_ref, v_ref, cum_ref, cumt_ref, o_ref, qk_s, *, tq, tk):
    p = pl.program_id(1)
    i = pl.program_id(2)
    qs = _split_pair(q_ref[0])
    lane = lax.broadcasted_iota(I32, (tq, LANES), 1)
    cumblk = cum_ref[0]
    cqs = [jnp.sum(jnp.where(lane == MISC_LF + 2 * p + hh, cumblk, 0.0), axis=1, keepdims=True)
           for hh in range(2)]
    row_g = i * tq + lax.broadcasted_iota(I32, (tq, tk), 0)
    col_l = lax.broadcasted_iota(I32, (tq, tk), 1)

    def store_qk(slot, j):
        off = pl.multiple_of(j * tk, tk)
        kblk = k_ref[0, pl.ds(off, tk), :]
        for hh in range(2):
            qk_s[slot, hh] = _nt(qs[hh], kblk)

    def step(j, state, slot, last):
        off = pl.multiple_of(j * tk, tk)
        vblk = v_ref[0, pl.ds(off, tk), :]
        new = []
        for hh in range(2):
            ck = cumt_ref[0, pl.ds(2 * p + hh, 1), pl.ds(off, tk)]
            s = qk_s[slot, hh] + (cqs[hh] - ck)
            if last:
                s = jnp.where(off + col_l <= row_g, s, NEG)
            new.append(_flash_update(state[hh], s, vblk))
        return tuple(new)

    nblk = (i * tq) // tk + 1
    init = (_flash_init(tq, LANES), _flash_init(tq, LANES))
    (_, l0, acc0), (_, l1, acc1) = _pipelined_blocks(nblk, store_qk, step, init)
    o_ref[0] = _join_pair(acc0 / l0, acc1 / l1).astype(BF16)


def _fox_prompt(qa, kab, vab, cum, cumt, tq, tk):
    b, s, _ = qa.shape
    npairs = A_HEADS // 2
    return pl.pallas_call(
        functools.partial(_fox_kernel, tq=tq, tk=tk),
        grid=(b, npairs, s // tq),
        in_specs=[pl.BlockSpec((1, tq, LANES), lambda bb, p, i: (bb, i, p)),
                  pl.BlockSpec((1, s, LANES), lambda bb, p, i: (bb, 0, p)),
                  pl.BlockSpec((1, s, LANES), lambda bb, p, i: (bb, 0, p)),
                  pl.BlockSpec((1, tq, LANES), lambda bb, p, i: (bb, i, 0)),
                  pl.BlockSpec((1, 8, s), lambda bb, p, i: (bb, MISC_LF // 8, 0))],
        out_specs=pl.BlockSpec((1, tq, LANES), lambda bb, p, i: (bb, i, p)),
        out_shape=jax.ShapeDtypeStruct((b, s, A_WIDTH), BF16),
        scratch_shapes=[pltpu.VMEM((2, 2, tq, tk), F32)],
        compiler_params=_cparams(("parallel", "parallel", "arbitrary")),
        name="fox_prompt",
    )(qa, kab, vab, cum, cumt)


def _kth_threshold(count_ge, rows, k, total):
    def cond(state):
        it, _, cnt = state
        return (it < 32) & (jnp.max(cnt) > k)

    def body(state):
        it, t, cnt = state
        cand = t | lax.shift_left(jnp.int32(1), 31 - it)
        c = count_ge(cand ^ INT_MIN)
        ok = c >= k
        return it + 1, jnp.where(ok, cand, t), jnp.where(ok, c, cnt)

    init = (jnp.int32(0), jnp.zeros((rows, 1), I32), jnp.full((rows, 1), total, I32))
    _, t, cnt = lax.while_loop(cond, body, init)
    return t ^ INT_MIN, cnt


def _tie_cutoff(count_tie_below, rows, need, nbits):
    def body(it, j):
        cand = j | lax.shift_left(jnp.int32(1), nbits - 1 - it)
        cnt = count_tie_below(cand)
        return jnp.where(cnt <= need, cand, j)
    return lax.fori_loop(0, nbits, body, jnp.zeros((rows, 1), I32))


def _dsa_kernel(qi_ref, misc_ref, kib_ref, qb_ref, kb_ref, vb_ref, o_ref, key_ref, bias_ref, qk_s,
                khi_ref, klo_ref, *, tq, kc, kcc, topk, nbits):
    i = pl.program_id(1)
    nch = (i * tq + tq + kc - 1) // kc
    ncc = (i * tq + tq + kcc - 1) // kcc
    nch1 = ncc * (kcc // kc)
    w8 = misc_ref[0][:, MISC_WI:MISC_WI + IDX_HEADS] * (IDX_HEADS ** -0.5)
    row_g = i * tq + lax.broadcasted_iota(I32, (tq, kc), 0)
    col_l = lax.broadcasted_iota(I32, (tq, kc), 1)
    col_c = lax.broadcasted_iota(I32, (tq, kcc), 1)

    def p1(c, carry):
        off = pl.multiple_of(c * kc, kc)
        kib = kib_ref[0, pl.ds(off, kc), :]
        acc = jnp.zeros((tq, kc), F32)
        for h in range(IDX_HEADS):
            r = _nt(qi_ref[0, :, h * LANES:(h + 1) * LANES], kib)
            acc = acc + w8[:, h:h + 1] * jnp.maximum(r, 0.0)
        acc = jnp.where(acc == 0.0, 0.0, acc)
        sc = jnp.where(off + col_l <= row_g, acc, -jnp.inf)
        key = _sort_key(sc)
        key_ref[:, pl.ds(off, kc)] = key
        khi_ref[:, pl.ds(off, kc)] = (key >> 16).astype(I16)
        return carry

    lax.fori_loop(0, nch1, p1, 0)

    def lane_groups(x):
        tot = x[:, 0:LANES]
        for g in range(1, kcc // LANES):
            tot = tot + x[:, g * LANES:(g + 1) * LANES]
        return tot

    def count(pred):
        def body(c, cnt):
            off = pl.multiple_of(c * kcc, kcc)
            return cnt + lane_groups(pred(key_ref[:, pl.ds(off, kcc)], off).astype(I32))
        cnt = lax.fori_loop(0, ncc, body, jnp.zeros((tq, LANES), I32))
        return jnp.sum(cnt, axis=1, keepdims=True)

    def count16(ref16, pred):
        def body(c, cnt):
            off = pl.multiple_of(c * kcc, kcc)
            hit = jnp.where(pred(ref16[:, pl.ds(off, kcc)]), jnp.int16(1), jnp.int16(0))
            return cnt + lane_groups(hit)
        cnt = lax.fori_loop(0, ncc, body, jnp.zeros((tq, LANES), I16))
        return jnp.sum(cnt.astype(I32), axis=1, keepdims=True)

    def bit_step(t, cnt, it, count_ge16, base):
        cand = t | lax.shift_left(jnp.int32(1), 15 - it)
        c = base + count_ge16((cand - HALF16).astype(I16))
        ok = c >= topk
        return jnp.where(ok, cand, t), jnp.where(ok, c, cnt)

    zero = jnp.zeros((tq, 1), I32)
    t_hi, n_hi_ge = lax.fori_loop(
        0, 16,
        lambda it, st: bit_step(st[0], st[1], it, lambda th: count16(khi_ref, lambda kk: kk >= th), 0),
        (zero, jnp.full((tq, 1), ncc * kcc, I32)))
    th_s = t_hi - HALF16

    def low_half():
        th16 = th_s.astype(I16)
        n_hi_gt = count16(khi_ref, lambda kk: kk > th16)

        def build(c, carry):
            off = pl.multiple_of(c * kcc, kcc)
            lo = ((key_ref[:, pl.ds(off, kcc)] & 0xFFFF) - HALF16).astype(I16)
            klo_ref[:, pl.ds(off, kcc)] = jnp.where(khi_ref[:, pl.ds(off, kcc)] == th16, lo, jnp.int16(-HALF16))
            return carry

        lax.fori_loop(0, ncc, build, 0)

        def cond(st):
            it, _, cnt = st
            return (it < 16) & (jnp.max(cnt) > topk)

        def body(st):
            it, t, cnt = st
            t, cnt = bit_step(t, cnt, it, lambda th: count16(klo_ref, lambda kk: kk >= th), n_hi_gt)
            return it + 1, t, cnt

        _, t_lo, cnt = lax.while_loop(cond, body, (jnp.int32(0), zero, n_hi_ge))
        return t_lo, cnt

    t_lo, n_ge = lax.cond(jnp.max(n_hi_ge) > topk, low_half, lambda: (zero, n_hi_ge))
    thr = lax.shift_left(th_s, 16) | t_lo
    big = jnp.full((tq, 1), 2 ** nbits - 1, I32)

    def tie_cutoff():
        need = topk - count(lambda kk, off: kk > thr)
        return _tie_cutoff(lambda j: count(lambda kk, off: (kk == thr) & (off + col_c < j)), tq, need, nbits)

    jstar = lax.cond(jnp.max(n_ge) > topk, tie_cutoff, lambda: big)

    def p2(c, carry):
        off = pl.multiple_of(c * kc, kc)
        kk = key_ref[:, pl.ds(off, kc)]
        colg = off + col_l
        sel = (kk > thr) | ((kk == thr) & (colg < jstar))
        bias_ref[:, pl.ds(off, kc)] = jnp.where(sel & (colg <= row_g), 0.0, NEG)
        return carry

    lax.fori_loop(0, nch, p2, 0)

    for pr in range(B_HEADS // 2):
        qs = _split_pair(qb_ref[0, :, pr * LANES:(pr + 1) * LANES])

        def store_qk(slot, c):
            off = pl.multiple_of(c * kc, kc)
            kblk = kb_ref[0, pl.ds(off, kc), pr * LANES:(pr + 1) * LANES]
            for hh in range(2):
                qk_s[slot, hh] = _nt(qs[hh], kblk)

        def step(c, state, slot, last):
            off = pl.multiple_of(c * kc, kc)
            vblk = vb_ref[0, pl.ds(off, kc), pr * LANES:(pr + 1) * LANES]
            bias = bias_ref[:, pl.ds(off, kc)]
            return tuple(_flash_update(state[hh], qk_s[slot, hh] + bias, vblk) for hh in range(2))

        init = (_flash_init(tq, LANES), _flash_init(tq, LANES))
        (_, l0, acc0), (_, l1, acc1) = _pipelined_blocks(nch, store_qk, step, init)
        o_ref[0, :, pr * LANES:(pr + 1) * LANES] = _join_pair(acc0 / l0, acc1 / l1).astype(BF16)


def _dsa_prompt(qi, misc, miscb, qb, kbb, vbb, tq, kc, topk):
    b, s, _ = qb.shape
    nbits = max(1, (s - 1).bit_length()) + 1
    kcc = 2 * kc if s % (2 * kc) == 0 else kc
    qspec = lambda w: pl.BlockSpec((1, tq, w), lambda bb, i: (bb, i, 0))
    full = lambda w: pl.BlockSpec((1, s, w), lambda bb, i: (bb, 0, 0))
    return pl.pallas_call(
        functools.partial(_dsa_kernel, tq=tq, kc=kc, kcc=kcc, topk=topk, nbits=nbits),
        grid=(b, s // tq),
        in_specs=[qspec(IDX_HEADS * LANES), qspec(LANES), full(LANES), qspec(B_WIDTH), full(B_WIDTH), full(B_WIDTH)],
        out_specs=qspec(B_WIDTH),
        out_shape=jax.ShapeDtypeStruct((b, s, B_WIDTH), BF16),
        scratch_shapes=[pltpu.VMEM((tq, s), I32), pltpu.VMEM((tq, s), F32), pltpu.VMEM((2, 2, tq, kc), F32),
                        pltpu.VMEM((tq, s), I16), pltpu.VMEM((tq, s), I16)],
        compiler_params=_cparams(("parallel", "arbitrary")),
        name="dsa_prompt",
    )(qi, misc, miscb, qb, kbb, vbb)


def _sidx_kernel(pt_ref, qi_ref, misc_ref, *rest, pps, nps, past, topk, nbits):
    page_refs = rest[:pps]
    knew_ref, bias_ref, key_ref, qst_ref, wst_ref = rest[pps:]
    j = pl.program_id(1)
    t = qi_ref.shape[1]
    lp = key_ref.shape[1]

    @pl.when(j == 0)
    def _():
        qf = qi_ref[0].astype(F32)
        qst_ref[...] = jnp.concatenate([qf[:, h * LANES:h * LANES + IDX_DIM] for h in range(IDX_HEADS)],
                                       axis=0).astype(BF16)
        w8 = misc_ref[0][:, MISC_WI:MISC_WI + IDX_HEADS] * (IDX_HEADS ** -0.5)
        wst_ref[...] = jnp.concatenate([w8[:, h:h + 1] for h in range(IDX_HEADS)], axis=0)

    def score(kt):
        r = jnp.dot(qst_ref[...], kt.astype(BF16), preferred_element_type=F32)
        r = wst_ref[...] * jnp.maximum(r, 0.0)
        acc = r[0:t, :]
        for h in range(1, IDX_HEADS):
            acc = acc + r[h * t:(h + 1) * t, :]
        return jnp.where(acc == 0.0, 0.0, acc)

    @pl.when(j < nps)
    def _():
        kt = jnp.concatenate([page_refs[k][0, 0] for k in range(pps)], axis=1)
        off = pl.multiple_of(j * (pps * PAGE_SIZE), pps * PAGE_SIZE)
        key_ref[:, pl.ds(off, pps * PAGE_SIZE)] = _sort_key(score(kt))

    @pl.when(j == nps)
    def _():
        rowi = lax.broadcasted_iota(I32, (t, PAGE_SIZE), 0)
        coli = lax.broadcasted_iota(I32, (t, PAGE_SIZE), 1)
        sc = jnp.where(coli <= rowi, score(knew_ref[0]), -jnp.inf)
        key_ref[:, past:past + PAGE_SIZE] = _sort_key(sc)

        keys = key_ref[...]
        colg = lax.broadcasted_iota(I32, (t, lp), 1)
        count = lambda pred: jnp.sum(pred.astype(I32), axis=1, keepdims=True)
        thr, _ = _kth_threshold(lambda th: count(keys >= th), t, topk, lp)
        need = topk - count(keys > thr)
        jstar = _tie_cutoff(lambda jj: count((keys == thr) & (colg < jj)), t, need, nbits)
        sel = (keys > thr) | ((keys == thr) & (colg < jstar))
        rowg = past + lax.broadcasted_iota(I32, (t, lp), 0)
        bias_ref[0] = jnp.where(sel & (colg <= rowg), 0.0, NEG)


def _sample_index(page_table, qi3, misc3, cache_idx_t, layer, knew_t, pps, topk):
    nb, t, _ = qi3.shape
    npg = page_table.shape[1]
    nps = npg // pps
    past = npg * PAGE_SIZE
    lp = past + PAGE_SIZE
    nbits = lp.bit_length() + 1

    def page_map(k):
        return lambda b, j, pt: (layer, pt[b * npg + jnp.minimum(j, nps - 1) * pps + k], 0, 0)

    bmap = lambda b, j, pt: (b, 0, 0)
    grid_spec = pltpu.PrefetchScalarGridSpec(
        num_scalar_prefetch=1,
        grid=(nb, nps + 1),
        in_specs=[pl.BlockSpec((1, t, IDX_HEADS * LANES), bmap),
                  pl.BlockSpec((1, t, LANES), bmap)]
                 + [pl.BlockSpec((1, 1, IDX_DIM, PAGE_SIZE), page_map(k)) for k in range(pps)]
                 + [pl.BlockSpec((1, IDX_DIM, PAGE_SIZE), bmap)],
        out_specs=pl.BlockSpec((1, t, lp), bmap),
        scratch_shapes=[pltpu.VMEM((t, lp), I32), pltpu.VMEM((IDX_HEADS * t, IDX_DIM), BF16),
                        pltpu.VMEM((IDX_HEADS * t, 1), F32)],
    )
    return pl.pallas_call(
        functools.partial(_sidx_kernel, pps=pps, nps=nps, past=past, topk=topk, nbits=nbits),
        grid_spec=grid_spec,
        out_shape=jax.ShapeDtypeStruct((nb, t, lp), F32),
        compiler_params=_cparams(("parallel", "arbitrary")),
        name="sample_index",
    )(page_table.reshape(-1), qi3, misc3, *([cache_idx_t] * pps), knew_t)


def _sattn_kernel(pt_ref, qa_ref, qb_ref, bias_ref, *rest, pps, nps):
    grp = lambda g: rest[g * pps:(g + 1) * pps]
    ak_refs, av_refs, lf_refs, bk_refs, bv_refs = (grp(g) for g in range(5))
    (akn_ref, avn_ref, lfn_ref, bkn_ref, bvn_ref, oa_ref, ob_ref,
     qbd_a, qbd_b, ma, la, acca, mb, lb, accb, carry_ref) = rest[5 * pps:]
    j = pl.program_id(1)
    t = qa_ref.shape[1]
    rows = A_HEADS * t
    lane_q = lax.broadcasted_iota(I32, (t, A_WIDTH), 1)

    def block_diag(q):
        qf = q.astype(F32)
        parts = [jnp.where((lane_q >= h * HEAD_DIM) & (lane_q < (h + 1) * HEAD_DIM), qf, 0.0)
                 for h in range(A_HEADS)]
        return jnp.concatenate(parts, axis=0).astype(BF16)

    @pl.when(j == 0)
    def _():
        qbd_a[...] = block_diag(qa_ref[0])
        qbd_b[...] = block_diag(qb_ref[0])
        for r in (ma, mb):
            r[...] = jnp.full(r.shape, NEG, F32)
        for r in (la, lb, acca, accb, carry_ref):
            r[...] = jnp.zeros(r.shape, F32)

    r_i = lax.broadcasted_iota(I32, (PAGE_SIZE, PAGE_SIZE), 0)
    c_i = lax.broadcasted_iota(I32, (PAGE_SIZE, PAGE_SIZE), 1)
    triu = (r_i <= c_i).astype(F32)

    def cat(refs, dtype):
        pages = [r[0, 0] if len(r.shape) == 4 else r[0] for r in refs]
        x = pages[0] if len(pages) == 1 else jnp.concatenate(pages, axis=1)
        return x.astype(dtype)

    def update(m_ref, l_ref, acc_ref, s, vt):
        m, l, acc = _flash_update((m_ref[...], l_ref[...], acc_ref[...]), s, vt, v_transposed=True)
        m_ref[...] = m
        l_ref[...] = l
        acc_ref[...] = acc

    def process(ak, av, lf, bk, bv, bias8, mask):
        run = carry_ref[...]
        cls = []
        for r in lf:
            page = r[0, 0] if len(r.shape) == 4 else r[0]
            cl = jnp.dot(page, triu, precision=lax.Precision.HIGHEST, preferred_element_type=F32) + run
            run = jnp.broadcast_to(cl[:, PAGE_SIZE - 1:PAGE_SIZE], cl.shape)
            cls.append(cl)
        carry_ref[...] = run
        ck = cls[0] if len(cls) == 1 else jnp.concatenate(cls, axis=1)
        width = ck.shape[1]
        ck_rows = jnp.concatenate([jnp.broadcast_to(ck[h:h + 1, :], (t, width)) for h in range(A_HEADS)], axis=0)
        sa = jnp.dot(qbd_a[...], cat(ak, BF16), preferred_element_type=F32) - ck_rows
        if mask is not None:
            sa = jnp.where(mask, sa, NEG)
        update(ma, la, acca, sa, cat(av, BF16))
        sb = jnp.dot(qbd_b[...], cat(bk, BF16), preferred_element_type=F32)
        sb = sb + jnp.concatenate([bias8] * B_HEADS, axis=0)
        update(mb, lb, accb, sb, cat(bv, BF16))

    @pl.when(j < nps)
    def _():
        off = pl.multiple_of(j * (pps * PAGE_SIZE), pps * PAGE_SIZE)
        process(ak_refs, av_refs, lf_refs, bk_refs, bv_refs, bias_ref[0, :, pl.ds(off, pps * PAGE_SIZE)], None)

    @pl.when(j == nps)
    def _():
        past = nps * pps * PAGE_SIZE
        rowi = lax.broadcasted_iota(I32, (rows, PAGE_SIZE), 0)
        coli = lax.broadcasted_iota(I32, (rows, PAGE_SIZE), 1)
        mask = coli <= (rowi & (t - 1))
        process([akn_ref], [avn_ref], [lfn_ref], [bkn_ref], [bvn_ref],
                bias_ref[0, :, past:past + PAGE_SIZE], mask)

        def gather_heads(acc_ref, l_ref):
            o = jnp.zeros((t, A_WIDTH), F32)
            for h in range(A_HEADS):
                blk = acc_ref[h * t:(h + 1) * t, :] / l_ref[h * t:(h + 1) * t, :]
                o = jnp.where((lane_q >= h * HEAD_DIM) & (lane_q < (h + 1) * HEAD_DIM), blk, o)
            return o.astype(BF16)

        oa_ref[0] = gather_heads(acca, la)
        ob_ref[0] = gather_heads(accb, lb)


def _sample_attn(page_table, qa3, qb3, bias, caches, layer, news, pps):
    nb, t, _ = qa3.shape
    assert t & (t - 1) == 0
    npg = page_table.shape[1]
    nps = npg // pps
    lp = bias.shape[2]
    rows = A_HEADS * t

    def page_map(k):
        return lambda b, j, pt: (layer, pt[b * npg + jnp.minimum(j, nps - 1) * pps + k], 0, 0)

    bmap = lambda b, j, pt: (b, 0, 0)
    cache_specs, cache_args = [], []
    for c in caches:
        blk = (1, 1) + c.shape[2:]
        for k in range(pps):
            cache_specs.append(pl.BlockSpec(blk, page_map(k)))
            cache_args.append(c)
    new_specs = [pl.BlockSpec((1,) + a.shape[1:], bmap) for a in news]
    grid_spec = pltpu.PrefetchScalarGridSpec(
        num_scalar_prefetch=1,
        grid=(nb, nps + 1),
        in_specs=[pl.BlockSpec((1, t, A_WIDTH), bmap), pl.BlockSpec((1, t, B_WIDTH), bmap),
                  pl.BlockSpec((1, t, lp), bmap)] + cache_specs + new_specs,
        out_specs=[pl.BlockSpec((1, t, A_WIDTH), bmap), pl.BlockSpec((1, t, B_WIDTH), bmap)],
        scratch_shapes=[pltpu.VMEM((rows, A_WIDTH), BF16), pltpu.VMEM((rows, B_WIDTH), BF16),
                        pltpu.VMEM((rows, 1), F32), pltpu.VMEM((rows, 1), F32), pltpu.VMEM((rows, A_WIDTH), F32),
                        pltpu.VMEM((rows, 1), F32), pltpu.VMEM((rows, 1), F32), pltpu.VMEM((rows, B_WIDTH), F32),
                        pltpu.VMEM((8, PAGE_SIZE), F32)],
    )
    return pl.pallas_call(
        functools.partial(_sattn_kernel, pps=pps, nps=nps),
        grid_spec=grid_spec,
        out_shape=[jax.ShapeDtypeStruct((nb, t, A_WIDTH), BF16), jax.ShapeDtypeStruct((nb, t, B_WIDTH), BF16)],
        compiler_params=_cparams(("parallel", "arbitrary")),
        name="sample_attn",
    )(page_table.reshape(-1), qa3, qb3, bias, *cache_args, *news)


def _merge_kernel(oa_ref, ob_ref, u_ref, gb_ref, x_ref, buf_ref, cw_ref, wo_ref, g_ref, b_ref,
                  y_ref, nbuf_ref, carry_ref, *, tm):
    j = pl.program_id(1)

    @pl.when(j == 0)
    def _():
        carry_ref[...] = buf_ref[0]

    u = u_ref[0]
    rowi = lax.broadcasted_iota(I32, u.shape, 0)
    c0 = carry_ref[0:1, :]
    c1 = carry_ref[1:2, :]
    u1 = jnp.where(rowi == 0, c1, pltpu.roll(u, 1, 0))
    u2 = jnp.where(rowi == 0, c0, jnp.where(rowi == 1, c1, pltpu.roll(u, 2, 0)))
    yc = cw_ref[0:1, :] * u2 + cw_ref[1:2, :] * u1 + cw_ref[2:3, :] * u
    oc = (gb_ref[0] * yc).astype(BF16)
    new_carry = u[tm - 2:tm, :]
    carry_ref[...] = new_carry
    nbuf_ref[0] = new_carry
    mix = jnp.dot(oa_ref[0], wo_ref[0:A_WIDTH, :], preferred_element_type=F32)
    mix = mix + jnp.dot(ob_ref[0], wo_ref[A_WIDTH:A_WIDTH + B_WIDTH, :], preferred_element_type=F32)
    mix = mix + jnp.dot(oc, wo_ref[A_WIDTH + B_WIDTH:, :], preferred_element_type=F32)
    y_ref[0] = _layer_norm_rows(DN_ALPHA * x_ref[0] + mix, g_ref[...], b_ref[...])


def _merge(oa, ob, u, gb, x, buf, conv_w, wo_bf, g, b, tm):
    nseq, t, _ = x.shape
    blk = lambda w: pl.BlockSpec((1, tm, w), lambda s, j: (s, j, 0))
    const = lambda r, w: pl.BlockSpec((r, w), lambda s, j: (0, 0))
    return pl.pallas_call(
        functools.partial(_merge_kernel, tm=tm),
        grid=(nseq, t // tm),
        in_specs=[blk(A_WIDTH), blk(B_WIDTH), blk(C_WIDTH), blk(C_WIDTH), blk(D_MODEL),
                  pl.BlockSpec((1, CONV_W - 1, C_WIDTH), lambda s, j: (s, 0, 0)),
                  const(CONV_W, C_WIDTH), const(D_MODEL, D_MODEL), const(1, D_MODEL), const(1, D_MODEL)],
        out_specs=[blk(D_MODEL), pl.BlockSpec((1, CONV_W - 1, C_WIDTH), lambda s, j: (s, 0, 0))],
        out_shape=[jax.ShapeDtypeStruct((nseq, t, D_MODEL), F32),
                   jax.ShapeDtypeStruct((nseq, CONV_W - 1, C_WIDTH), F32)],
        scratch_shapes=[pltpu.VMEM((CONV_W - 1, C_WIDTH), F32)],
        compiler_params=_cparams(("parallel", "arbitrary")),
        name="merge_ln",
    )(oa, ob, u, gb, x, buf, conv_w, wo_bf, g, b)


def _top16_rows(s_list):
    nk, tm = s_list[0].shape

    def extract(s, first_occurrence):
        idx = lax.broadcasted_iota(I32, (nk, tm), 0).astype(F32)
        work = s
        rank = jnp.full((nk, tm), 99.0, F32)
        vals = []
        for r in range(PEER_TOPK):
            m = jnp.max(work, axis=0, keepdims=True)
            hit = work == m
            if first_occurrence:
                hit = idx == jnp.min(jnp.where(hit, idx, 1e9), axis=0, keepdims=True)
            rank = jnp.where(hit, float(r), rank)
            work = jnp.where(hit, -jnp.inf, work)
            vals.append(m)
        return jnp.concatenate(vals, axis=0), rank

    fast = [extract(s, False) for s in s_list]
    bad = jnp.zeros((1, tm), F32)
    for _, rank in fast:
        nsel = jnp.sum(jnp.where(rank < 99.0, 1.0, 0.0), axis=0, keepdims=True)
        bad = jnp.maximum(bad, jnp.where(nsel != float(PEER_TOPK), 1.0, 0.0))
    flat = lambda pairs: tuple(x for pair in pairs for x in pair)
    out = lax.cond(jnp.max(bad) > 0.0, lambda: flat([extract(s, True) for s in s_list]), lambda: flat(fast))
    return [(out[2 * i], out[2 * i + 1]) for i in range(len(s_list))]


def _peer_kernel(x_ref, wqt_ref, sk_ref, u_ref, vt_ref, g_ref, b_ref, o_ref,
                 xt_s, qt_s, outt_s, a_s, n1_s, bb_s, r2_s, *, tm, ec, nk):
    e = pl.program_id(1)
    ne = pl.num_programs(1)
    k = PEER_TOPK
    half = PEER_DKEY // 2

    @pl.when(e == 0)
    def _():
        xt = x_ref[...].T.astype(BF16)
        xt_s[...] = xt
        qt_s[...] = jnp.dot(wqt_ref[...], xt, preferred_element_type=F32).astype(BF16)
        rho = lax.broadcasted_iota(I32, (80, tm), 0)
        mid = rho - 16
        r1 = jnp.where(rho < 16, rho, jnp.where(rho < 72, mid & 7, 0))
        r2 = jnp.where(rho < 16, 0, jnp.where(rho < 72, (mid >> 3) + 1, rho - 64))
        pos = (r1 * k + r2).astype(F32)
        valid = (r1 + 1) * (r2 + 1) <= k
        row8 = lax.broadcasted_iota(I32, (8, tm), 0)
        def head(h, carry):
            q1 = qt_s[pl.ds(pl.multiple_of(2 * h * half, half), half), :]
            q2 = qt_s[pl.ds(pl.multiple_of((2 * h + 1) * half, half), half), :]
            s1 = jnp.dot(sk_ref[h, 0], q1, preferred_element_type=F32)
            s2 = jnp.dot(sk_ref[h, 1], q2, preferred_element_type=F32)
            (v1, rank1), (v2, rank2) = _top16_rows([s1, s2])
            slabs = [v1 + v2[0:1, :]]
            for j in range(1, 8):
                slabs.append(v1[0:8, :] + v2[j:j + 1, :])
            slabs.append(v1[0:1, :] + v2[8:16, :])
            cand = jnp.where(valid, jnp.concatenate(slabs, axis=0), -jnp.inf)
            cmax = v1[0:1, :] + v2[0:1, :]
            sel = jnp.zeros((80, tm), F32)
            z = jnp.zeros((1, tm), F32)
            for r in range(k):
                m = jnp.max(cand, axis=0, keepdims=True)
                first = jnp.min(jnp.where(cand == m, pos, 1e9), axis=0, keepdims=True)
                hit = pos == first
                sel = jnp.where(hit, 1.0, sel)
                cand = jnp.where(hit, -jnp.inf, cand)
                z = z + jnp.exp(m - cmax)
            top8 = sel[0:8, :]
            for j in range(1, 8):
                top8 = top8 + sel[8 + 8 * j:16 + 8 * j, :]
            extra = jnp.sum(sel[72:80, :], axis=0, keepdims=True)
            top8 = top8 + jnp.where(row8 == 0, extra, 0.0)
            ncount = jnp.concatenate([top8, sel[8:16, :]], axis=0)
            n1 = jnp.zeros((nk, tm), F32)
            for r in range(k):
                n1 = jnp.where(rank1 == float(r), ncount[r:r + 1, :], n1)
            a_s[h] = jnp.exp(s1 - v1[0:1, :])
            n1_s[h] = n1
            bb_s[h] = (jnp.exp(s2 - v2[0:1, :]) / z).astype(BF16)
            r2_s[h] = rank2.astype(BF16)
            return carry

        lax.fori_loop(0, PEER_HEADS, head, 0)
        outt_s[...] = jnp.zeros_like(outt_s)

    per = ec // nk
    gps = max(1, min(per, MXU_DEPTH // nk))
    sl = gps * nk
    xt = xt_s[...]
    acc = None
    nsl = ec // sl

    def rows16(row):
        if nk % BF16_ROWS:
            return jnp.broadcast_to(row, (nk, tm)).astype(BF16)
        tile = jnp.broadcast_to(row, (BF16_ROWS, tm)).astype(BF16)
        return jnp.concatenate([tile] * (nk // BF16_ROWS), axis=0) if nk > BF16_ROWS else tile

    hidden = lambda sb: jnp.dot(u_ref[sb * sl:(sb + 1) * sl, :], xt, preferred_element_type=F32)
    ht_next = hidden(0)
    for sb in range(nsl):
        ht = ht_next
        if sb + 1 < nsl:
            ht_next = hidden(sb + 1)
        act = (0.5 * ht * (1.0 + lax.erf(ht * (0.5 ** 0.5)))).astype(BF16)
        coefs = []
        for g in range(gps):
            i1 = e * per + sb * gps + g
            gate = jnp.zeros((nk, tm), BF16)
            for h in range(PEER_HEADS):
                arow = rows16(a_s[h, pl.ds(i1, 1), :])
                nrow = rows16(n1_s[h, pl.ds(i1, 1), :])
                gate = gate + jnp.where(r2_s[h] < nrow, bb_s[h], jnp.zeros((), BF16)) * arow
            coefs.append(gate * act[g * nk:(g + 1) * nk, :])
        coef = jnp.concatenate(coefs, axis=0) if gps > 1 else coefs[0]
        part = jnp.dot(vt_ref[:, sb * sl:(sb + 1) * sl], coef, preferred_element_type=F32)
        acc = part if acc is None else acc + part
    outt_s[...] += acc

    @pl.when(e == ne - 1)
    def _():
        y = outt_s[...].T
        o_ref[...] = _layer_norm_rows(DN_ALPHA * x_ref[...] + y, g_ref[...], b_ref[...])


def _peer(x, wqt, sk, u_bf, vt_bf, g, b, tm, ec):
    n = x.shape[0]
    nk = sk.shape[2]
    ne = (nk * nk) // ec
    return pl.pallas_call(
        functools.partial(_peer_kernel, tm=tm, ec=ec, nk=nk),
        grid=(n // tm, ne),
        in_specs=[pl.BlockSpec((tm, D_MODEL), lambda i, e: (i, 0)),
                  pl.BlockSpec((D_MODEL, D_MODEL), lambda i, e: (0, 0)),
                  pl.BlockSpec(sk.shape, lambda i, e: (0, 0, 0, 0)),
                  pl.BlockSpec((ec, D_MODEL), lambda i, e: (e, 0)),
                  pl.BlockSpec((D_MODEL, ec), lambda i, e: (0, e)),
                  pl.BlockSpec((1, D_MODEL), lambda i, e: (0, 0)),
                  pl.BlockSpec((1, D_MODEL), lambda i, e: (0, 0))],
        out_specs=pl.BlockSpec((tm, D_MODEL), lambda i, e: (i, 0)),
        out_shape=jax.ShapeDtypeStruct((n, D_MODEL), F32),
        scratch_shapes=[pltpu.VMEM((D_MODEL, tm), BF16), pltpu.VMEM((PEER_HEADS * PEER_DKEY, tm), BF16),
                        pltpu.VMEM((D_MODEL, tm), F32)]
                       + [pltpu.VMEM((PEER_HEADS, nk, tm), F32)] * 2
                       + [pltpu.VMEM((PEER_HEADS, nk, tm), BF16)] * 2,
        compiler_params=_cparams(("parallel", "arbitrary")),
        name="peer_ln",
    )(x, wqt, sk, u_bf, vt_bf, g, b)


def _pick(n, pref):
    t = min(n, pref)
    while n % t:
        t //= 2
    return t


def kernel(x_prompt, x_sample, cache_a_k, cache_a_v, cache_a_logf, cache_b_k, cache_b_v, cache_b_idx_k, state_conv, page_table, w_in, b_fgate, conv_w, w_o, ln1_g, ln1_b, peer_wq, peer_subkeys, peer_u, peer_v, ln2_g, ln2_b):
    bp, s, d = x_prompt.shape
    nb, t, _ = x_sample.shape
    depth = w_in.shape[0]
    npool = cache_a_k.shape[1]
    npg = page_table.shape[1]
    past = npg * PAGE_SIZE
    nk = peer_subkeys.shape[3]
    topk_p = min(TOPK_MAX, s // 4)
    topk_s = min(TOPK_MAX, (past + t) // 4)

    n_p, n_s = bp * s, nb * t
    tm_p = _pick(n_p, 512)
    tm_s = _pick(n_s, 256)
    tq_fox = _pick(s, 128)
    tk_fox = _pick(s, 512)
    tq_dsa = _pick(s, 128)
    kc_dsa = _pick(s, 512)
    tc = _pick(s, 256)
    tm_merge = _pick(s, 512)
    tm_peer_p = _pick(n_p, 512)
    tm_peer_s = _pick(n_s, 256)
    ec = _pick(nk * nk, max(2048, nk))
    pps_i = _pick(npg, 8)
    pps_a = _pick(npg, 16)

    page_t = lambda c: jnp.transpose(c, (0, 1, 3, 4, 2)).reshape(depth, npool, -1, PAGE_SIZE)
    cak, cav, cbk, cbv = page_t(cache_a_k), page_t(cache_a_v), page_t(cache_b_k), page_t(cache_b_v)
    cidx = jnp.swapaxes(cache_b_idx_k, 2, 3)
    clf = jnp.pad(jnp.swapaxes(cache_a_logf, 2, 3), ((0, 0), (0, 0), (0, 8 - A_HEADS), (0, 0)))

    w_in_t = jnp.transpose(w_in, (2, 0, 1))
    xp = x_prompt.reshape(n_p, d)
    xs = x_sample.reshape(n_s, d)
    outs_p = [[] for _ in range(7)]
    outs_s = [[] for _ in range(7)]
    zero_buf = jnp.zeros((bp, CONV_W - 1, C_WIDTH), F32)
    row2 = lambda v: v.reshape(1, -1).astype(F32)

    for l in range(depth):
        wcat, bfrow = _build_wcat(w_in_t[:, l, :], b_fgate[l])
        wo_bf = w_o[l].astype(BF16)
        wqt = peer_wq[l].T.astype(BF16)
        sk = peer_subkeys[l].astype(BF16)
        u_bf = peer_u[l].astype(BF16)
        vt_bf = peer_v[l].T.astype(BF16)
        g1, b1, g2, b2 = row2(ln1_g[l]), row2(ln1_b[l]), row2(ln2_g[l]), row2(ln2_b[l])
        cw = conv_w[l].astype(F32)

        pr = _project(xp, wcat, bfrow, tm_p, seq=s)
        r3 = lambda a: a.reshape(bp, s, a.shape[-1])
        cum, cumt = _cumsum(r3(pr["misc"]), tc)
        oa = _fox_prompt(r3(pr["qa"]), r3(pr["kab"]), r3(pr["vab"]), cum, cumt, tq_fox, tk_fox)
        ob = _dsa_prompt(r3(pr["qi"]), r3(pr["misc"]), r3(pr["miscb"]), r3(pr["qb"]), r3(pr["kbb"]),
                         r3(pr["vbb"]), tq_dsa, kc_dsa, topk_p)
        x1, buf_p = _merge(oa, ob, r3(pr["u"]), r3(pr["gb"]), r3(xp), zero_buf, cw, wo_bf, g1, b1, tm_merge)
        xp = _peer(x1.reshape(n_p, d), wqt, sk, u_bf, vt_bf, g2, b2, tm_peer_p, ec)
        heads_t = lambda a: a.reshape(bp, A_HEADS, HEAD_DIM, s)
        for lst, val in zip(outs_p, (heads_t(pr["ka"]), heads_t(pr["va"]), pr["misct"][:, MISC_LF:MISC_WI, :],
                                     heads_t(pr["kb"]), heads_t(pr["vb"]), pr["misct"][:, :IDX_DIM, :], buf_p)):
            lst.append(val)

        ps = _project(xs, wcat, bfrow, tm_s)
        q3 = lambda a: a.reshape(nb, t, a.shape[-1])
        new_t = lambda a: jnp.pad(jnp.swapaxes(q3(a), 1, 2), ((0, 0), (0, 0), (0, PAGE_SIZE - t)))
        ki_new = new_t(ps["misc"][:, :IDX_DIM])
        lf_new = jnp.pad(new_t(ps["misc"][:, MISC_LF:MISC_WI]), ((0, 0), (0, 8 - A_HEADS), (0, 0)))
        bias = _sample_index(page_table, q3(ps["qi"]), q3(ps["misc"]), cidx, l, ki_new, pps_i, topk_s)
        oa_s, ob_s = _sample_attn(page_table, q3(ps["qa"]), q3(ps["qb"]), bias, (cak, cav, clf, cbk, cbv), l,
                                  (new_t(ps["ka"]), new_t(ps["va"]), lf_new, new_t(ps["kb"]), new_t(ps["vb"])),
                                  pps_a)
        x1s, buf_s = _merge(oa_s, ob_s, q3(ps["u"]), q3(ps["gb"]), q3(xs), state_conv[l].astype(F32), cw, wo_bf,
                            g1, b1, t)
        xs = _peer(x1s.reshape(n_s, d), wqt, sk, u_bf, vt_bf, g2, b2, tm_peer_s, ec)
        for lst, val in zip(outs_s, (ps["ka"].reshape(nb, t, A_HEADS, HEAD_DIM), ps["va"].reshape(nb, t, A_HEADS, HEAD_DIM),
                                     ps["misc"][:, MISC_LF:MISC_WI].reshape(nb, t, A_HEADS),
                                     ps["kb"].reshape(nb, t, B_HEADS, HEAD_DIM), ps["vb"].reshape(nb, t, B_HEADS, HEAD_DIM),
                                     ps["misc"][:, :IDX_DIM].reshape(nb, t, IDX_DIM), buf_s)):
            lst.append(val)

    pos_perm = {5: (0, 1, 4, 2, 3), 4: (0, 1, 3, 2)}
    prompt_out = [jnp.stack(v) for v in outs_p]
    prompt_out = [jnp.transpose(a, pos_perm[a.ndim]) for a in prompt_out[:6]] + prompt_out[6:]
    return ((xp.reshape(bp, s, d), xs.reshape(nb, t, d))
            + tuple(prompt_out) + tuple(jnp.stack(v) for v in outs_s))
```

```python
import functools

import jax
import jax.numpy as jnp
from jax import lax
from jax.experimental import pallas as pl
from jax.experimental.pallas import tpu as pltpu

F32 = jnp.float32
BF16 = jnp.bfloat16
I32 = jnp.int32
I16 = jnp.int16
HALF16 = 2 ** 15

D_MODEL = 1024
PAGE_SIZE = 128
HEAD_DIM = 64
A_HEADS = 6
B_HEADS = 6
A_WIDTH = A_HEADS * HEAD_DIM
B_WIDTH = B_HEADS * HEAD_DIM
C_WIDTH = 256
IDX_HEADS = 8
IDX_DIM = 64
TOPK_MAX = 256
CONV_W = 3
PEER_HEADS = 8
PEER_DKEY = 128
PEER_TOPK = 16
LN_EPS = 1e-5
DEPTH = 2
DN_ALPHA = (2 * DEPTH) ** 0.25
IN_WIDTHS = (A_WIDTH, A_WIDTH, A_WIDTH, A_HEADS,
             B_WIDTH, B_WIDTH, B_WIDTH, IDX_HEADS * IDX_DIM, IDX_DIM, IDX_HEADS,
             C_WIDTH, C_WIDTH, C_WIDTH)

LANES = 128
MXU_DEPTH = 256
BF16_ROWS = 16
VMEM_LIMIT = 56 * 1024 * 1024
NEG = -1e30
INT_MIN = -2 ** 31

MISC_LF = IDX_DIM
MISC_WI = IDX_DIM + A_HEADS

O_QA, O_KA, O_VA = 0, 384, 768
O_QB, O_KB, O_VB = 1152, 1536, 1920
O_QI = 2304
O_CIN, O_GB, O_GC = 3328, 3584, 3840
O_MISC = 4096
W_TOTAL = 4224


def _nt(a, b):
    return lax.dot_general(a, b, (((1,), (1,)), ((), ())), preferred_element_type=F32)


def _cparams(sem):
    return pltpu.CompilerParams(dimension_semantics=sem, vmem_limit_bytes=VMEM_LIMIT)


def _sort_key(x):
    bits = pltpu.bitcast(x, I32)
    return bits ^ ((bits >> 31) & 0x7FFFFFFF)


def _layer_norm_rows(z, g, b):
    mu = jnp.mean(z, axis=-1, keepdims=True)
    zc = z - mu
    var = jnp.mean(zc * zc, axis=-1, keepdims=True)
    return zc * lax.rsqrt(var + LN_EPS) * g + b


def _proj_kernel(x_ref, w_ref, bf_ref, qa_ref, ka_ref, va_ref, kab_ref, vab_ref,
                 qb_ref, kb_ref, vb_ref, kbb_ref, vbb_ref, qi_ref, u_ref, gb_ref,
                 misc_ref, miscb_ref, *maybe_t_refs, transposed):
    xb = x_ref[...].astype(BF16)

    def seg(off, width):
        return jnp.dot(xb, w_ref[:, off:off + width], preferred_element_type=F32)

    def put(ref, val):
        if transposed:
            ref[0] = val.T
        else:
            ref[...] = val

    qa_ref[...] = (seg(O_QA, A_WIDTH) * (HEAD_DIM ** -0.5)).astype(BF16)
    ka = seg(O_KA, A_WIDTH)
    put(ka_ref, ka)
    kab_ref[...] = ka.astype(BF16)
    va = seg(O_VA, A_WIDTH)
    put(va_ref, va)
    vab_ref[...] = va.astype(BF16)
    qb_ref[...] = (seg(O_QB, B_WIDTH) * (HEAD_DIM ** -0.5)).astype(BF16)
    kb = seg(O_KB, B_WIDTH)
    put(kb_ref, kb)
    kbb_ref[...] = kb.astype(BF16)
    vb = seg(O_VB, B_WIDTH)
    put(vb_ref, vb)
    vbb_ref[...] = vb.astype(BF16)
    qi_ref[...] = (seg(O_QI, IDX_HEADS * LANES) * (IDX_DIM ** -0.5)).astype(BF16)
    cin = seg(O_CIN, C_WIDTH)
    gb_ref[...] = seg(O_GB, C_WIDTH)
    gc = seg(O_GC, C_WIDTH)
    u_ref[...] = gc * cin
    misc = seg(O_MISC, LANES)
    lane = lax.broadcasted_iota(I32, misc.shape, 1)
    z = misc + bf_ref[...]
    logsig = jnp.minimum(z, 0.0) - jnp.log1p(jnp.exp(-jnp.abs(z)))
    misc = jnp.where((lane >= MISC_LF) & (lane < MISC_WI), logsig, misc)
    misc_ref[...] = misc
    miscb_ref[...] = misc.astype(BF16)
    if transposed:
        misct_ref, kabt_ref, kbbt_ref, miscbt_ref = maybe_t_refs
        misc_t = misc.T
        misct_ref[0] = misc_t
        miscbt_ref[0] = misc_t.astype(BF16)
        kabt_ref[0] = ka.T.astype(BF16)
        kbbt_ref[0] = kb.T.astype(BF16)


def _project(x, wcat, bfrow, tm, seq=None):
    n = x.shape[0]
    transposed = seq is not None
    row = lambda w: pl.BlockSpec((tm, w), lambda i: (i, 0))
    widths = [(A_WIDTH, BF16), (A_WIDTH, F32), (A_WIDTH, F32), (A_WIDTH, BF16), (A_WIDTH, BF16),
              (B_WIDTH, BF16), (B_WIDTH, F32), (B_WIDTH, F32), (B_WIDTH, BF16), (B_WIDTH, BF16),
              (IDX_HEADS * LANES, BF16), (C_WIDTH, F32), (C_WIDTH, F32), (LANES, F32), (LANES, BF16)]
    names = ["qa", "ka", "va", "kab", "vab", "qb", "kb", "vb", "kbb", "vbb", "qi", "u", "gb", "misc", "miscb"]
    out_specs = [row(w) for w, _ in widths]
    out_shape = [jax.ShapeDtypeStruct((n, w), dt) for w, dt in widths]
    if transposed:
        per = seq // tm
        tspec = lambda w: pl.BlockSpec((1, w, tm), lambda i: (i // per, 0, i % per))
        tshape = lambda w, dt=F32: jax.ShapeDtypeStruct((n // seq, w, seq), dt)
        for name in ("ka", "va", "kb", "vb"):
            k = names.index(name)
            out_specs[k], out_shape[k] = tspec(widths[k][0]), tshape(widths[k][0])
        for name, w, dt in (("misct", LANES, F32), ("kabt", A_WIDTH, BF16), ("kbbt", B_WIDTH, BF16),
                            ("miscbt", LANES, BF16)):
            names.append(name)
            out_specs.append(tspec(w))
            out_shape.append(tshape(w, dt))
    outs = pl.pallas_call(
        functools.partial(_proj_kernel, transposed=transposed),
        grid=(n // tm,),
        in_specs=[row(D_MODEL),
                  pl.BlockSpec((D_MODEL, W_TOTAL), lambda i: (0, 0)),
                  pl.BlockSpec((1, LANES), lambda i: (0, 0))],
        out_specs=out_specs,
        out_shape=out_shape,
        compiler_params=_cparams(("parallel",)),
        name="proj",
    )(x, wcat, bfrow)
    return dict(zip(names, outs))


def _wt_kernel(w_ref, o_ref):
    o_ref[...] = w_ref[...].T.astype(BF16)


def _build_wcat(w_in_t, b_f):
    parts, off = [], 0
    for w in IN_WIDTHS:
        parts.append(w_in_t[off:off + w, :])
        off += w
    qa, ka, va, fa, qb, kb, vb, qi, ki, wi, cin, gb, gc = parts
    d = w_in_t.shape[1]
    qi_pad = jnp.pad(qi.reshape(IDX_HEADS, IDX_DIM, d), ((0, 0), (0, LANES - IDX_DIM), (0, 0)))
    qi_pad = qi_pad.reshape(IDX_HEADS * LANES, d)
    misc = jnp.concatenate([ki, fa, wi, jnp.zeros((LANES - MISC_WI - IDX_HEADS, d), w_in_t.dtype)], axis=0)
    wcat_t = jnp.concatenate([qa, ka, va, qb, kb, vb, qi_pad, cin, gb, gc, misc], axis=0)
    wcat = pl.pallas_call(
        _wt_kernel,
        grid=(W_TOTAL // LANES,),
        in_specs=[pl.BlockSpec((LANES, d), lambda i: (i, 0))],
        out_specs=pl.BlockSpec((d, LANES), lambda i: (0, i)),
        out_shape=jax.ShapeDtypeStruct((d, W_TOTAL), BF16),
        compiler_params=_cparams(("parallel",)),
        name="w_in_transpose",
    )(wcat_t)
    bfrow = jnp.zeros((1, LANES), F32).at[0, MISC_LF:MISC_WI].set(b_f.astype(F32))
    return wcat, bfrow


def _cumsum_kernel(m_ref, cum_ref, cumt_ref, carry_ref, *, tc):
    @pl.when(pl.program_id(1) == 0)
    def _():
        carry_ref[...] = jnp.zeros_like(carry_ref)

    v = m_ref[0]
    r = lax.broadcasted_iota(I32, (tc, tc), 0)
    c = lax.broadcasted_iota(I32, (tc, tc), 1)
    tri = (c <= r).astype(F32)
    loc = jnp.dot(tri, v, precision=lax.Precision.HIGHEST, preferred_element_type=F32) + carry_ref[...]
    cum_ref[0] = loc
    cumt_ref[0] = loc.T
    carry_ref[...] = loc[tc - 1:tc, :]


def _cumsum(misc3, tc):
    b, s, _ = misc3.shape
    return pl.pallas_call(
        functools.partial(_cumsum_kernel, tc=tc),
        grid=(b, s // tc),
        in_specs=[pl.BlockSpec((1, tc, LANES), lambda i, j: (i, j, 0))],
        out_specs=[pl.BlockSpec((1, tc, LANES), lambda i, j: (i, j, 0)),
                   pl.BlockSpec((1, LANES, tc), lambda i, j: (i, 0, j))],
        out_shape=[jax.ShapeDtypeStruct((b, s, LANES), F32), jax.ShapeDtypeStruct((b, LANES, s), F32)],
        scratch_shapes=[pltpu.VMEM((1, LANES), F32)],
        compiler_params=_cparams(("parallel", "arbitrary")),
        name="logf_cumsum",
    )(misc3)


def _flash_update(carry, s, vblk, v_transposed=False):
    m, l, acc = carry
    m_new = jnp.maximum(m, jnp.max(s, axis=1, keepdims=True))
    alpha = jnp.exp(m - m_new)
    p = jnp.exp(s - m_new)
    l = alpha * l + jnp.sum(p, axis=1, keepdims=True)
    p16 = p.astype(BF16)
    pv = _nt(p16, vblk) if v_transposed else jnp.dot(p16, vblk, preferred_element_type=F32)
    return m_new, l, alpha * acc + pv


def _flash_init(rows, width):
    return (jnp.full((rows, 1), NEG, F32), jnp.zeros((rows, 1), F32), jnp.zeros((rows, width), F32))


def _split_pair(qp):
    lane = lax.broadcasted_iota(I32, qp.shape, 1)
    zero = jnp.zeros_like(qp)
    return jnp.where(lane < HEAD_DIM, qp, zero), jnp.where(lane >= HEAD_DIM, qp, zero)


def _join_pair(o0, o1):
    lane = lax.broadcasted_iota(I32, o0.shape, 1)
    return jnp.where(lane < HEAD_DIM, o0, o1)


def _pipelined_blocks(nblk, store_qk, step, state):
    store_qk(0, 0)

    def pair(jj, st):
        store_qk(1, 2 * jj + 1)
        st = step(2 * jj, st, 0, False)
        store_qk(0, 2 * jj + 2)
        return step(2 * jj + 1, st, 1, False)

    npair = (nblk - 1) // 2
    st = lax.fori_loop(0, npair, pair, state)

    def one_left(st):
        return step(nblk - 1, st, 0, True)

    def two_left(st):
        store_qk(1, nblk - 1)
        st = step(nblk - 2, st, 0, False)
        return step(nblk - 1, st, 1, True)

    return lax.cond(nblk - 2 * npair == 2, two_left, one_left, st)


def _fox_kernel(q_ref, kt_ref, v_ref, cum_ref, cumt_ref, o_ref, qk_s, *, tq, tk):
    p = pl.program_id(1)
    i = pl.program_id(2)
    qs = _split_pair(q_ref[0])
    lane = lax.broadcasted_iota(I32, (tq, LANES), 1)
    cumblk = cum_ref[0]
    cqs = [jnp.sum(jnp.where(lane == MISC_LF + 2 * p + hh, cumblk, 0.0), axis=1, keepdims=True)
           for hh in range(2)]
    row_g = i * tq + lax.broadcasted_iota(I32, (tq, tk), 0)
    col_l = lax.broadcasted_iota(I32, (tq, tk), 1)

    def store_qk(slot, j):
        off = pl.multiple_of(j * tk, tk)
        ktblk = kt_ref[0, :, pl.ds(off, tk)]
        for hh in range(2):
            qk_s[slot, hh] = jnp.dot(qs[hh], ktblk, preferred_element_type=F32)

    def step(j, state, slot, last):
        off = pl.multiple_of(j * tk, tk)
        vblk = v_ref[0, pl.ds(off, tk), :]
        new = []
        for hh in range(2):
            ck = cumt_ref[0, pl.ds(2 * p + hh, 1), pl.ds(off, tk)]
            s = qk_s[slot, hh] + (cqs[hh] - ck)
            if last:
                s = jnp.where(off + col_l <= row_g, s, NEG)
            new.append(_flash_update(state[hh], s, vblk))
        return tuple(new)

    nblk = (i * tq) // tk + 1
    init = (_flash_init(tq, LANES), _flash_init(tq, LANES))
    (_, l0, acc0), (_, l1, acc1) = _pipelined_blocks(nblk, store_qk, step, init)
    o_ref[0] = _join_pair(acc0 / l0, acc1 / l1).astype(BF16)


def _fox_prompt(qa, kab, vab, cum, cumt, tq, tk):
    b, s, _ = qa.shape
    npairs = A_HEADS // 2
    return pl.pallas_call(
        functools.partial(_fox_kernel, tq=tq, tk=tk),
        grid=(b, npairs, s // tq),
        in_specs=[pl.BlockSpec((1, tq, LANES), lambda bb, p, i: (bb, i, p)),
                  pl.BlockSpec((1, LANES, s), lambda bb, p, i: (bb, p, 0)),
                  pl.BlockSpec((1, s, LANES), lambda bb, p, i: (bb, 0, p)),
                  pl.BlockSpec((1, tq, LANES), lambda bb, p, i: (bb, i, 0)),
                  pl.BlockSpec((1, 8, s), lambda bb, p, i: (bb, MISC_LF // 8, 0))],
        out_specs=pl.BlockSpec((1, tq, LANES), lambda bb, p, i: (bb, i, p)),
        out_shape=jax.ShapeDtypeStruct((b, s, A_WIDTH), BF16),
        scratch_shapes=[pltpu.VMEM((2, 2, tq, tk), F32)],
        compiler_params=_cparams(("parallel", "parallel", "arbitrary")),
        name="fox_prompt",
    )(qa, kab, vab, cum, cumt)


def _kth_threshold(count_ge, rows, k, total):
    def cond(state):
        it, _, cnt = state
        return (it < 32) & (jnp.max(cnt) > k)

    def body(state):
        it, t, cnt = state
        cand = t | lax.shift_left(jnp.int32(1), 31 - it)
        c = count_ge(cand ^ INT_MIN)
        ok = c >= k
        return it + 1, jnp.where(ok, cand, t), jnp.where(ok, c, cnt)

    init = (jnp.int32(0), jnp.zeros((rows, 1), I32), jnp.full((rows, 1), total, I32))
    _, t, cnt = lax.while_loop(cond, body, init)
    return t ^ INT_MIN, cnt


def _tie_cutoff(count_tie_below, rows, need, nbits):
    def body(it, j):
        cand = j | lax.shift_left(jnp.int32(1), nbits - 1 - it)
        cnt = count_tie_below(cand)
        return jnp.where(cnt <= need, cand, j)
    return lax.fori_loop(0, nbits, body, jnp.zeros((rows, 1), I32))


def _dsa_kernel(qi_ref, misc_ref, kit_ref, qb_ref, kbt_ref, vb_ref, o_ref, key_ref, bias_ref, qk_s,
                khi_ref, klo_ref, *, tq, kc, kcc, topk, nbits):
    i = pl.program_id(1)
    nch = (i * tq + tq + kc - 1) // kc
    ncc = (i * tq + tq + kcc - 1) // kcc
    nch1 = ncc * (kcc // kc)
    w8 = misc_ref[0][:, MISC_WI:MISC_WI + IDX_HEADS] * (IDX_HEADS ** -0.5)
    row_g = i * tq + lax.broadcasted_iota(I32, (tq, kc), 0)
    col_l = lax.broadcasted_iota(I32, (tq, kc), 1)
    col_c = lax.broadcasted_iota(I32, (tq, kcc), 1)

    def p1(c, carry):
        off = pl.multiple_of(c * kc, kc)
        kit = kit_ref[0, :, pl.ds(off, kc)]
        acc = jnp.zeros((tq, kc), F32)
        for h in range(IDX_HEADS):
            r = jnp.dot(qi_ref[0, :, h * LANES:(h + 1) * LANES], kit, preferred_element_type=F32)
            acc = acc + w8[:, h:h + 1] * jnp.maximum(r, 0.0)
        acc = jnp.where(acc == 0.0, 0.0, acc)
        sc = jnp.where(off + col_l <= row_g, acc, -jnp.inf)
        key = _sort_key(sc)
        key_ref[:, pl.ds(off, kc)] = key
        khi_ref[:, pl.ds(off, kc)] = (key >> 16).astype(I16)
        return carry

    lax.fori_loop(0, nch1, p1, 0)

    def lane_groups(x):
        tot = x[:, 0:LANES]
        for g in range(1, kcc // LANES):
            tot = tot + x[:, g * LANES:(g + 1) * LANES]
        return tot

    def count(pred):
        def body(c, cnt):
            off = pl.multiple_of(c * kcc, kcc)
            return cnt + lane_groups(pred(key_ref[:, pl.ds(off, kcc)], off).astype(I32))
        cnt = lax.fori_loop(0, ncc, body, jnp.zeros((tq, LANES), I32))
        return jnp.sum(cnt, axis=1, keepdims=True)

    def count16(ref16, pred):
        def body(c, cnt):
            off = pl.multiple_of(c * kcc, kcc)
            hit = jnp.where(pred(ref16[:, pl.ds(off, kcc)]), jnp.int16(1), jnp.int16(0))
            return cnt + lane_groups(hit)
        cnt = lax.fori_loop(0, ncc, body, jnp.zeros((tq, LANES), I16))
        return jnp.sum(cnt.astype(I32), axis=1, keepdims=True)

    def bit_step(t, cnt, it, count_ge16, base):
        cand = t | lax.shift_left(jnp.int32(1), 15 - it)
        c = base + count_ge16((cand - HALF16).astype(I16))
        ok = c >= topk
        return jnp.where(ok, cand, t), jnp.where(ok, c, cnt)

    zero = jnp.zeros((tq, 1), I32)
    t_hi, n_hi_ge = lax.fori_loop(
        0, 16,
        lambda it, st: bit_step(st[0], st[1], it, lambda th: count16(khi_ref, lambda kk: kk >= th), 0),
        (zero, jnp.full((tq, 1), ncc * kcc, I32)))
    th_s = t_hi - HALF16

    def low_half():
        th16 = th_s.astype(I16)
        n_hi_gt = count16(khi_ref, lambda kk: kk > th16)

        def build(c, carry):
            off = pl.multiple_of(c * kcc, kcc)
            lo = ((key_ref[:, pl.ds(off, kcc)] & 0xFFFF) - HALF16).astype(I16)
            klo_ref[:, pl.ds(off, kcc)] = jnp.where(khi_ref[:, pl.ds(off, kcc)] == th16, lo, jnp.int16(-HALF16))
            return carry

        lax.fori_loop(0, ncc, build, 0)

        def cond(st):
            it, _, cnt = st
            return (it < 16) & (jnp.max(cnt) > topk)

        def body(st):
            it, t, cnt = st
            t, cnt = bit_step(t, cnt, it, lambda th: count16(klo_ref, lambda kk: kk >= th), n_hi_gt)
            return it + 1, t, cnt

        _, t_lo, cnt = lax.while_loop(cond, body, (jnp.int32(0), zero, n_hi_ge))
        return t_lo, cnt

    t_lo, n_ge = lax.cond(jnp.max(n_hi_ge) > topk, low_half, lambda: (zero, n_hi_ge))
    thr = lax.shift_left(th_s, 16) | t_lo
    big = jnp.full((tq, 1), 2 ** nbits - 1, I32)

    def tie_cutoff():
        need = topk - count(lambda kk, off: kk > thr)
        return _tie_cutoff(lambda j: count(lambda kk, off: (kk == thr) & (off + col_c < j)), tq, need, nbits)

    jstar = lax.cond(jnp.max(n_ge) > topk, tie_cutoff, lambda: big)

    def p2(c, carry):
        off = pl.multiple_of(c * kc, kc)
        kk = key_ref[:, pl.ds(off, kc)]
        colg = off + col_l
        sel = (kk > thr) | ((kk == thr) & (colg < jstar))
        bias_ref[:, pl.ds(off, kc)] = jnp.where(sel & (colg <= row_g), 0.0, NEG)
        return carry

    lax.fori_loop(0, nch, p2, 0)

    for pr in range(B_HEADS // 2):
        qs = _split_pair(qb_ref[0, :, pr * LANES:(pr + 1) * LANES])

        def store_qk(slot, c):
            off = pl.multiple_of(c * kc, kc)
            ktblk = kbt_ref[0, pr * LANES:(pr + 1) * LANES, pl.ds(off, kc)]
            for hh in range(2):
                qk_s[slot, hh] = jnp.dot(qs[hh], ktblk, preferred_element_type=F32)

        def step(c, state, slot, last):
            off = pl.multiple_of(c * kc, kc)
            vblk = vb_ref[0, pl.ds(off, kc), pr * LANES:(pr + 1) * LANES]
            bias = bias_ref[:, pl.ds(off, kc)]
            return tuple(_flash_update(state[hh], qk_s[slot, hh] + bias, vblk) for hh in range(2))

        init = (_flash_init(tq, LANES), _flash_init(tq, LANES))
        (_, l0, acc0), (_, l1, acc1) = _pipelined_blocks(nch, store_qk, step, init)
        o_ref[0, :, pr * LANES:(pr + 1) * LANES] = _join_pair(acc0 / l0, acc1 / l1).astype(BF16)


def _dsa_prompt(qi, misc, miscb, qb, kbb, vbb, tq, kc, topk):
    b, s, _ = qb.shape
    nbits = max(1, (s - 1).bit_length()) + 1
    kcc = 2 * kc if s % (2 * kc) == 0 else kc
    qspec = lambda w: pl.BlockSpec((1, tq, w), lambda bb, i: (bb, i, 0))
    full = lambda w: pl.BlockSpec((1, s, w), lambda bb, i: (bb, 0, 0))
    full_t = lambda w: pl.BlockSpec((1, w, s), lambda bb, i: (bb, 0, 0))
    return pl.pallas_call(
        functools.partial(_dsa_kernel, tq=tq, kc=kc, kcc=kcc, topk=topk, nbits=nbits),
        grid=(b, s // tq),
        in_specs=[qspec(IDX_HEADS * LANES), qspec(LANES), full_t(LANES), qspec(B_WIDTH), full_t(B_WIDTH),
                  full(B_WIDTH)],
        out_specs=qspec(B_WIDTH),
        out_shape=jax.ShapeDtypeStruct((b, s, B_WIDTH), BF16),
        scratch_shapes=[pltpu.VMEM((tq, s), I32), pltpu.VMEM((tq, s), F32), pltpu.VMEM((2, 2, tq, kc), F32),
                        pltpu.VMEM((tq, s), I16), pltpu.VMEM((tq, s), I16)],
        compiler_params=_cparams(("parallel", "arbitrary")),
        name="dsa_prompt",
    )(qi, misc, miscb, qb, kbb, vbb)


def _sidx_kernel(pt_ref, qi_ref, misc_ref, *rest, pps, nps, past, topk, nbits):
    page_refs = rest[:pps]
    knew_ref, bias_ref, key_ref, qst_ref, wst_ref = rest[pps:]
    j = pl.program_id(1)
    t = qi_ref.shape[1]
    lp = key_ref.shape[1]

    @pl.when(j == 0)
    def _():
        qf = qi_ref[0].astype(F32)
        qst_ref[...] = jnp.concatenate([qf[:, h * LANES:h * LANES + IDX_DIM] for h in range(IDX_HEADS)],
                                       axis=0).astype(BF16)
        w8 = misc_ref[0][:, MISC_WI:MISC_WI + IDX_HEADS] * (IDX_HEADS ** -0.5)
        wst_ref[...] = jnp.concatenate([w8[:, h:h + 1] for h in range(IDX_HEADS)], axis=0)

    def score(kt):
        r = jnp.dot(qst_ref[...], kt.astype(BF16), preferred_element_type=F32)
        r = wst_ref[...] * jnp.maximum(r, 0.0)
        acc = r[0:t, :]
        for h in range(1, IDX_HEADS):
            acc = acc + r[h * t:(h + 1) * t, :]
        return jnp.where(acc == 0.0, 0.0, acc)

    @pl.when(j < nps)
    def _():
        kt = jnp.concatenate([page_refs[k][0, 0] for k in range(pps)], axis=1)
        off = pl.multiple_of(j * (pps * PAGE_SIZE), pps * PAGE_SIZE)
        key_ref[:, pl.ds(off, pps * PAGE_SIZE)] = _sort_key(score(kt))

    @pl.when(j == nps)
    def _():
        rowi = lax.broadcasted_iota(I32, (t, PAGE_SIZE), 0)
        coli = lax.broadcasted_iota(I32, (t, PAGE_SIZE), 1)
        sc = jnp.where(coli <= rowi, score(knew_ref[0]), -jnp.inf)
        key_ref[:, past:past + PAGE_SIZE] = _sort_key(sc)

        keys = key_ref[...]
        colg = lax.broadcasted_iota(I32, (t, lp), 1)
        count = lambda pred: jnp.sum(pred.astype(I32), axis=1, keepdims=True)
        thr, _ = _kth_threshold(lambda th: count(keys >= th), t, topk, lp)
        need = topk - count(keys > thr)
        jstar = _tie_cutoff(lambda jj: count((keys == thr) & (colg < jj)), t, need, nbits)
        sel = (keys > thr) | ((keys == thr) & (colg < jstar))
        rowg = past + lax.broadcasted_iota(I32, (t, lp), 0)
        bias_ref[0] = jnp.where(sel & (colg <= rowg), 0.0, NEG)


def _sample_index(page_table, qi3, misc3, cache_idx_t, layer, knew_t, pps, topk):
    nb, t, _ = qi3.shape
    npg = page_table.shape[1]
    nps = npg // pps
    past = npg * PAGE_SIZE
    lp = past + PAGE_SIZE
    nbits = lp.bit_length() + 1

    def page_map(k):
        return lambda b, j, pt: (layer, pt[b * npg + jnp.minimum(j, nps - 1) * pps + k], 0, 0)

    bmap = lambda b, j, pt: (b, 0, 0)
    grid_spec = pltpu.PrefetchScalarGridSpec(
        num_scalar_prefetch=1,
        grid=(nb, nps + 1),
        in_specs=[pl.BlockSpec((1, t, IDX_HEADS * LANES), bmap),
                  pl.BlockSpec((1, t, LANES), bmap)]
                 + [pl.BlockSpec((1, 1, IDX_DIM, PAGE_SIZE), page_map(k)) for k in range(pps)]
                 + [pl.BlockSpec((1, IDX_DIM, PAGE_SIZE), bmap)],
        out_specs=pl.BlockSpec((1, t, lp), bmap),
        scratch_shapes=[pltpu.VMEM((t, lp), I32), pltpu.VMEM((IDX_HEADS * t, IDX_DIM), BF16),
                        pltpu.VMEM((IDX_HEADS * t, 1), F32)],
    )
    return pl.pallas_call(
        functools.partial(_sidx_kernel, pps=pps, nps=nps, past=past, topk=topk, nbits=nbits),
        grid_spec=grid_spec,
        out_shape=jax.ShapeDtypeStruct((nb, t, lp), F32),
        compiler_params=_cparams(("parallel", "arbitrary")),
        name="sample_index",
    )(page_table.reshape(-1), qi3, misc3, *([cache_idx_t] * pps), knew_t)


def _sattn_kernel(pt_ref, qa_ref, qb_ref, bias_ref, *rest, pps, nps):
    grp = lambda g: rest[g * pps:(g + 1) * pps]
    ak_refs, av_refs, lf_refs, bk_refs, bv_refs = (grp(g) for g in range(5))
    (akn_ref, avn_ref, lfn_ref, bkn_ref, bvn_ref, oa_ref, ob_ref,
     qbd_a, qbd_b, ma, la, acca, mb, lb, accb, carry_ref) = rest[5 * pps:]
    j = pl.program_id(1)
    t = qa_ref.shape[1]
    rows = A_HEADS * t
    lane_q = lax.broadcasted_iota(I32, (t, A_WIDTH), 1)

    def block_diag(q):
        qf = q.astype(F32)
        parts = [jnp.where((lane_q >= h * HEAD_DIM) & (lane_q < (h + 1) * HEAD_DIM), qf, 0.0)
                 for h in range(A_HEADS)]
        return jnp.concatenate(parts, axis=0).astype(BF16)

    @pl.when(j == 0)
    def _():
        qbd_a[...] = block_diag(qa_ref[0])
        qbd_b[...] = block_diag(qb_ref[0])
        for r in (ma, mb):
            r[...] = jnp.full(r.shape, NEG, F32)
        for r in (la, lb, acca, accb, carry_ref):
            r[...] = jnp.zeros(r.shape, F32)

    r_i = lax.broadcasted_iota(I32, (PAGE_SIZE, PAGE_SIZE), 0)
    c_i = lax.broadcasted_iota(I32, (PAGE_SIZE, PAGE_SIZE), 1)
    triu = (r_i <= c_i).astype(F32)

    def cat(refs, dtype):
        pages = [r[0, 0] if len(r.shape) == 4 else r[0] for r in refs]
        x = pages[0] if len(pages) == 1 else jnp.concatenate(pages, axis=1)
        return x.astype(dtype)

    def update(m_ref, l_ref, acc_ref, s, vt):
        m, l, acc = _flash_update((m_ref[...], l_ref[...], acc_ref[...]), s, vt, v_transposed=True)
        m_ref[...] = m
        l_ref[...] = l
        acc_ref[...] = acc

    def process(ak, av, lf, bk, bv, bias8, mask):
        run = carry_ref[...]
        cls = []
        for r in lf:
            page = r[0, 0] if len(r.shape) == 4 else r[0]
            cl = jnp.dot(page, triu, precision=lax.Precision.HIGHEST, preferred_element_type=F32) + run
            run = jnp.broadcast_to(cl[:, PAGE_SIZE - 1:PAGE_SIZE], cl.shape)
            cls.append(cl)
        carry_ref[...] = run
        ck = cls[0] if len(cls) == 1 else jnp.concatenate(cls, axis=1)
        width = ck.shape[1]
        ck_rows = jnp.concatenate([jnp.broadcast_to(ck[h:h + 1, :], (t, width)) for h in range(A_HEADS)], axis=0)
        sa = jnp.dot(qbd_a[...], cat(ak, BF16), preferred_element_type=F32) - ck_rows
        if mask is not None:
            sa = jnp.where(mask, sa, NEG)
        update(ma, la, acca, sa, cat(av, BF16))
        sb = jnp.dot(qbd_b[...], cat(bk, BF16), preferred_element_type=F32)
        sb = sb + jnp.concatenate([bias8] * B_HEADS, axis=0)
        update(mb, lb, accb, sb, cat(bv, BF16))

    @pl.when(j < nps)
    def _():
        off = pl.multiple_of(j * (pps * PAGE_SIZE), pps * PAGE_SIZE)
        process(ak_refs, av_refs, lf_refs, bk_refs, bv_refs, bias_ref[0, :, pl.ds(off, pps * PAGE_SIZE)], None)

    @pl.when(j == nps)
    def _():
        past = nps * pps * PAGE_SIZE
        rowi = lax.broadcasted_iota(I32, (rows, PAGE_SIZE), 0)
        coli = lax.broadcasted_iota(I32, (rows, PAGE_SIZE), 1)
        mask = coli <= (rowi & (t - 1))
        process([akn_ref], [avn_ref], [lfn_ref], [bkn_ref], [bvn_ref],
                bias_ref[0, :, past:past + PAGE_SIZE], mask)

        def gather_heads(acc_ref, l_ref):
            o = jnp.zeros((t, A_WIDTH), F32)
            for h in range(A_HEADS):
                blk = acc_ref[h * t:(h + 1) * t, :] / l_ref[h * t:(h + 1) * t, :]
                o = jnp.where((lane_q >= h * HEAD_DIM) & (lane_q < (h + 1) * HEAD_DIM), blk, o)
            return o.astype(BF16)

        oa_ref[0] = gather_heads(acca, la)
        ob_ref[0] = gather_heads(accb, lb)


def _sample_attn(page_table, qa3, qb3, bias, caches, layer, news, pps):
    nb, t, _ = qa3.shape
    assert t & (t - 1) == 0
    npg = page_table.shape[1]
    nps = npg // pps
    lp = bias.shape[2]
    rows = A_HEADS * t

    def page_map(k):
        return lambda b, j, pt: (layer, pt[b * npg + jnp.minimum(j, nps - 1) * pps + k], 0, 0)

    bmap = lambda b, j, pt: (b, 0, 0)
    cache_specs, cache_args = [], []
    for c in caches:
        blk = (1, 1) + c.shape[2:]
        for k in range(pps):
            cache_specs.append(pl.BlockSpec(blk, page_map(k)))
            cache_args.append(c)
    new_specs = [pl.BlockSpec((1,) + a.shape[1:], bmap) for a in news]
    grid_spec = pltpu.PrefetchScalarGridSpec(
        num_scalar_prefetch=1,
        grid=(nb, nps + 1),
        in_specs=[pl.BlockSpec((1, t, A_WIDTH), bmap), pl.BlockSpec((1, t, B_WIDTH), bmap),
                  pl.BlockSpec((1, t, lp), bmap)] + cache_specs + new_specs,
        out_specs=[pl.BlockSpec((1, t, A_WIDTH), bmap), pl.BlockSpec((1, t, B_WIDTH), bmap)],
        scratch_shapes=[pltpu.VMEM((rows, A_WIDTH), BF16), pltpu.VMEM((rows, B_WIDTH), BF16),
                        pltpu.VMEM((rows, 1), F32), pltpu.VMEM((rows, 1), F32), pltpu.VMEM((rows, A_WIDTH), F32),
                        pltpu.VMEM((rows, 1), F32), pltpu.VMEM((rows, 1), F32), pltpu.VMEM((rows, B_WIDTH), F32),
                        pltpu.VMEM((8, PAGE_SIZE), F32)],
    )
    return pl.pallas_call(
        functools.partial(_sattn_kernel, pps=pps, nps=nps),
        grid_spec=grid_spec,
        out_shape=[jax.ShapeDtypeStruct((nb, t, A_WIDTH), BF16), jax.ShapeDtypeStruct((nb, t, B_WIDTH), BF16)],
        compiler_params=_cparams(("parallel", "arbitrary")),
        name="sample_attn",
    )(page_table.reshape(-1), qa3, qb3, bias, *cache_args, *news)


def _merge_kernel(oa_ref, ob_ref, u_ref, gb_ref, x_ref, buf_ref, cw_ref, wo_ref, g_ref, b_ref,
                  y_ref, nbuf_ref, carry_ref, *, tm):
    j = pl.program_id(1)

    @pl.when(j == 0)
    def _():
        carry_ref[...] = buf_ref[0]

    u = u_ref[0]
    rowi = lax.broadcasted_iota(I32, u.shape, 0)
    c0 = carry_ref[0:1, :]
    c1 = carry_ref[1:2, :]
    u1 = jnp.where(rowi == 0, c1, pltpu.roll(u, 1, 0))
    u2 = jnp.where(rowi == 0, c0, jnp.where(rowi == 1, c1, pltpu.roll(u, 2, 0)))
    yc = cw_ref[0:1, :] * u2 + cw_ref[1:2, :] * u1 + cw_ref[2:3, :] * u
    oc = (gb_ref[0] * yc).astype(BF16)
    new_carry = u[tm - 2:tm, :]
    carry_ref[...] = new_carry
    nbuf_ref[0] = new_carry
    mix = jnp.dot(oa_ref[0], wo_ref[0:A_WIDTH, :], preferred_element_type=F32)
    mix = mix + jnp.dot(ob_ref[0], wo_ref[A_WIDTH:A_WIDTH + B_WIDTH, :], preferred_element_type=F32)
    mix = mix + jnp.dot(oc, wo_ref[A_WIDTH + B_WIDTH:, :], preferred_element_type=F32)
    y_ref[0] = _layer_norm_rows(DN_ALPHA * x_ref[0] + mix, g_ref[...], b_ref[...])


def _merge(oa, ob, u, gb, x, buf, conv_w, wo_bf, g, b, tm):
    nseq, t, _ = x.shape
    blk = lambda w: pl.BlockSpec((1, tm, w), lambda s, j: (s, j, 0))
    const = lambda r, w: pl.BlockSpec((r, w), lambda s, j: (0, 0))
    return pl.pallas_call(
        functools.partial(_merge_kernel, tm=tm),
        grid=(nseq, t // tm),
        in_specs=[blk(A_WIDTH), blk(B_WIDTH), blk(C_WIDTH), blk(C_WIDTH), blk(D_MODEL),
                  pl.BlockSpec((1, CONV_W - 1, C_WIDTH), lambda s, j: (s, 0, 0)),
                  const(CONV_W, C_WIDTH), const(D_MODEL, D_MODEL), const(1, D_MODEL), const(1, D_MODEL)],
        out_specs=[blk(D_MODEL), pl.BlockSpec((1, CONV_W - 1, C_WIDTH), lambda s, j: (s, 0, 0))],
        out_shape=[jax.ShapeDtypeStruct((nseq, t, D_MODEL), F32),
                   jax.ShapeDtypeStruct((nseq, CONV_W - 1, C_WIDTH), F32)],
        scratch_shapes=[pltpu.VMEM((CONV_W - 1, C_WIDTH), F32)],
        compiler_params=_cparams(("parallel", "arbitrary")),
        name="merge_ln",
    )(oa, ob, u, gb, x, buf, conv_w, wo_bf, g, b)


def _top16_rows(s_list):
    nk, tm = s_list[0].shape

    def extract(s, first_occurrence):
        idx = lax.broadcasted_iota(I32, (nk, tm), 0).astype(F32)
        work = s
        rank = jnp.full((nk, tm), 99.0, F32)
        vals = []
        for r in range(PEER_TOPK):
            m = jnp.max(work, axis=0, keepdims=True)
            hit = work == m
            if first_occurrence:
                hit = idx == jnp.min(jnp.where(hit, idx, 1e9), axis=0, keepdims=True)
            rank = jnp.where(hit, float(r), rank)
            work = jnp.where(hit, -jnp.inf, work)
            vals.append(m)
        return jnp.concatenate(vals, axis=0), rank

    fast = [extract(s, False) for s in s_list]
    bad = jnp.zeros((1, tm), F32)
    for _, rank in fast:
        nsel = jnp.sum(jnp.where(rank < 99.0, 1.0, 0.0), axis=0, keepdims=True)
        bad = jnp.maximum(bad, jnp.where(nsel != float(PEER_TOPK), 1.0, 0.0))
    flat = lambda pairs: tuple(x for pair in pairs for x in pair)
    out = lax.cond(jnp.max(bad) > 0.0, lambda: flat([extract(s, True) for s in s_list]), lambda: flat(fast))
    return [(out[2 * i], out[2 * i + 1]) for i in range(len(s_list))]


def _peer_kernel(x_ref, wqt_ref, sk_ref, u_ref, vt_ref, g_ref, b_ref, o_ref,
                 xt_s, qt_s, outt_s, a_s, n1_s, bb_s, r2_s, *, tm, ec, nk):
    e = pl.program_id(1)
    ne = pl.num_programs(1)
    k = PEER_TOPK
    half = PEER_DKEY // 2

    @pl.when(e == 0)
    def _():
        xt = x_ref[...].T.astype(BF16)
        xt_s[...] = xt
        qt_s[...] = jnp.dot(wqt_ref[...], xt, preferred_element_type=F32).astype(BF16)
        rho = lax.broadcasted_iota(I32, (80, tm), 0)
        mid = rho - 16
        r1 = jnp.where(rho < 16, rho, jnp.where(rho < 72, mid & 7, 0))
        r2 = jnp.where(rho < 16, 0, jnp.where(rho < 72, (mid >> 3) + 1, rho - 64))
        pos = (r1 * k + r2).astype(F32)
        valid = (r1 + 1) * (r2 + 1) <= k
        row8 = lax.broadcasted_iota(I32, (8, tm), 0)
        def head(h, carry):
            q1 = qt_s[pl.ds(pl.multiple_of(2 * h * half, half), half), :]
            q2 = qt_s[pl.ds(pl.multiple_of((2 * h + 1) * half, half), half), :]
            s1 = jnp.dot(sk_ref[h, 0], q1, preferred_element_type=F32)
            s2 = jnp.dot(sk_ref[h, 1], q2, preferred_element_type=F32)
            (v1, rank1), (v2, rank2) = _top16_rows([s1, s2])
            slabs = [v1 + v2[0:1, :]]
            for j in range(1, 8):
                slabs.append(v1[0:8, :] + v2[j:j + 1, :])
            slabs.append(v1[0:1, :] + v2[8:16, :])
            cand = jnp.where(valid, jnp.concatenate(slabs, axis=0), -jnp.inf)
            cmax = v1[0:1, :] + v2[0:1, :]
            sel = jnp.zeros((80, tm), F32)
            z = jnp.zeros((1, tm), F32)
            for r in range(k):
                m = jnp.max(cand, axis=0, keepdims=True)
                first = jnp.min(jnp.where(cand == m, pos, 1e9), axis=0, keepdims=True)
                hit = pos == first
                sel = jnp.where(hit, 1.0, sel)
                cand = jnp.where(hit, -jnp.inf, cand)
                z = z + jnp.exp(m - cmax)
            top8 = sel[0:8, :]
            for j in range(1, 8):
                top8 = top8 + sel[8 + 8 * j:16 + 8 * j, :]
            extra = jnp.sum(sel[72:80, :], axis=0, keepdims=True)
            top8 = top8 + jnp.where(row8 == 0, extra, 0.0)
            ncount = jnp.concatenate([top8, sel[8:16, :]], axis=0)
            n1 = jnp.zeros((nk, tm), F32)
            for r in range(k):
                n1 = jnp.where(rank1 == float(r), ncount[r:r + 1, :], n1)
            a_s[h] = jnp.exp(s1 - v1[0:1, :])
            n1_s[h] = n1
            bb_s[h] = (jnp.exp(s2 - v2[0:1, :]) / z).astype(BF16)
            r2_s[h] = rank2.astype(BF16)
            return carry

        lax.fori_loop(0, PEER_HEADS, head, 0)
        outt_s[...] = jnp.zeros_like(outt_s)

    per = ec // nk
    gps = max(1, min(per, MXU_DEPTH // nk))
    sl = gps * nk
    xt = xt_s[...]
    acc = None
    nsl = ec // sl

    def rows16(row):
        if nk % BF16_ROWS:
            return jnp.broadcast_to(row, (nk, tm)).astype(BF16)
        tile = jnp.broadcast_to(row, (BF16_ROWS, tm)).astype(BF16)
        return jnp.concatenate([tile] * (nk // BF16_ROWS), axis=0) if nk > BF16_ROWS else tile

    hidden = lambda sb: jnp.dot(u_ref[sb * sl:(sb + 1) * sl, :], xt, preferred_element_type=F32)
    ht_next = hidden(0)
    for sb in range(nsl):
        ht = ht_next
        if sb + 1 < nsl:
            ht_next = hidden(sb + 1)
        act = (0.5 * ht * (1.0 + lax.erf(ht * (0.5 ** 0.5)))).astype(BF16)
        coefs = []
        for g in range(gps):
            i1 = e * per + sb * gps + g
            gate = jnp.zeros((nk, tm), BF16)
            for h in range(PEER_HEADS):
                arow = rows16(a_s[h, pl.ds(i1, 1), :])
                nrow = rows16(n1_s[h, pl.ds(i1, 1), :])
                gate = gate + jnp.where(r2_s[h] < nrow, bb_s[h], jnp.zeros((), BF16)) * arow
            coefs.append(gate * act[g * nk:(g + 1) * nk, :])
        coef = jnp.concatenate(coefs, axis=0) if gps > 1 else coefs[0]
        part = jnp.dot(vt_ref[:, sb * sl:(sb + 1) * sl], coef, preferred_element_type=F32)
        acc = part if acc is None else acc + part
    outt_s[...] += acc

    @pl.when(e == ne - 1)
    def _():
        y = outt_s[...].T
        o_ref[...] = _layer_norm_rows(DN_ALPHA * x_ref[...] + y, g_ref[...], b_ref[...])


def _peer(x, wqt, sk, u_bf, vt_bf, g, b, tm, ec):
    n = x.shape[0]
    nk = sk.shape[2]
    ne = (nk * nk) // ec
    return pl.pallas_call(
        functools.partial(_peer_kernel, tm=tm, ec=ec, nk=nk),
        grid=(n // tm, ne),
        in_specs=[pl.BlockSpec((tm, D_MODEL), lambda i, e: (i, 0)),
                  pl.BlockSpec((D_MODEL, D_MODEL), lambda i, e: (0, 0)),
                  pl.BlockSpec(sk.shape, lambda i, e: (0, 0, 0, 0)),
                  pl.BlockSpec((ec, D_MODEL), lambda i, e: (e, 0)),
                  pl.BlockSpec((D_MODEL, ec), lambda i, e: (0, e)),
                  pl.BlockSpec((1, D_MODEL), lambda i, e: (0, 0)),
                  pl.BlockSpec((1, D_MODEL), lambda i, e: (0, 0))],
        out_specs=pl.BlockSpec((tm, D_MODEL), lambda i, e: (i, 0)),
        out_shape=jax.ShapeDtypeStruct((n, D_MODEL), F32),
        scratch_shapes=[pltpu.VMEM((D_MODEL, tm), BF16), pltpu.VMEM((PEER_HEADS * PEER_DKEY, tm), BF16),
                        pltpu.VMEM((D_MODEL, tm), F32)]
                       + [pltpu.VMEM((PEER_HEADS, nk, tm), F32)] * 2
                       + [pltpu.VMEM((PEER_HEADS, nk, tm), BF16)] * 2,
        compiler_params=_cparams(("parallel", "arbitrary")),
        name="peer_ln",
    )(x, wqt, sk, u_bf, vt_bf, g, b)


def _pick(n, pref):
    t = min(n, pref)
    while n % t:
        t //= 2
    return t


def kernel(x_prompt, x_sample, cache_a_k, cache_a_v, cache_a_logf, cache_b_k, cache_b_v, cache_b_idx_k, state_conv, page_table, w_in, b_fgate, conv_w, w_o, ln1_g, ln1_b, peer_wq, peer_subkeys, peer_u, peer_v, ln2_g, ln2_b):
    bp, s, d = x_prompt.shape
    nb, t, _ = x_sample.shape
    depth = w_in.shape[0]
    npool = cache_a_k.shape[1]
    npg = page_table.shape[1]
    past = npg * PAGE_SIZE
    nk = peer_subkeys.shape[3]
    topk_p = min(TOPK_MAX, s // 4)
    topk_s = min(TOPK_MAX, (past + t) // 4)

    n_p, n_s = bp * s, nb * t
    tm_p = _pick(n_p, 512)
    tm_s = _pick(n_s, 256)
    tq_fox = _pick(s, 128)
    tk_fox = _pick(s, 512)
    tq_dsa = _pick(s, 128)
    kc_dsa = _pick(s, 512)
    tc = _pick(s, 256)
    tm_merge = _pick(s, 512)
    tm_peer_p = _pick(n_p, 512)
    tm_peer_s = _pick(n_s, 256)
    ec = _pick(nk * nk, max(2048, nk))
    pps_i = _pick(npg, 16)
    pps_a = _pick(npg, 16)

    page_t = lambda c: jnp.transpose(c, (0, 1, 3, 4, 2)).reshape(depth, npool, -1, PAGE_SIZE)
    cak, cav, cbk, cbv = page_t(cache_a_k), page_t(cache_a_v), page_t(cache_b_k), page_t(cache_b_v)
    cidx = jnp.swapaxes(cache_b_idx_k, 2, 3)
    clf = jnp.pad(jnp.swapaxes(cache_a_logf, 2, 3), ((0, 0), (0, 0), (0, 8 - A_HEADS), (0, 0)))

    w_in_t = jnp.transpose(w_in, (2, 0, 1))
    xp = x_prompt.reshape(n_p, d)
    xs = x_sample.reshape(n_s, d)
    outs_p = [[] for _ in range(7)]
    outs_s = [[] for _ in range(7)]
    zero_buf = jnp.zeros((bp, CONV_W - 1, C_WIDTH), F32)
    row2 = lambda v: v.reshape(1, -1).astype(F32)

    for l in range(depth):
        wcat, bfrow = _build_wcat(w_in_t[:, l, :], b_fgate[l])
        wo_bf = w_o[l].astype(BF16)
        wqt = peer_wq[l].T.astype(BF16)
        sk = peer_subkeys[l].astype(BF16)
        u_bf = peer_u[l].astype(BF16)
        vt_bf = peer_v[l].T.astype(BF16)
        g1, b1, g2, b2 = row2(ln1_g[l]), row2(ln1_b[l]), row2(ln2_g[l]), row2(ln2_b[l])
        cw = conv_w[l].astype(F32)

        pr = _project(xp, wcat, bfrow, tm_p, seq=s)
        r3 = lambda a: a.reshape(bp, s, a.shape[-1])
        cum, cumt = _cumsum(r3(pr["misc"]), tc)
        oa = _fox_prompt(r3(pr["qa"]), pr["kabt"], r3(pr["vab"]), cum, cumt, tq_fox, tk_fox)
        ob = _dsa_prompt(r3(pr["qi"]), r3(pr["misc"]), pr["miscbt"], r3(pr["qb"]), pr["kbbt"],
                         r3(pr["vbb"]), tq_dsa, kc_dsa, topk_p)
        x1, buf_p = _merge(oa, ob, r3(pr["u"]), r3(pr["gb"]), r3(xp), zero_buf, cw, wo_bf, g1, b1, tm_merge)
        xp = _peer(x1.reshape(n_p, d), wqt, sk, u_bf, vt_bf, g2, b2, tm_peer_p, ec)
        heads_t = lambda a: a.reshape(bp, A_HEADS, HEAD_DIM, s)
        for lst, val in zip(outs_p, (heads_t(pr["ka"]), heads_t(pr["va"]), pr["misct"][:, MISC_LF:MISC_WI, :],
                                     heads_t(pr["kb"]), heads_t(pr["vb"]), pr["misct"][:, :IDX_DIM, :], buf_p)):
            lst.append(val)

        ps = _project(xs, wcat, bfrow, tm_s)
        q3 = lambda a: a.reshape(nb, t, a.shape[-1])
        new_t = lambda a: jnp.pad(jnp.swapaxes(q3(a), 1, 2), ((0, 0), (0, 0), (0, PAGE_SIZE - t)))
        ki_new = new_t(ps["misc"][:, :IDX_DIM])
        lf_new = jnp.pad(new_t(ps["misc"][:, MISC_LF:MISC_WI]), ((0, 0), (0, 8 - A_HEADS), (0, 0)))
        bias = _sample_index(page_table, q3(ps["qi"]), q3(ps["misc"]), cidx, l, ki_new, pps_i, topk_s)
        oa_s, ob_s = _sample_attn(page_table, q3(ps["qa"]), q3(ps["qb"]), bias, (cak, cav, clf, cbk, cbv), l,
                                  (new_t(ps["ka"]), new_t(ps["va"]), lf_new, new_t(ps["kb"]), new_t(ps["vb"])),
                                  pps_a)
        x1s, buf_s = _merge(oa_s, ob_s, q3(ps["u"]), q3(ps["gb"]), q3(xs), state_conv[l].astype(F32), cw, wo_bf,
                            g1, b1, t)
        xs = _peer(x1s.reshape(n_s, d), wqt, sk, u_bf, vt_bf, g2, b2, tm_peer_s, ec)
        for lst, val in zip(outs_s, (ps["ka"].reshape(nb, t, A_HEADS, HEAD_DIM), ps["va"].reshape(nb, t, A_HEADS, HEAD_DIM),
                                     ps["misc"][:, MISC_LF:MISC_WI].reshape(nb, t, A_HEADS),
                                     ps["kb"].reshape(nb, t, B_HEADS, HEAD_DIM), ps["vb"].reshape(nb, t, B_HEADS, HEAD_DIM),
                                     ps["misc"][:, :IDX_DIM].reshape(nb, t, IDX_DIM), buf_s)):
            lst.append(val)

    pos_perm = {5: (0, 1, 4, 2, 3), 4: (0, 1, 3, 2)}
    prompt_out = [jnp.stack(v) for v in outs_p]
    prompt_out = [jnp.transpose(a, pos_perm[a.ndim]) for a in prompt_out[:6]] + prompt_out[6:]
    return ((xp.reshape(bp, s, d), xs.reshape(nb, t, d))
            + tuple(prompt_out) + tuple(jnp.stack(v) for v in outs_s))
```

```python
import functools

import jax
import jax.numpy as jnp
from jax import lax
from jax.experimental import pallas as pl
from jax.experimental.pallas import tpu as pltpu

F32 = jnp.float32
BF16 = jnp.bfloat16
I32 = jnp.int32
I16 = jnp.int16
HALF16 = 2 ** 15

D_MODEL = 1024
PAGE_SIZE = 128
HEAD_DIM = 64
A_HEADS = 6
B_HEADS = 6
A_WIDTH = A_HEADS * HEAD_DIM
B_WIDTH = B_HEADS * HEAD_DIM
C_WIDTH = 256
IDX_HEADS = 8
IDX_DIM = 64
TOPK_MAX = 256
CONV_W = 3
PEER_HEADS = 8
PEER_DKEY = 128
PEER_TOPK = 16
LN_EPS = 1e-5
DEPTH = 2
DN_ALPHA = (2 * DEPTH) ** 0.25
IN_WIDTHS = (A_WIDTH, A_WIDTH, A_WIDTH, A_HEADS,
             B_WIDTH, B_WIDTH, B_WIDTH, IDX_HEADS * IDX_DIM, IDX_DIM, IDX_HEADS,
             C_WIDTH, C_WIDTH, C_WIDTH)

LANES = 128
MXU_DEPTH = 256
BF16_ROWS = 16
VMEM_LIMIT = 56 * 1024 * 1024
NEG = -1e30
INT_MIN = -2 ** 31

MISC_LF = IDX_DIM
MISC_WI = IDX_DIM + A_HEADS

O_QA, O_KA, O_VA = 0, 384, 768
O_QB, O_KB, O_VB = 1152, 1536, 1920
O_QI = 2304
O_CIN, O_GB, O_GC = 3328, 3584, 3840
O_MISC = 4096
W_TOTAL = 4224


def _nt(a, b):
    return lax.dot_general(a, b, (((1,), (1,)), ((), ())), preferred_element_type=F32)


def _cparams(sem):
    return pltpu.CompilerParams(dimension_semantics=sem, vmem_limit_bytes=VMEM_LIMIT)


def _sort_key(x):
    bits = pltpu.bitcast(x, I32)
    return bits ^ ((bits >> 31) & 0x7FFFFFFF)


def _layer_norm_rows(z, g, b):
    mu = jnp.mean(z, axis=-1, keepdims=True)
    zc = z - mu
    var = jnp.mean(zc * zc, axis=-1, keepdims=True)
    return zc * lax.rsqrt(var + LN_EPS) * g + b


def _proj_kernel(x_ref, w_ref, bf_ref, qa_ref, ka_ref, va_ref, kab_ref, vab_ref,
                 qb_ref, kb_ref, vb_ref, kbb_ref, vbb_ref, qi_ref, u_ref, gb_ref,
                 misc_ref, miscb_ref, *maybe_t_refs, transposed):
    xb = x_ref[...].astype(BF16)

    def seg(off, width):
        return jnp.dot(xb, w_ref[:, off:off + width], preferred_element_type=F32)

    def put(ref, val):
        if transposed:
            ref[0] = val.T
        else:
            ref[...] = val

    qa_ref[...] = (seg(O_QA, A_WIDTH) * (HEAD_DIM ** -0.5)).astype(BF16)
    ka = seg(O_KA, A_WIDTH)
    put(ka_ref, ka)
    kab_ref[...] = ka.astype(BF16)
    va = seg(O_VA, A_WIDTH)
    put(va_ref, va)
    vab_ref[...] = va.astype(BF16)
    qb_ref[...] = (seg(O_QB, B_WIDTH) * (HEAD_DIM ** -0.5)).astype(BF16)
    kb = seg(O_KB, B_WIDTH)
    put(kb_ref, kb)
    kbb_ref[...] = kb.astype(BF16)
    vb = seg(O_VB, B_WIDTH)
    put(vb_ref, vb)
    vbb_ref[...] = vb.astype(BF16)
    qi_ref[...] = (seg(O_QI, IDX_HEADS * LANES) * (IDX_DIM ** -0.5)).astype(BF16)
    cin = seg(O_CIN, C_WIDTH)
    gb_ref[...] = seg(O_GB, C_WIDTH)
    gc = seg(O_GC, C_WIDTH)
    u_ref[...] = gc * cin
    misc = seg(O_MISC, LANES)
    lane = lax.broadcasted_iota(I32, misc.shape, 1)
    z = misc + bf_ref[...]
    logsig = jnp.minimum(z, 0.0) - jnp.log1p(jnp.exp(-jnp.abs(z)))
    misc = jnp.where((lane >= MISC_LF) & (lane < MISC_WI), logsig, misc)
    misc_ref[...] = misc
    miscb_ref[...] = misc.astype(BF16)
    if transposed:
        misct_ref, kabt_ref, kbbt_ref, miscbt_ref = maybe_t_refs
        misc_t = misc.T
        misct_ref[0] = misc_t
        miscbt_ref[0] = misc_t.astype(BF16)
        kabt_ref[0] = ka.T.astype(BF16)
        kbbt_ref[0] = kb.T.astype(BF16)


def _project(x, wcat, bfrow, tm, seq=None):
    n = x.shape[0]
    transposed = seq is not None
    row = lambda w: pl.BlockSpec((tm, w), lambda i: (i, 0))
    widths = [(A_WIDTH, BF16), (A_WIDTH, F32), (A_WIDTH, F32), (A_WIDTH, BF16), (A_WIDTH, BF16),
              (B_WIDTH, BF16), (B_WIDTH, F32), (B_WIDTH, F32), (B_WIDTH, BF16), (B_WIDTH, BF16),
              (IDX_HEADS * LANES, BF16), (C_WIDTH, F32), (C_WIDTH, F32), (LANES, F32), (LANES, BF16)]
    names = ["qa", "ka", "va", "kab", "vab", "qb", "kb", "vb", "kbb", "vbb", "qi", "u", "gb", "misc", "miscb"]
    out_specs = [row(w) for w, _ in widths]
    out_shape = [jax.ShapeDtypeStruct((n, w), dt) for w, dt in widths]
    if transposed:
        per = seq // tm
        tspec = lambda w: pl.BlockSpec((1, w, tm), lambda i: (i // per, 0, i % per))
        tshape = lambda w, dt=F32: jax.ShapeDtypeStruct((n // seq, w, seq), dt)
        for name in ("ka", "va", "kb", "vb"):
            k = names.index(name)
            out_specs[k], out_shape[k] = tspec(widths[k][0]), tshape(widths[k][0])
        for name, w, dt in (("misct", LANES, F32), ("kabt", A_WIDTH, BF16), ("kbbt", B_WIDTH, BF16),
                            ("miscbt", LANES, BF16)):
            names.append(name)
            out_specs.append(tspec(w))
            out_shape.append(tshape(w, dt))
    outs = pl.pallas_call(
        functools.partial(_proj_kernel, transposed=transposed),
        grid=(n // tm,),
        in_specs=[row(D_MODEL),
                  pl.BlockSpec((D_MODEL, W_TOTAL), lambda i: (0, 0)),
                  pl.BlockSpec((1, LANES), lambda i: (0, 0))],
        out_specs=out_specs,
        out_shape=out_shape,
        compiler_params=_cparams(("parallel",)),
        name="proj",
    )(x, wcat, bfrow)
    return dict(zip(names, outs))


def _wt_kernel(w_ref, o_ref):
    o_ref[...] = w_ref[...].T.astype(BF16)


def _build_wcat(w_in_t, b_f):
    parts, off = [], 0
    for w in IN_WIDTHS:
        parts.append(w_in_t[off:off + w, :])
        off += w
    qa, ka, va, fa, qb, kb, vb, qi, ki, wi, cin, gb, gc = parts
    d = w_in_t.shape[1]
    qi_pad = jnp.pad(qi.reshape(IDX_HEADS, IDX_DIM, d), ((0, 0), (0, LANES - IDX_DIM), (0, 0)))
    qi_pad = qi_pad.reshape(IDX_HEADS * LANES, d)
    misc = jnp.concatenate([ki, fa, wi, jnp.zeros((LANES - MISC_WI - IDX_HEADS, d), w_in_t.dtype)], axis=0)
    wcat_t = jnp.concatenate([qa, ka, va, qb, kb, vb, qi_pad, cin, gb, gc, misc], axis=0)
    wcat = pl.pallas_call(
        _wt_kernel,
        grid=(W_TOTAL // LANES,),
        in_specs=[pl.BlockSpec((LANES, d), lambda i: (i, 0))],
        out_specs=pl.BlockSpec((d, LANES), lambda i: (0, i)),
        out_shape=jax.ShapeDtypeStruct((d, W_TOTAL), BF16),
        compiler_params=_cparams(("parallel",)),
        name="w_in_transpose",
    )(wcat_t)
    bfrow = jnp.zeros((1, LANES), F32).at[0, MISC_LF:MISC_WI].set(b_f.astype(F32))
    return wcat, bfrow


def _cumsum_kernel(m_ref, cum_ref, cumt_ref, carry_ref, *, tc):
    @pl.when(pl.program_id(1) == 0)
    def _():
        carry_ref[...] = jnp.zeros_like(carry_ref)

    v = m_ref[0]
    r = lax.broadcasted_iota(I32, (tc, tc), 0)
    c = lax.broadcasted_iota(I32, (tc, tc), 1)
    tri = (c <= r).astype(F32)
    loc = jnp.dot(tri, v, precision=lax.Precision.HIGHEST, preferred_element_type=F32) + carry_ref[...]
    cum_ref[0] = loc
    cumt_ref[0] = loc.T
    carry_ref[...] = loc[tc - 1:tc, :]


def _cumsum(misc3, tc):
    b, s, _ = misc3.shape
    return pl.pallas_call(
        functools.partial(_cumsum_kernel, tc=tc),
        grid=(b, s // tc),
        in_specs=[pl.BlockSpec((1, tc, LANES), lambda i, j: (i, j, 0))],
        out_specs=[pl.BlockSpec((1, tc, LANES), lambda i, j: (i, j, 0)),
                   pl.BlockSpec((1, LANES, tc), lambda i, j: (i, 0, j))],
        out_shape=[jax.ShapeDtypeStruct((b, s, LANES), F32), jax.ShapeDtypeStruct((b, LANES, s), F32)],
        scratch_shapes=[pltpu.VMEM((1, LANES), F32)],
        compiler_params=_cparams(("parallel", "arbitrary")),
        name="logf_cumsum",
    )(misc3)


def _flash_update(carry, s, vblk, v_transposed=False):
    m, l, acc = carry
    m_new = jnp.maximum(m, jnp.max(s, axis=1, keepdims=True))
    alpha = jnp.exp(m - m_new)
    p = jnp.exp(s - m_new)
    l = alpha * l + jnp.sum(p, axis=1, keepdims=True)
    p16 = p.astype(BF16)
    pv = _nt(p16, vblk) if v_transposed else jnp.dot(p16, vblk, preferred_element_type=F32)
    return m_new, l, alpha * acc + pv


def _flash_init(rows, width):
    return (jnp.full((rows, 1), NEG, F32), jnp.zeros((rows, 1), F32), jnp.zeros((rows, width), F32))


def _split_pair(qp):
    lane = lax.broadcasted_iota(I32, qp.shape, 1)
    zero = jnp.zeros_like(qp)
    return jnp.where(lane < HEAD_DIM, qp, zero), jnp.where(lane >= HEAD_DIM, qp, zero)


def _join_pair(o0, o1):
    lane = lax.broadcasted_iota(I32, o0.shape, 1)
    return jnp.where(lane < HEAD_DIM, o0, o1)


def _pipelined_blocks(nblk, store_qk, step, state):
    store_qk(0, 0)

    def pair(jj, st):
        store_qk(1, 2 * jj + 1)
        st = step(2 * jj, st, 0, False)
        store_qk(0, 2 * jj + 2)
        return step(2 * jj + 1, st, 1, False)

    npair = (nblk - 1) // 2
    st = lax.fori_loop(0, npair, pair, state)

    def one_left(st):
        return step(nblk - 1, st, 0, True)

    def two_left(st):
        store_qk(1, nblk - 1)
        st = step(nblk - 2, st, 0, False)
        return step(nblk - 1, st, 1, True)

    return lax.cond(nblk - 2 * npair == 2, two_left, one_left, st)


def _fox_kernel(q_ref, kt_ref, v_ref, cum_ref, cumt_ref, o_ref, qk_s, *, tq, tk):
    p = pl.program_id(1)
    i = pl.program_id(2)
    qs = _split_pair(q_ref[0])
    lane = lax.broadcasted_iota(I32, (tq, LANES), 1)
    cumblk = cum_ref[0]
    cqs = [jnp.sum(jnp.where(lane == MISC_LF + 2 * p + hh, cumblk, 0.0), axis=1, keepdims=True)
           for hh in range(2)]
    row_g = i * tq + lax.broadcasted_iota(I32, (tq, tk), 0)
    col_l = lax.broadcasted_iota(I32, (tq, tk), 1)

    def store_qk(slot, j):
        off = pl.multiple_of(j * tk, tk)
        ktblk = kt_ref[0, :, pl.ds(off, tk)]
        for hh in range(2):
            qk_s[slot, hh] = jnp.dot(qs[hh], ktblk, preferred_element_type=F32)

    def step(j, state, slot, last):
        off = pl.multiple_of(j * tk, tk)
        vblk = v_ref[0, pl.ds(off, tk), :]
        new = []
        for hh in range(2):
            ck = cumt_ref[0, pl.ds(2 * p + hh, 1), pl.ds(off, tk)]
            s = qk_s[slot, hh] + (cqs[hh] - ck)
            if last:
                s = jnp.where(off + col_l <= row_g, s, NEG)
            new.append(_flash_update(state[hh], s, vblk))
        return tuple(new)

    nblk = (i * tq) // tk + 1
    init = (_flash_init(tq, LANES), _flash_init(tq, LANES))
    (_, l0, acc0), (_, l1, acc1) = _pipelined_blocks(nblk, store_qk, step, init)
    o_ref[0] = _join_pair(acc0 / l0, acc1 / l1).astype(BF16)


def _fox_prompt(qa, kab, vab, cum, cumt, tq, tk):
    b, s, _ = qa.shape
    npairs = A_HEADS // 2
    return pl.pallas_call(
        functools.partial(_fox_kernel, tq=tq, tk=tk),
        grid=(b, npairs, s // tq),
        in_specs=[pl.BlockSpec((1, tq, LANES), lambda bb, p, i: (bb, i, p)),
                  pl.BlockSpec((1, LANES, s), lambda bb, p, i: (bb, p, 0)),
                  pl.BlockSpec((1, s, LANES), lambda bb, p, i: (bb, 0, p)),
                  pl.BlockSpec((1, tq, LANES), lambda bb, p, i: (bb, i, 0)),
                  pl.BlockSpec((1, 8, s), lambda bb, p, i: (bb, MISC_LF // 8, 0))],
        out_specs=pl.BlockSpec((1, tq, LANES), lambda bb, p, i: (bb, i, p)),
        out_shape=jax.ShapeDtypeStruct((b, s, A_WIDTH), BF16),
        scratch_shapes=[pltpu.VMEM((2, 2, tq, tk), F32)],
        compiler_params=_cparams(("parallel", "parallel", "arbitrary")),
        name="fox_prompt",
    )(qa, kab, vab, cum, cumt)


def _kth_threshold(count_ge, rows, k, total):
    def cond(state):
        it, _, cnt = state
        return (it < 32) & (jnp.max(cnt) > k)

    def body(state):
        it, t, cnt = state
        cand = t | lax.shift_left(jnp.int32(1), 31 - it)
        c = count_ge(cand ^ INT_MIN)
        ok = c >= k
        return it + 1, jnp.where(ok, cand, t), jnp.where(ok, c, cnt)

    init = (jnp.int32(0), jnp.zeros((rows, 1), I32), jnp.full((rows, 1), total, I32))
    _, t, cnt = lax.while_loop(cond, body, init)
    return t ^ INT_MIN, cnt


def _tie_cutoff(count_tie_below, rows, need, nbits):
    def body(it, j):
        cand = j | lax.shift_left(jnp.int32(1), nbits - 1 - it)
        cnt = count_tie_below(cand)
        return jnp.where(cnt <= need, cand, j)
    return lax.fori_loop(0, nbits, body, jnp.zeros((rows, 1), I32))


def _dsa_kernel(qi_ref, misc_ref, kit_ref, qb_ref, kbt_ref, vb_ref, o_ref, key_ref, bias_ref, qk_s,
                khi_ref, klo_ref, *, tq, kc, kcc, topk, nbits):
    i = pl.program_id(1)
    nch = (i * tq + tq + kc - 1) // kc
    ncc = (i * tq + tq + kcc - 1) // kcc
    nch1 = ncc * (kcc // kc)
    w8 = misc_ref[0][:, MISC_WI:MISC_WI + IDX_HEADS] * (IDX_HEADS ** -0.5)
    row_g = i * tq + lax.broadcasted_iota(I32, (tq, kc), 0)
    col_l = lax.broadcasted_iota(I32, (tq, kc), 1)
    col_c = lax.broadcasted_iota(I32, (tq, kcc), 1)

    def p1(c, carry):
        off = pl.multiple_of(c * kc, kc)
        kit = kit_ref[0, :, pl.ds(off, kc)]
        acc = jnp.zeros((tq, kc), F32)
        for h in range(IDX_HEADS):
            r = jnp.dot(qi_ref[0, :, h * LANES:(h + 1) * LANES], kit, preferred_element_type=F32)
            acc = acc + w8[:, h:h + 1] * jnp.maximum(r, 0.0)
        acc = jnp.where(acc == 0.0, 0.0, acc)
        sc = jnp.where(off + col_l <= row_g, acc, -jnp.inf)
        key = _sort_key(sc)
        key_ref[:, pl.ds(off, kc)] = key
        khi_ref[:, pl.ds(off, kc)] = (key >> 16).astype(I16)
        return carry

    lax.fori_loop(0, nch1, p1, 0)

    def lane_groups(x):
        tot = x[:, 0:LANES]
        for g in range(1, kcc // LANES):
            tot = tot + x[:, g * LANES:(g + 1) * LANES]
        return tot

    def count(pred):
        def body(c, cnt):
            off = pl.multiple_of(c * kcc, kcc)
            return cnt + lane_groups(pred(key_ref[:, pl.ds(off, kcc)], off).astype(I32))
        cnt = lax.fori_loop(0, ncc, body, jnp.zeros((tq, LANES), I32))
        return jnp.sum(cnt, axis=1, keepdims=True)

    def count16(ref16, pred):
        def body(c, cnt):
            off = pl.multiple_of(c * kcc, kcc)
            hit = jnp.where(pred(ref16[:, pl.ds(off, kcc)]), jnp.int16(1), jnp.int16(0))
            return cnt + lane_groups(hit)
        cnt = lax.fori_loop(0, ncc, body, jnp.zeros((tq, LANES), I16))
        return jnp.sum(cnt.astype(I32), axis=1, keepdims=True)

    def bit_step(t, cnt, it, count_ge16, base):
        cand = t | lax.shift_left(jnp.int32(1), 15 - it)
        c = base + count_ge16((cand - HALF16).astype(I16))
        ok = c >= topk
        return jnp.where(ok, cand, t), jnp.where(ok, c, cnt)

    zero = jnp.zeros((tq, 1), I32)
    t_hi, n_hi_ge = lax.fori_loop(
        0, 16,
        lambda it, st: bit_step(st[0], st[1], it, lambda th: count16(khi_ref, lambda kk: kk >= th), 0),
        (zero, jnp.full((tq, 1), ncc * kcc, I32)))
    th_s = t_hi - HALF16

    def low_half():
        th16 = th_s.astype(I16)
        n_hi_gt = count16(khi_ref, lambda kk: kk > th16)

        def build(c, carry):
            off = pl.multiple_of(c * kcc, kcc)
            lo = ((key_ref[:, pl.ds(off, kcc)] & 0xFFFF) - HALF16).astype(I16)
            klo_ref[:, pl.ds(off, kcc)] = jnp.where(khi_ref[:, pl.ds(off, kcc)] == th16, lo, jnp.int16(-HALF16))
            return carry

        lax.fori_loop(0, ncc, build, 0)

        def cond(st):
            it, _, cnt = st
            return (it < 16) & (jnp.max(cnt) > topk)

        def body(st):
            it, t, cnt = st
            t, cnt = bit_step(t, cnt, it, lambda th: count16(klo_ref, lambda kk: kk >= th), n_hi_gt)
            return it + 1, t, cnt

        _, t_lo, cnt = lax.while_loop(cond, body, (jnp.int32(0), zero, n_hi_ge))
        return t_lo, cnt

    t_lo, n_ge = lax.cond(jnp.max(n_hi_ge) > topk, low_half, lambda: (zero, n_hi_ge))
    thr = lax.shift_left(th_s, 16) | t_lo
    big = jnp.full((tq, 1), 2 ** nbits - 1, I32)

    def tie_cutoff():
        need = topk - count(lambda kk, off: kk > thr)
        return _tie_cutoff(lambda j: count(lambda kk, off: (kk == thr) & (off + col_c < j)), tq, need, nbits)

    jstar = lax.cond(jnp.max(n_ge) > topk, tie_cutoff, lambda: big)

    def p2(c, carry):
        off = pl.multiple_of(c * kc, kc)
        kk = key_ref[:, pl.ds(off, kc)]
        colg = off + col_l
        sel = (kk > thr) | ((kk == thr) & (colg < jstar))
        bias_ref[:, pl.ds(off, kc)] = jnp.where(sel & (colg <= row_g), 0.0, NEG)
        return carry

    lax.fori_loop(0, nch, p2, 0)

    for pr in range(B_HEADS // 2):
        qs = _split_pair(qb_ref[0, :, pr * LANES:(pr + 1) * LANES])

        def store_qk(slot, c):
            off = pl.multiple_of(c * kc, kc)
            ktblk = kbt_ref[0, pr * LANES:(pr + 1) * LANES, pl.ds(off, kc)]
            for hh in range(2):
                qk_s[slot, hh] = jnp.dot(qs[hh], ktblk, preferred_element_type=F32)

        def step(c, state, slot, last):
            off = pl.multiple_of(c * kc, kc)
            vblk = vb_ref[0, pl.ds(off, kc), pr * LANES:(pr + 1) * LANES]
            bias = bias_ref[:, pl.ds(off, kc)]
            return tuple(_flash_update(state[hh], qk_s[slot, hh] + bias, vblk) for hh in range(2))

        init = (_flash_init(tq, LANES), _flash_init(tq, LANES))
        (_, l0, acc0), (_, l1, acc1) = _pipelined_blocks(nch, store_qk, step, init)
        o_ref[0, :, pr * LANES:(pr + 1) * LANES] = _join_pair(acc0 / l0, acc1 / l1).astype(BF16)


def _dsa_prompt(qi, misc, miscb, qb, kbb, vbb, tq, kc, topk):
    b, s, _ = qb.shape
    nbits = max(1, (s - 1).bit_length()) + 1
    kcc = 2 * kc if s % (2 * kc) == 0 else kc
    qspec = lambda w: pl.BlockSpec((1, tq, w), lambda bb, i: (bb, i, 0))
    full = lambda w: pl.BlockSpec((1, s, w), lambda bb, i: (bb, 0, 0))
    full_t = lambda w: pl.BlockSpec((1, w, s), lambda bb, i: (bb, 0, 0))
    return pl.pallas_call(
        functools.partial(_dsa_kernel, tq=tq, kc=kc, kcc=kcc, topk=topk, nbits=nbits),
        grid=(b, s // tq),
        in_specs=[qspec(IDX_HEADS * LANES), qspec(LANES), full_t(LANES), qspec(B_WIDTH), full_t(B_WIDTH),
                  full(B_WIDTH)],
        out_specs=qspec(B_WIDTH),
        out_shape=jax.ShapeDtypeStruct((b, s, B_WIDTH), BF16),
        scratch_shapes=[pltpu.VMEM((tq, s), I32), pltpu.VMEM((tq, s), F32), pltpu.VMEM((2, 2, tq, kc), F32),
                        pltpu.VMEM((tq, s), I16), pltpu.VMEM((tq, s), I16)],
        compiler_params=_cparams(("parallel", "arbitrary")),
        name="dsa_prompt",
    )(qi, misc, miscb, qb, kbb, vbb)


def _sidx_kernel(pt_ref, qi_ref, misc_ref, *rest, pps, nps, past, topk, nbits):
    page_refs = rest[:pps]
    knew_ref, bias_ref, key_ref, qst_ref, wst_ref = rest[pps:]
    j = pl.program_id(1)
    t = qi_ref.shape[1]
    lp = key_ref.shape[1]

    @pl.when(j == 0)
    def _():
        qf = qi_ref[0].astype(F32)
        qst_ref[...] = jnp.concatenate([qf[:, h * LANES:h * LANES + IDX_DIM] for h in range(IDX_HEADS)],
                                       axis=0).astype(BF16)
        w8 = misc_ref[0][:, MISC_WI:MISC_WI + IDX_HEADS] * (IDX_HEADS ** -0.5)
        wst_ref[...] = jnp.concatenate([w8[:, h:h + 1] for h in range(IDX_HEADS)], axis=0)

    def score(kt):
        r = jnp.dot(qst_ref[...], kt.astype(BF16), preferred_element_type=F32)
        r = wst_ref[...] * jnp.maximum(r, 0.0)
        acc = r[0:t, :]
        for h in range(1, IDX_HEADS):
            acc = acc + r[h * t:(h + 1) * t, :]
        return jnp.where(acc == 0.0, 0.0, acc)

    @pl.when(j < nps)
    def _():
        kt = jnp.concatenate([page_refs[k][0, 0] for k in range(pps)], axis=1)
        off = pl.multiple_of(j * (pps * PAGE_SIZE), pps * PAGE_SIZE)
        key_ref[:, pl.ds(off, pps * PAGE_SIZE)] = _sort_key(score(kt))

    @pl.when(j == nps)
    def _():
        rowi = lax.broadcasted_iota(I32, (t, PAGE_SIZE), 0)
        coli = lax.broadcasted_iota(I32, (t, PAGE_SIZE), 1)
        sc = jnp.where(coli <= rowi, score(knew_ref[0]), -jnp.inf)
        key_ref[:, past:past + PAGE_SIZE] = _sort_key(sc)

        keys = key_ref[...]
        colg = lax.broadcasted_iota(I32, (t, lp), 1)
        count = lambda pred: jnp.sum(pred.astype(I32), axis=1, keepdims=True)
        thr, n_ge = _kth_threshold(lambda th: count(keys >= th), t, topk, lp)

        def tie_cutoff():
            need = topk - count(keys > thr)
            return _tie_cutoff(lambda jj: count((keys == thr) & (colg < jj)), t, need, nbits)

        jstar = lax.cond(jnp.max(n_ge) > topk, tie_cutoff, lambda: jnp.full((t, 1), 2 ** nbits - 1, I32))
        sel = (keys > thr) | ((keys == thr) & (colg < jstar))
        rowg = past + lax.broadcasted_iota(I32, (t, lp), 0)
        bias_ref[0] = jnp.where(sel & (colg <= rowg), 0.0, NEG)


def _sample_index(page_table, qi3, misc3, cache_idx_t, layer, knew_t, pps, topk):
    nb, t, _ = qi3.shape
    npg = page_table.shape[1]
    nps = npg // pps
    past = npg * PAGE_SIZE
    lp = past + PAGE_SIZE
    nbits = lp.bit_length() + 1

    def page_map(k):
        return lambda b, j, pt: (layer, pt[b * npg + jnp.minimum(j, nps - 1) * pps + k], 0, 0)

    bmap = lambda b, j, pt: (b, 0, 0)
    grid_spec = pltpu.PrefetchScalarGridSpec(
        num_scalar_prefetch=1,
        grid=(nb, nps + 1),
        in_specs=[pl.BlockSpec((1, t, IDX_HEADS * LANES), bmap),
                  pl.BlockSpec((1, t, LANES), bmap)]
                 + [pl.BlockSpec((1, 1, IDX_DIM, PAGE_SIZE), page_map(k)) for k in range(pps)]
                 + [pl.BlockSpec((1, IDX_DIM, PAGE_SIZE), bmap)],
        out_specs=pl.BlockSpec((1, t, lp), bmap),
        scratch_shapes=[pltpu.VMEM((t, lp), I32), pltpu.VMEM((IDX_HEADS * t, IDX_DIM), BF16),
                        pltpu.VMEM((IDX_HEADS * t, 1), F32)],
    )
    return pl.pallas_call(
        functools.partial(_sidx_kernel, pps=pps, nps=nps, past=past, topk=topk, nbits=nbits),
        grid_spec=grid_spec,
        out_shape=jax.ShapeDtypeStruct((nb, t, lp), F32),
        compiler_params=_cparams(("parallel", "arbitrary")),
        name="sample_index",
    )(page_table.reshape(-1), qi3, misc3, *([cache_idx_t] * pps), knew_t)


def _sattn_kernel(pt_ref, qa_ref, qb_ref, bias_ref, *rest, pps, nps):
    grp = lambda g: rest[g * pps:(g + 1) * pps]
    ak_refs, av_refs, lf_refs, bk_refs, bv_refs = (grp(g) for g in range(5))
    (akn_ref, avn_ref, lfn_ref, bkn_ref, bvn_ref, oa_ref, ob_ref,
     qbd_a, qbd_b, ma, la, acca, mb, lb, accb, carry_ref) = rest[5 * pps:]
    j = pl.program_id(1)
    t = qa_ref.shape[1]
    rows = A_HEADS * t
    lane_q = lax.broadcasted_iota(I32, (t, A_WIDTH), 1)

    def block_diag(q):
        qf = q.astype(F32)
        parts = [jnp.where((lane_q >= h * HEAD_DIM) & (lane_q < (h + 1) * HEAD_DIM), qf, 0.0)
                 for h in range(A_HEADS)]
        return jnp.concatenate(parts, axis=0).astype(BF16)

    @pl.when(j == 0)
    def _():
        qbd_a[...] = block_diag(qa_ref[0])
        qbd_b[...] = block_diag(qb_ref[0])
        for r in (ma, mb):
            r[...] = jnp.full(r.shape, NEG, F32)
        for r in (la, lb, acca, accb, carry_ref):
            r[...] = jnp.zeros(r.shape, F32)

    r_i = lax.broadcasted_iota(I32, (PAGE_SIZE, PAGE_SIZE), 0)
    c_i = lax.broadcasted_iota(I32, (PAGE_SIZE, PAGE_SIZE), 1)
    triu = (r_i <= c_i).astype(F32)

    def cat(refs, dtype):
        pages = [r[0, 0] if len(r.shape) == 4 else r[0] for r in refs]
        x = pages[0] if len(pages) == 1 else jnp.concatenate(pages, axis=1)
        return x.astype(dtype)

    def update(m_ref, l_ref, acc_ref, s, vt):
        m, l, acc = _flash_update((m_ref[...], l_ref[...], acc_ref[...]), s, vt, v_transposed=True)
        m_ref[...] = m
        l_ref[...] = l
        acc_ref[...] = acc

    def process(ak, av, lf, bk, bv, bias8, mask):
        run = carry_ref[...]
        cls = []
        for r in lf:
            page = r[0, 0] if len(r.shape) == 4 else r[0]
            cl = jnp.dot(page, triu, precision=lax.Precision.HIGHEST, preferred_element_type=F32) + run
            run = jnp.broadcast_to(cl[:, PAGE_SIZE - 1:PAGE_SIZE], cl.shape)
            cls.append(cl)
        carry_ref[...] = run
        ck = cls[0] if len(cls) == 1 else jnp.concatenate(cls, axis=1)
        width = ck.shape[1]
        ck_rows = jnp.concatenate([jnp.broadcast_to(ck[h:h + 1, :], (t, width)) for h in range(A_HEADS)], axis=0)
        sa = jnp.dot(qbd_a[...], cat(ak, BF16), preferred_element_type=F32) - ck_rows
        if mask is not None:
            sa = jnp.where(mask, sa, NEG)
        update(ma, la, acca, sa, cat(av, BF16))
        sb = jnp.dot(qbd_b[...], cat(bk, BF16), preferred_element_type=F32)
        sb = sb + jnp.concatenate([bias8] * B_HEADS, axis=0)
        update(mb, lb, accb, sb, cat(bv, BF16))

    @pl.when(j < nps)
    def _():
        off = pl.multiple_of(j * (pps * PAGE_SIZE), pps * PAGE_SIZE)
        process(ak_refs, av_refs, lf_refs, bk_refs, bv_refs, bias_ref[0, :, pl.ds(off, pps * PAGE_SIZE)], None)

    @pl.when(j == nps)
    def _():
        past = nps * pps * PAGE_SIZE
        rowi = lax.broadcasted_iota(I32, (rows, PAGE_SIZE), 0)
        coli = lax.broadcasted_iota(I32, (rows, PAGE_SIZE), 1)
        mask = coli <= (rowi & (t - 1))
        process([akn_ref], [avn_ref], [lfn_ref], [bkn_ref], [bvn_ref],
                bias_ref[0, :, past:past + PAGE_SIZE], mask)

        def gather_heads(acc_ref, l_ref):
            o = jnp.zeros((t, A_WIDTH), F32)
            for h in range(A_HEADS):
                blk = acc_ref[h * t:(h + 1) * t, :] / l_ref[h * t:(h + 1) * t, :]
                o = jnp.where((lane_q >= h * HEAD_DIM) & (lane_q < (h + 1) * HEAD_DIM), blk, o)
            return o.astype(BF16)

        oa_ref[0] = gather_heads(acca, la)
        ob_ref[0] = gather_heads(accb, lb)


def _sample_attn(page_table, qa3, qb3, bias, caches, layer, news, pps):
    nb, t, _ = qa3.shape
    assert t & (t - 1) == 0
    npg = page_table.shape[1]
    nps = npg // pps
    lp = bias.shape[2]
    rows = A_HEADS * t

    def page_map(k):
        return lambda b, j, pt: (layer, pt[b * npg + jnp.minimum(j, nps - 1) * pps + k], 0, 0)

    bmap = lambda b, j, pt: (b, 0, 0)
    cache_specs, cache_args = [], []
    for c in caches:
        blk = (1, 1) + c.shape[2:]
        for k in range(pps):
            cache_specs.append(pl.BlockSpec(blk, page_map(k)))
            cache_args.append(c)
    new_specs = [pl.BlockSpec((1,) + a.shape[1:], bmap) for a in news]
    grid_spec = pltpu.PrefetchScalarGridSpec(
        num_scalar_prefetch=1,
        grid=(nb, nps + 1),
        in_specs=[pl.BlockSpec((1, t, A_WIDTH), bmap), pl.BlockSpec((1, t, B_WIDTH), bmap),
                  pl.BlockSpec((1, t, lp), bmap)] + cache_specs + new_specs,
        out_specs=[pl.BlockSpec((1, t, A_WIDTH), bmap), pl.BlockSpec((1, t, B_WIDTH), bmap)],
        scratch_shapes=[pltpu.VMEM((rows, A_WIDTH), BF16), pltpu.VMEM((rows, B_WIDTH), BF16),
                        pltpu.VMEM((rows, 1), F32), pltpu.VMEM((rows, 1), F32), pltpu.VMEM((rows, A_WIDTH), F32),
                        pltpu.VMEM((rows, 1), F32), pltpu.VMEM((rows, 1), F32), pltpu.VMEM((rows, B_WIDTH), F32),
                        pltpu.VMEM((8, PAGE_SIZE), F32)],
    )
    return pl.pallas_call(
        functools.partial(_sattn_kernel, pps=pps, nps=nps),
        grid_spec=grid_spec,
        out_shape=[jax.ShapeDtypeStruct((nb, t, A_WIDTH), BF16), jax.ShapeDtypeStruct((nb, t, B_WIDTH), BF16)],
        compiler_params=_cparams(("parallel", "arbitrary")),
        name="sample_attn",
    )(page_table.reshape(-1), qa3, qb3, bias, *cache_args, *news)


def _merge_kernel(oa_ref, ob_ref, u_ref, gb_ref, x_ref, buf_ref, cw_ref, wo_ref, g_ref, b_ref,
                  y_ref, nbuf_ref, carry_ref, *, tm):
    j = pl.program_id(1)

    @pl.when(j == 0)
    def _():
        carry_ref[...] = buf_ref[0]

    u = u_ref[0]
    rowi = lax.broadcasted_iota(I32, u.shape, 0)
    c0 = carry_ref[0:1, :]
    c1 = carry_ref[1:2, :]
    u1 = jnp.where(rowi == 0, c1, pltpu.roll(u, 1, 0))
    u2 = jnp.where(rowi == 0, c0, jnp.where(rowi == 1, c1, pltpu.roll(u, 2, 0)))
    yc = cw_ref[0:1, :] * u2 + cw_ref[1:2, :] * u1 + cw_ref[2:3, :] * u
    oc = (gb_ref[0] * yc).astype(BF16)
    new_carry = u[tm - 2:tm, :]
    carry_ref[...] = new_carry
    nbuf_ref[0] = new_carry
    mix = jnp.dot(oa_ref[0], wo_ref[0:A_WIDTH, :], preferred_element_type=F32)
    mix = mix + jnp.dot(ob_ref[0], wo_ref[A_WIDTH:A_WIDTH + B_WIDTH, :], preferred_element_type=F32)
    mix = mix + jnp.dot(oc, wo_ref[A_WIDTH + B_WIDTH:, :], preferred_element_type=F32)
    y_ref[0] = _layer_norm_rows(DN_ALPHA * x_ref[0] + mix, g_ref[...], b_ref[...])


def _merge(oa, ob, u, gb, x, buf, conv_w, wo_bf, g, b, tm):
    nseq, t, _ = x.shape
    blk = lambda w: pl.BlockSpec((1, tm, w), lambda s, j: (s, j, 0))
    const = lambda r, w: pl.BlockSpec((r, w), lambda s, j: (0, 0))
    return pl.pallas_call(
        functools.partial(_merge_kernel, tm=tm),
        grid=(nseq, t // tm),
        in_specs=[blk(A_WIDTH), blk(B_WIDTH), blk(C_WIDTH), blk(C_WIDTH), blk(D_MODEL),
                  pl.BlockSpec((1, CONV_W - 1, C_WIDTH), lambda s, j: (s, 0, 0)),
                  const(CONV_W, C_WIDTH), const(D_MODEL, D_MODEL), const(1, D_MODEL), const(1, D_MODEL)],
        out_specs=[blk(D_MODEL), pl.BlockSpec((1, CONV_W - 1, C_WIDTH), lambda s, j: (s, 0, 0))],
        out_shape=[jax.ShapeDtypeStruct((nseq, t, D_MODEL), F32),
                   jax.ShapeDtypeStruct((nseq, CONV_W - 1, C_WIDTH), F32)],
        scratch_shapes=[pltpu.VMEM((CONV_W - 1, C_WIDTH), F32)],
        compiler_params=_cparams(("parallel", "arbitrary")),
        name="merge_ln",
    )(oa, ob, u, gb, x, buf, conv_w, wo_bf, g, b)


def _top16_rows(s_list):
    nk, tm = s_list[0].shape

    def extract(s, first_occurrence):
        idx = lax.broadcasted_iota(I32, (nk, tm), 0).astype(F32)
        work = s
        rank = jnp.full((nk, tm), 99.0, F32)
        vals = []
        for r in range(PEER_TOPK):
            m = jnp.max(work, axis=0, keepdims=True)
            hit = work == m
            if first_occurrence:
                hit = idx == jnp.min(jnp.where(hit, idx, 1e9), axis=0, keepdims=True)
            rank = jnp.where(hit, float(r), rank)
            work = jnp.where(hit, -jnp.inf, work)
            vals.append(m)
        return jnp.concatenate(vals, axis=0), rank

    fast = [extract(s, False) for s in s_list]
    bad = jnp.zeros((1, tm), F32)
    for _, rank in fast:
        nsel = jnp.sum(jnp.where(rank < 99.0, 1.0, 0.0), axis=0, keepdims=True)
        bad = jnp.maximum(bad, jnp.where(nsel != float(PEER_TOPK), 1.0, 0.0))
    flat = lambda pairs: tuple(x for pair in pairs for x in pair)
    out = lax.cond(jnp.max(bad) > 0.0, lambda: flat([extract(s, True) for s in s_list]), lambda: flat(fast))
    return [(out[2 * i], out[2 * i + 1]) for i in range(len(s_list))]


def _peer_kernel(x_ref, wqt_ref, sk_ref, u_ref, vt_ref, g_ref, b_ref, o_ref,
                 xt_s, qt_s, outt_s, a_s, n1_s, bb_s, r2_s, *, tm, ec, nk):
    e = pl.program_id(1)
    ne = pl.num_programs(1)
    k = PEER_TOPK
    half = PEER_DKEY // 2

    @pl.when(e == 0)
    def _():
        xt = x_ref[...].T.astype(BF16)
        xt_s[...] = xt
        qt_s[...] = jnp.dot(wqt_ref[...], xt, preferred_element_type=F32).astype(BF16)
        rho = lax.broadcasted_iota(I32, (80, tm), 0)
        mid = rho - 16
        r1 = jnp.where(rho < 16, rho, jnp.where(rho < 72, mid & 7, 0))
        r2 = jnp.where(rho < 16, 0, jnp.where(rho < 72, (mid >> 3) + 1, rho - 64))
        pos = (r1 * k + r2).astype(F32)
        valid = (r1 + 1) * (r2 + 1) <= k
        row8 = lax.broadcasted_iota(I32, (8, tm), 0)
        def head(h, carry):
            q1 = qt_s[pl.ds(pl.multiple_of(2 * h * half, half), half), :]
            q2 = qt_s[pl.ds(pl.multiple_of((2 * h + 1) * half, half), half), :]
            s1 = jnp.dot(sk_ref[h, 0], q1, preferred_element_type=F32)
            s2 = jnp.dot(sk_ref[h, 1], q2, preferred_element_type=F32)
            (v1, rank1), (v2, rank2) = _top16_rows([s1, s2])
            slabs = [v1 + v2[0:1, :]]
            for j in range(1, 8):
                slabs.append(v1[0:8, :] + v2[j:j + 1, :])
            slabs.append(v1[0:1, :] + v2[8:16, :])
            cand = jnp.where(valid, jnp.concatenate(slabs, axis=0), -jnp.inf)
            cmax = v1[0:1, :] + v2[0:1, :]
            sel = jnp.zeros((80, tm), F32)
            z = jnp.zeros((1, tm), F32)
            for r in range(k):
                m = jnp.max(cand, axis=0, keepdims=True)
                first = jnp.min(jnp.where(cand == m, pos, 1e9), axis=0, keepdims=True)
                hit = pos == first
                sel = jnp.where(hit, 1.0, sel)
                cand = jnp.where(hit, -jnp.inf, cand)
                z = z + jnp.exp(m - cmax)
            top8 = sel[0:8, :]
            for j in range(1, 8):
                top8 = top8 + sel[8 + 8 * j:16 + 8 * j, :]
            extra = jnp.sum(sel[72:80, :], axis=0, keepdims=True)
            top8 = top8 + jnp.where(row8 == 0, extra, 0.0)
            ncount = jnp.concatenate([top8, sel[8:16, :]], axis=0)
            n1 = jnp.zeros((nk, tm), F32)
            for r in range(k):
                n1 = jnp.where(rank1 == float(r), ncount[r:r + 1, :], n1)
            a_s[h] = jnp.exp(s1 - v1[0:1, :])
            n1_s[h] = n1
            bb_s[h] = (jnp.exp(s2 - v2[0:1, :]) / z).astype(BF16)
            r2_s[h] = rank2.astype(BF16)
            return carry

        lax.fori_loop(0, PEER_HEADS, head, 0)
        outt_s[...] = jnp.zeros_like(outt_s)

    per = ec // nk
    gps = max(1, min(per, MXU_DEPTH // nk))
    sl = gps * nk
    xt = xt_s[...]
    acc = None
    nsl = ec // sl

    def rows16(row):
        if nk % BF16_ROWS:
            return jnp.broadcast_to(row, (nk, tm)).astype(BF16)
        tile = jnp.broadcast_to(row, (BF16_ROWS, tm)).astype(BF16)
        return jnp.concatenate([tile] * (nk // BF16_ROWS), axis=0) if nk > BF16_ROWS else tile

    hidden = lambda sb: jnp.dot(u_ref[sb * sl:(sb + 1) * sl, :], xt, preferred_element_type=F32)
    ht_next = hidden(0)
    for sb in range(nsl):
        ht = ht_next
        if sb + 1 < nsl:
            ht_next = hidden(sb + 1)
        act = (0.5 * ht * (1.0 + lax.erf(ht * (0.5 ** 0.5)))).astype(BF16)
        coefs = []
        for g in range(gps):
            i1 = e * per + sb * gps + g
            gate = jnp.zeros((nk, tm), BF16)
            for h in range(PEER_HEADS):
                arow = rows16(a_s[h, pl.ds(i1, 1), :])
                nrow = rows16(n1_s[h, pl.ds(i1, 1), :])
                gate = gate + jnp.where(r2_s[h] < nrow, bb_s[h], jnp.zeros((), BF16)) * arow
            coefs.append(gate * act[g * nk:(g + 1) * nk, :])
        coef = jnp.concatenate(coefs, axis=0) if gps > 1 else coefs[0]
        part = jnp.dot(vt_ref[:, sb * sl:(sb + 1) * sl], coef, preferred_element_type=F32)
        acc = part if acc is None else acc + part
    outt_s[...] += acc

    @pl.when(e == ne - 1)
    def _():
        y = outt_s[...].T
        o_ref[...] = _layer_norm_rows(DN_ALPHA * x_ref[...] + y, g_ref[...], b_ref[...])


def _peer(x, wqt, sk, u_bf, vt_bf, g, b, tm, ec):
    n = x.shape[0]
    nk = sk.shape[2]
    ne = (nk * nk) // ec
    return pl.pallas_call(
        functools.partial(_peer_kernel, tm=tm, ec=ec, nk=nk),
        grid=(n // tm, ne),
        in_specs=[pl.BlockSpec((tm, D_MODEL), lambda i, e: (i, 0)),
                  pl.BlockSpec((D_MODEL, D_MODEL), lambda i, e: (0, 0)),
                  pl.BlockSpec(sk.shape, lambda i, e: (0, 0, 0, 0)),
                  pl.BlockSpec((ec, D_MODEL), lambda i, e: (e, 0)),
                  pl.BlockSpec((D_MODEL, ec), lambda i, e: (0, e)),
                  pl.BlockSpec((1, D_MODEL), lambda i, e: (0, 0)),
                  pl.BlockSpec((1, D_MODEL), lambda i, e: (0, 0))],
        out_specs=pl.BlockSpec((tm, D_MODEL), lambda i, e: (i, 0)),
        out_shape=jax.ShapeDtypeStruct((n, D_MODEL), F32),
        scratch_shapes=[pltpu.VMEM((D_MODEL, tm), BF16), pltpu.VMEM((PEER_HEADS * PEER_DKEY, tm), BF16),
                        pltpu.VMEM((D_MODEL, tm), F32)]
                       + [pltpu.VMEM((PEER_HEADS, nk, tm), F32)] * 2
                       + [pltpu.VMEM((PEER_HEADS, nk, tm), BF16)] * 2,
        compiler_params=_cparams(("parallel", "arbitrary")),
        name="peer_ln",
    )(x, wqt, sk, u_bf, vt_bf, g, b)


def _pick(n, pref):
    t = min(n, pref)
    while n % t:
        t //= 2
    return t


def kernel(x_prompt, x_sample, cache_a_k, cache_a_v, cache_a_logf, cache_b_k, cache_b_v, cache_b_idx_k, state_conv, page_table, w_in, b_fgate, conv_w, w_o, ln1_g, ln1_b, peer_wq, peer_subkeys, peer_u, peer_v, ln2_g, ln2_b):
    bp, s, d = x_prompt.shape
    nb, t, _ = x_sample.shape
    depth = w_in.shape[0]
    npool = cache_a_k.shape[1]
    npg = page_table.shape[1]
    past = npg * PAGE_SIZE
    nk = peer_subkeys.shape[3]
    topk_p = min(TOPK_MAX, s // 4)
    topk_s = min(TOPK_MAX, (past + t) // 4)

    n_p, n_s = bp * s, nb * t
    tm_p = _pick(n_p, 512)
    tm_s = _pick(n_s, 256)
    tq_fox = _pick(s, 128)
    tk_fox = _pick(s, 512)
    tq_dsa = _pick(s, 128)
    kc_dsa = _pick(s, 512)
    tc = _pick(s, 256)
    tm_merge = _pick(s, 512)
    tm_peer_p = _pick(n_p, 512)
    tm_peer_s = _pick(n_s, 256)
    ec = _pick(nk * nk, max(2048, nk))
    pps_i = _pick(npg, 16)
    pps_a = _pick(npg, 16)

    page_t = lambda c: jnp.transpose(c, (0, 1, 3, 4, 2)).reshape(depth, npool, -1, PAGE_SIZE)
    cak, cav, cbk, cbv = page_t(cache_a_k), page_t(cache_a_v), page_t(cache_b_k), page_t(cache_b_v)
    cidx = jnp.swapaxes(cache_b_idx_k, 2, 3)
    clf = jnp.pad(jnp.swapaxes(cache_a_logf, 2, 3), ((0, 0), (0, 0), (0, 8 - A_HEADS), (0, 0)))

    w_in_t = jnp.transpose(w_in, (2, 0, 1))
    xp = x_prompt.reshape(n_p, d)
    xs = x_sample.reshape(n_s, d)
    outs_p = [[] for _ in range(7)]
    outs_s = [[] for _ in range(7)]
    zero_buf = jnp.zeros((bp, CONV_W - 1, C_WIDTH), F32)
    row2 = lambda v: v.reshape(1, -1).astype(F32)

    for l in range(depth):
        wcat, bfrow = _build_wcat(w_in_t[:, l, :], b_fgate[l])
        wo_bf = w_o[l].astype(BF16)
        wqt = peer_wq[l].T.astype(BF16)
        sk = peer_subkeys[l].astype(BF16)
        u_bf = peer_u[l].astype(BF16)
        vt_bf = peer_v[l].T.astype(BF16)
        g1, b1, g2, b2 = row2(ln1_g[l]), row2(ln1_b[l]), row2(ln2_g[l]), row2(ln2_b[l])
        cw = conv_w[l].astype(F32)

        pr = _project(xp, wcat, bfrow, tm_p, seq=s)
        r3 = lambda a: a.reshape(bp, s, a.shape[-1])
        cum, cumt = _cumsum(r3(pr["misc"]), tc)
        oa = _fox_prompt(r3(pr["qa"]), pr["kabt"], r3(pr["vab"]), cum, cumt, tq_fox, tk_fox)
        ob = _dsa_prompt(r3(pr["qi"]), r3(pr["misc"]), pr["miscbt"], r3(pr["qb"]), pr["kbbt"],
                         r3(pr["vbb"]), tq_dsa, kc_dsa, topk_p)
        x1, buf_p = _merge(oa, ob, r3(pr["u"]), r3(pr["gb"]), r3(xp), zero_buf, cw, wo_bf, g1, b1, tm_merge)
        xp = _peer(x1.reshape(n_p, d), wqt, sk, u_bf, vt_bf, g2, b2, tm_peer_p, ec)
        heads_t = lambda a: a.reshape(bp, A_HEADS, HEAD_DIM, s)
        for lst, val in zip(outs_p, (heads_t(pr["ka"]), heads_t(pr["va"]), pr["misct"][:, MISC_LF:MISC_WI, :],
                                     heads_t(pr["kb"]), heads_t(pr["vb"]), pr["misct"][:, :IDX_DIM, :], buf_p)):
            lst.append(val)

        ps = _project(xs, wcat, bfrow, tm_s)
        q3 = lambda a: a.reshape(nb, t, a.shape[-1])
        new_t = lambda a: jnp.pad(jnp.swapaxes(q3(a), 1, 2), ((0, 0), (0, 0), (0, PAGE_SIZE - t)))
        ki_new = new_t(ps["misc"][:, :IDX_DIM])
        lf_new = jnp.pad(new_t(ps["misc"][:, MISC_LF:MISC_WI]), ((0, 0), (0, 8 - A_HEADS), (0, 0)))
        bias = _sample_index(page_table, q3(ps["qi"]), q3(ps["misc"]), cidx, l, ki_new, pps_i, topk_s)
        oa_s, ob_s = _sample_attn(page_table, q3(ps["qa"]), q3(ps["qb"]), bias, (cak, cav, clf, cbk, cbv), l,
                                  (new_t(ps["ka"]), new_t(ps["va"]), lf_new, new_t(ps["kb"]), new_t(ps["vb"])),
                                  pps_a)
        x1s, buf_s = _merge(oa_s, ob_s, q3(ps["u"]), q3(ps["gb"]), q3(xs), state_conv[l].astype(F32), cw, wo_bf,
                            g1, b1, t)
        xs = _peer(x1s.reshape(n_s, d), wqt, sk, u_bf, vt_bf, g2, b2, tm_peer_s, ec)
        for lst, val in zip(outs_s, (ps["ka"].reshape(nb, t, A_HEADS, HEAD_DIM), ps["va"].reshape(nb, t, A_HEADS, HEAD_DIM),
                                     ps["misc"][:, MISC_LF:MISC_WI].reshape(nb, t, A_HEADS),
                                     ps["kb"].reshape(nb, t, B_HEADS, HEAD_DIM), ps["vb"].reshape(nb, t, B_HEADS, HEAD_DIM),
                                     ps["misc"][:, :IDX_DIM].reshape(nb, t, IDX_DIM), buf_s)):
            lst.append(val)

    pos_perm = {5: (0, 1, 4, 2, 3), 4: (0, 1, 3, 2)}
    prompt_out = [jnp.stack(v) for v in outs_p]
    prompt_out = [jnp.transpose(a, pos_perm[a.ndim]) for a in prompt_out[:6]] + prompt_out[6:]
    return ((xp.reshape(bp, s, d), xs.reshape(nb, t, d))
            + tuple(prompt_out) + tuple(jnp.stack(v) for v in outs_s))
```

```python
import functools

import jax
import jax.numpy as jnp
from jax import lax
from jax.experimental import pallas as pl
from jax.experimental.pallas import tpu as pltpu

F32 = jnp.float32
BF16 = jnp.bfloat16
I32 = jnp.int32
I16 = jnp.int16
HALF16 = 2 ** 15

D_MODEL = 1024
PAGE_SIZE = 128
HEAD_DIM = 64
A_HEADS = 6
B_HEADS = 6
A_WIDTH = A_HEADS * HEAD_DIM
B_WIDTH = B_HEADS * HEAD_DIM
C_WIDTH = 256
IDX_HEADS = 8
IDX_DIM = 64
TOPK_MAX = 256
CONV_W = 3
PEER_HEADS = 8
PEER_DKEY = 128
PEER_TOPK = 16
LN_EPS = 1e-5
DEPTH = 2
DN_ALPHA = (2 * DEPTH) ** 0.25
IN_WIDTHS = (A_WIDTH, A_WIDTH, A_WIDTH, A_HEADS,
             B_WIDTH, B_WIDTH, B_WIDTH, IDX_HEADS * IDX_DIM, IDX_DIM, IDX_HEADS,
             C_WIDTH, C_WIDTH, C_WIDTH)

LANES = 128
MXU_DEPTH = 256
BF16_ROWS = 16
VMEM_LIMIT = 56 * 1024 * 1024
NEG = -1e30
INT_MIN = -2 ** 31

MISC_LF = IDX_DIM
MISC_WI = IDX_DIM + A_HEADS

O_QA, O_KA, O_VA = 0, 384, 768
O_QB, O_KB, O_VB = 1152, 1536, 1920
O_QI = 2304
O_CIN, O_GB, O_GC = 3328, 3584, 3840
O_MISC = 4096
W_TOTAL = 4224


def _nt(a, b):
    return lax.dot_general(a, b, (((1,), (1,)), ((), ())), preferred_element_type=F32)


def _cparams(sem):
    return pltpu.CompilerParams(dimension_semantics=sem, vmem_limit_bytes=VMEM_LIMIT)


def _sort_key(x):
    bits = pltpu.bitcast(x, I32)
    return bits ^ ((bits >> 31) & 0x7FFFFFFF)


def _layer_norm_rows(z, g, b):
    mu = jnp.mean(z, axis=-1, keepdims=True)
    zc = z - mu
    var = jnp.mean(zc * zc, axis=-1, keepdims=True)
    return zc * lax.rsqrt(var + LN_EPS) * g + b


def _proj_kernel(x_ref, w_ref, bf_ref, qa_ref, ka_ref, va_ref, kab_ref, vab_ref,
                 qb_ref, kb_ref, vb_ref, kbb_ref, vbb_ref, qi_ref, u_ref, gb_ref,
                 misc_ref, miscb_ref, *maybe_t_refs, transposed):
    xb = x_ref[...].astype(BF16)

    def seg(off, width):
        return jnp.dot(xb, w_ref[:, off:off + width], preferred_element_type=F32)

    def put(ref, val):
        if transposed:
            ref[0] = val.T
        else:
            ref[...] = val

    qa_ref[...] = (seg(O_QA, A_WIDTH) * (HEAD_DIM ** -0.5)).astype(BF16)
    ka = seg(O_KA, A_WIDTH)
    put(ka_ref, ka)
    kab_ref[...] = ka.astype(BF16)
    va = seg(O_VA, A_WIDTH)
    put(va_ref, va)
    vab_ref[...] = va.astype(BF16)
    qb_ref[...] = (seg(O_QB, B_WIDTH) * (HEAD_DIM ** -0.5)).astype(BF16)
    kb = seg(O_KB, B_WIDTH)
    put(kb_ref, kb)
    kbb_ref[...] = kb.astype(BF16)
    vb = seg(O_VB, B_WIDTH)
    put(vb_ref, vb)
    vbb_ref[...] = vb.astype(BF16)
    qi_ref[...] = (seg(O_QI, IDX_HEADS * LANES) * (IDX_DIM ** -0.5)).astype(BF16)
    cin = seg(O_CIN, C_WIDTH)
    gb_ref[...] = seg(O_GB, C_WIDTH)
    gc = seg(O_GC, C_WIDTH)
    u_ref[...] = gc * cin
    misc = seg(O_MISC, LANES)
    lane = lax.broadcasted_iota(I32, misc.shape, 1)
    z = misc + bf_ref[...]
    logsig = jnp.minimum(z, 0.0) - jnp.log1p(jnp.exp(-jnp.abs(z)))
    misc = jnp.where((lane >= MISC_LF) & (lane < MISC_WI), logsig, misc)
    misc_ref[...] = misc
    miscb_ref[...] = misc.astype(BF16)
    if transposed:
        misct_ref, kabt_ref, kbbt_ref, miscbt_ref = maybe_t_refs
        misc_t = misc.T
        misct_ref[0] = misc_t
        miscbt_ref[0] = misc_t.astype(BF16)
        kabt_ref[0] = ka.T.astype(BF16)
        kbbt_ref[0] = kb.T.astype(BF16)


def _project(x, wcat, bfrow, tm, seq=None):
    n = x.shape[0]
    transposed = seq is not None
    row = lambda w: pl.BlockSpec((tm, w), lambda i: (i, 0))
    widths = [(A_WIDTH, BF16), (A_WIDTH, F32), (A_WIDTH, F32), (A_WIDTH, BF16), (A_WIDTH, BF16),
              (B_WIDTH, BF16), (B_WIDTH, F32), (B_WIDTH, F32), (B_WIDTH, BF16), (B_WIDTH, BF16),
              (IDX_HEADS * LANES, BF16), (C_WIDTH, F32), (C_WIDTH, F32), (LANES, F32), (LANES, BF16)]
    names = ["qa", "ka", "va", "kab", "vab", "qb", "kb", "vb", "kbb", "vbb", "qi", "u", "gb", "misc", "miscb"]
    out_specs = [row(w) for w, _ in widths]
    out_shape = [jax.ShapeDtypeStruct((n, w), dt) for w, dt in widths]
    if transposed:
        per = seq // tm
        tspec = lambda w: pl.BlockSpec((1, w, tm), lambda i: (i // per, 0, i % per))
        tshape = lambda w, dt=F32: jax.ShapeDtypeStruct((n // seq, w, seq), dt)
        for name in ("ka", "va", "kb", "vb"):
            k = names.index(name)
            out_specs[k], out_shape[k] = tspec(widths[k][0]), tshape(widths[k][0])
        for name, w, dt in (("misct", LANES, F32), ("kabt", A_WIDTH, BF16), ("kbbt", B_WIDTH, BF16),
                            ("miscbt", LANES, BF16)):
            names.append(name)
            out_specs.append(tspec(w))
            out_shape.append(tshape(w, dt))
    outs = pl.pallas_call(
        functools.partial(_proj_kernel, transposed=transposed),
        grid=(n // tm,),
        in_specs=[row(D_MODEL),
                  pl.BlockSpec((D_MODEL, W_TOTAL), lambda i: (0, 0)),
                  pl.BlockSpec((1, LANES), lambda i: (0, 0))],
        out_specs=out_specs,
        out_shape=out_shape,
        compiler_params=_cparams(("parallel",)),
        name="proj",
    )(x, wcat, bfrow)
    return dict(zip(names, outs))


def _wt_kernel(w_ref, o_ref):
    o_ref[...] = w_ref[...].T.astype(BF16)


def _build_wcat(w_in_t, b_f):
    parts, off = [], 0
    for w in IN_WIDTHS:
        parts.append(w_in_t[off:off + w, :])
        off += w
    qa, ka, va, fa, qb, kb, vb, qi, ki, wi, cin, gb, gc = parts
    d = w_in_t.shape[1]
    qi_pad = jnp.pad(qi.reshape(IDX_HEADS, IDX_DIM, d), ((0, 0), (0, LANES - IDX_DIM), (0, 0)))
    qi_pad = qi_pad.reshape(IDX_HEADS * LANES, d)
    misc = jnp.concatenate([ki, fa, wi, jnp.zeros((LANES - MISC_WI - IDX_HEADS, d), w_in_t.dtype)], axis=0)
    wcat_t = jnp.concatenate([qa, ka, va, qb, kb, vb, qi_pad, cin, gb, gc, misc], axis=0)
    wcat = pl.pallas_call(
        _wt_kernel,
        grid=(W_TOTAL // LANES,),
        in_specs=[pl.BlockSpec((LANES, d), lambda i: (i, 0))],
        out_specs=pl.BlockSpec((d, LANES), lambda i: (0, i)),
        out_shape=jax.ShapeDtypeStruct((d, W_TOTAL), BF16),
        compiler_params=_cparams(("parallel",)),
        name="w_in_transpose",
    )(wcat_t)
    bfrow = jnp.zeros((1, LANES), F32).at[0, MISC_LF:MISC_WI].set(b_f.astype(F32))
    return wcat, bfrow


def _cumsum_kernel(m_ref, cum_ref, cumt_ref, carry_ref, *, tc):
    @pl.when(pl.program_id(1) == 0)
    def _():
        carry_ref[...] = jnp.zeros_like(carry_ref)

    v = m_ref[0]
    r = lax.broadcasted_iota(I32, (tc, tc), 0)
    c = lax.broadcasted_iota(I32, (tc, tc), 1)
    tri = (c <= r).astype(F32)
    loc = jnp.dot(tri, v, precision=lax.Precision.HIGHEST, preferred_element_type=F32) + carry_ref[...]
    cum_ref[0] = loc
    cumt_ref[0] = loc.T
    carry_ref[...] = loc[tc - 1:tc, :]


def _cumsum(misc3, tc):
    b, s, _ = misc3.shape
    return pl.pallas_call(
        functools.partial(_cumsum_kernel, tc=tc),
        grid=(b, s // tc),
        in_specs=[pl.BlockSpec((1, tc, LANES), lambda i, j: (i, j, 0))],
        out_specs=[pl.BlockSpec((1, tc, LANES), lambda i, j: (i, j, 0)),
                   pl.BlockSpec((1, LANES, tc), lambda i, j: (i, 0, j))],
        out_shape=[jax.ShapeDtypeStruct((b, s, LANES), F32), jax.ShapeDtypeStruct((b, LANES, s), F32)],
        scratch_shapes=[pltpu.VMEM((1, LANES), F32)],
        compiler_params=_cparams(("parallel", "arbitrary")),
        name="logf_cumsum",
    )(misc3)


def _flash_update(carry, s, vblk, v_transposed=False):
    m, l, acc = carry
    m_new = jnp.maximum(m, jnp.max(s, axis=1, keepdims=True))
    alpha = jnp.exp(m - m_new)
    p = jnp.exp(s - m_new)
    l = alpha * l + jnp.sum(p, axis=1, keepdims=True)
    p16 = p.astype(BF16)
    pv = _nt(p16, vblk) if v_transposed else jnp.dot(p16, vblk, preferred_element_type=F32)
    return m_new, l, alpha * acc + pv


def _flash_init(rows, width):
    return (jnp.full((rows, 1), NEG, F32), jnp.zeros((rows, 1), F32), jnp.zeros((rows, width), F32))


def _split_pair(qp):
    lane = lax.broadcasted_iota(I32, qp.shape, 1)
    zero = jnp.zeros_like(qp)
    return jnp.where(lane < HEAD_DIM, qp, zero), jnp.where(lane >= HEAD_DIM, qp, zero)


def _join_pair(o0, o1):
    lane = lax.broadcasted_iota(I32, o0.shape, 1)
    return jnp.where(lane < HEAD_DIM, o0, o1)


def _pipelined_blocks(nblk, store_qk, step, state):
    store_qk(0, 0)

    def pair(jj, st):
        store_qk(1, 2 * jj + 1)
        st = step(2 * jj, st, 0, False)
        store_qk(0, 2 * jj + 2)
        return step(2 * jj + 1, st, 1, False)

    npair = (nblk - 1) // 2
    st = lax.fori_loop(0, npair, pair, state)

    def one_left(st):
        return step(nblk - 1, st, 0, True)

    def two_left(st):
        store_qk(1, nblk - 1)
        st = step(nblk - 2, st, 0, False)
        return step(nblk - 1, st, 1, True)

    return lax.cond(nblk - 2 * npair == 2, two_left, one_left, st)


def _fox_kernel(q_ref, kt_ref, v_ref, cum_ref, cumt_ref, o_ref, qk_s, *, tq, tk):
    p = pl.program_id(1)
    i = pl.program_id(2)
    qs = _split_pair(q_ref[0])
    lane = lax.broadcasted_iota(I32, (tq, LANES), 1)
    cumblk = cum_ref[0]
    cqs = [jnp.sum(jnp.where(lane == MISC_LF + 2 * p + hh, cumblk, 0.0), axis=1, keepdims=True)
           for hh in range(2)]
    row_g = i * tq + lax.broadcasted_iota(I32, (tq, tk), 0)
    col_l = lax.broadcasted_iota(I32, (tq, tk), 1)

    def store_qk(slot, j):
        off = pl.multiple_of(j * tk, tk)
        ktblk = kt_ref[0, :, pl.ds(off, tk)]
        for hh in range(2):
            qk_s[slot, hh] = jnp.dot(qs[hh], ktblk, preferred_element_type=F32)

    def step(j, state, slot, last):
        off = pl.multiple_of(j * tk, tk)
        vblk = v_ref[0, pl.ds(off, tk), :]
        new = []
        for hh in range(2):
            ck = cumt_ref[0, pl.ds(2 * p + hh, 1), pl.ds(off, tk)]
            s = qk_s[slot, hh] + (cqs[hh] - ck)
            if last:
                s = jnp.where(off + col_l <= row_g, s, NEG)
            new.append(_flash_update(state[hh], s, vblk))
        return tuple(new)

    nblk = (i * tq) // tk + 1
    init = (_flash_init(tq, LANES), _flash_init(tq, LANES))
    (_, l0, acc0), (_, l1, acc1) = _pipelined_blocks(nblk, store_qk, step, init)
    o_ref[0] = _join_pair(acc0 / l0, acc1 / l1).astype(BF16)


def _fox_prompt(qa, kab, vab, cum, cumt, tq, tk):
    b, s, _ = qa.shape
    npairs = A_HEADS // 2
    return pl.pallas_call(
        functools.partial(_fox_kernel, tq=tq, tk=tk),
        grid=(b, npairs, s // tq),
        in_specs=[pl.BlockSpec((1, tq, LANES), lambda bb, p, i: (bb, i, p)),
                  pl.BlockSpec((1, LANES, s), lambda bb, p, i: (bb, p, 0)),
                  pl.BlockSpec((1, s, LANES), lambda bb, p, i: (bb, 0, p)),
                  pl.BlockSpec((1, tq, LANES), lambda bb, p, i: (bb, i, 0)),
                  pl.BlockSpec((1, 8, s), lambda bb, p, i: (bb, MISC_LF // 8, 0))],
        out_specs=pl.BlockSpec((1, tq, LANES), lambda bb, p, i: (bb, i, p)),
        out_shape=jax.ShapeDtypeStruct((b, s, A_WIDTH), BF16),
        scratch_shapes=[pltpu.VMEM((2, 2, tq, tk), F32)],
        compiler_params=_cparams(("parallel", "parallel", "arbitrary")),
        name="fox_prompt",
    )(qa, kab, vab, cum, cumt)


def _kth_threshold(count_ge, rows, k, total):
    def cond(state):
        it, _, cnt = state
        return (it < 32) & (jnp.max(cnt) > k)

    def body(state):
        it, t, cnt = state
        cand = t | lax.shift_left(jnp.int32(1), 31 - it)
        c = count_ge(cand ^ INT_MIN)
        ok = c >= k
        return it + 1, jnp.where(ok, cand, t), jnp.where(ok, c, cnt)

    init = (jnp.int32(0), jnp.zeros((rows, 1), I32), jnp.full((rows, 1), total, I32))
    _, t, cnt = lax.while_loop(cond, body, init)
    return t ^ INT_MIN, cnt


def _tie_cutoff(count_tie_below, rows, need, nbits):
    def body(it, j):
        cand = j | lax.shift_left(jnp.int32(1), nbits - 1 - it)
        cnt = count_tie_below(cand)
        return jnp.where(cnt <= need, cand, j)
    return lax.fori_loop(0, nbits, body, jnp.zeros((rows, 1), I32))


def _dsa_kernel(qi_ref, misc_ref, kit_ref, qb_ref, kbt_ref, vb_ref, o_ref, key_ref, bias_ref, qk_s,
                khi_ref, klo_ref, *, tq, kc, kcc, topk, nbits):
    i = pl.program_id(1)
    nch = (i * tq + tq + kc - 1) // kc
    ncc = (i * tq + tq + kcc - 1) // kcc
    nch1 = ncc * (kcc // kc)
    w8 = misc_ref[0][:, MISC_WI:MISC_WI + IDX_HEADS] * (IDX_HEADS ** -0.5)
    row_g = i * tq + lax.broadcasted_iota(I32, (tq, kc), 0)
    col_l = lax.broadcasted_iota(I32, (tq, kc), 1)
    col_c = lax.broadcasted_iota(I32, (tq, kcc), 1)

    def p1(c, carry):
        off = pl.multiple_of(c * kc, kc)
        kit = kit_ref[0, :, pl.ds(off, kc)]
        acc = jnp.zeros((tq, kc), F32)
        for h in range(IDX_HEADS):
            r = jnp.dot(qi_ref[0, :, h * LANES:(h + 1) * LANES], kit, preferred_element_type=F32)
            acc = acc + w8[:, h:h + 1] * jnp.maximum(r, 0.0)
        acc = jnp.where(acc == 0.0, 0.0, acc)
        sc = jnp.where(off + col_l <= row_g, acc, -jnp.inf)
        key = _sort_key(sc)
        key_ref[:, pl.ds(off, kc)] = key
        khi_ref[:, pl.ds(off, kc)] = (key >> 16).astype(I16)
        return carry

    lax.fori_loop(0, nch1, p1, 0)

    def lane_groups(x):
        tot = x[:, 0:LANES]
        for g in range(1, kcc // LANES):
            tot = tot + x[:, g * LANES:(g + 1) * LANES]
        return tot

    def count(pred):
        def body(c, cnt):
            off = pl.multiple_of(c * kcc, kcc)
            return cnt + lane_groups(pred(key_ref[:, pl.ds(off, kcc)], off).astype(I32))
        cnt = lax.fori_loop(0, ncc, body, jnp.zeros((tq, LANES), I32))
        return jnp.sum(cnt, axis=1, keepdims=True)

    def count16(ref16, pred):
        def body(c, cnt):
            off = pl.multiple_of(c * kcc, kcc)
            hit = jnp.where(pred(ref16[:, pl.ds(off, kcc)]), jnp.int16(1), jnp.int16(0))
            return cnt + lane_groups(hit)
        cnt = lax.fori_loop(0, ncc, body, jnp.zeros((tq, LANES), I16))
        return jnp.sum(cnt.astype(I32), axis=1, keepdims=True)

    def bit_step(t, cnt, it, count_ge16, base):
        cand = t | lax.shift_left(jnp.int32(1), 15 - it)
        c = base + count_ge16((cand - HALF16).astype(I16))
        ok = c >= topk
        return jnp.where(ok, cand, t), jnp.where(ok, c, cnt)

    zero = jnp.zeros((tq, 1), I32)
    t_hi, n_hi_ge = lax.fori_loop(
        0, 16,
        lambda it, st: bit_step(st[0], st[1], it, lambda th: count16(khi_ref, lambda kk: kk >= th), 0),
        (zero, jnp.full((tq, 1), ncc * kcc, I32)))
    th_s = t_hi - HALF16

    def low_half():
        th16 = th_s.astype(I16)
        n_hi_gt = count16(khi_ref, lambda kk: kk > th16)

        def build(c, carry):
            off = pl.multiple_of(c * kcc, kcc)
            lo = ((key_ref[:, pl.ds(off, kcc)] & 0xFFFF) - HALF16).astype(I16)
            klo_ref[:, pl.ds(off, kcc)] = jnp.where(khi_ref[:, pl.ds(off, kcc)] == th16, lo, jnp.int16(-HALF16))
            return carry

        lax.fori_loop(0, ncc, build, 0)

        def cond(st):
            it, _, cnt = st
            return (it < 16) & (jnp.max(cnt) > topk)

        def body(st):
            it, t, cnt = st
            t, cnt = bit_step(t, cnt, it, lambda th: count16(klo_ref, lambda kk: kk >= th), n_hi_gt)
            return it + 1, t, cnt

        _, t_lo, cnt = lax.while_loop(cond, body, (jnp.int32(0), zero, n_hi_ge))
        return t_lo, cnt

    t_lo, n_ge = lax.cond(jnp.max(n_hi_ge) > topk, low_half, lambda: (zero, n_hi_ge))
    thr = lax.shift_left(th_s, 16) | t_lo
    big = jnp.full((tq, 1), 2 ** nbits - 1, I32)

    def tie_cutoff():
        need = topk - count(lambda kk, off: kk > thr)
        return _tie_cutoff(lambda j: count(lambda kk, off: (kk == thr) & (off + col_c < j)), tq, need, nbits)

    jstar = lax.cond(jnp.max(n_ge) > topk, tie_cutoff, lambda: big)

    def p2(c, carry):
        off = pl.multiple_of(c * kc, kc)
        kk = key_ref[:, pl.ds(off, kc)]
        colg = off + col_l
        sel = (kk > thr) | ((kk == thr) & (colg < jstar))
        bias_ref[:, pl.ds(off, kc)] = jnp.where(sel & (colg <= row_g), 0.0, NEG)
        return carry

    lax.fori_loop(0, nch, p2, 0)

    for pr in range(B_HEADS // 2):
        qs = _split_pair(qb_ref[0, :, pr * LANES:(pr + 1) * LANES])

        def store_qk(slot, c):
            off = pl.multiple_of(c * kc, kc)
            ktblk = kbt_ref[0, pr * LANES:(pr + 1) * LANES, pl.ds(off, kc)]
            for hh in range(2):
                qk_s[slot, hh] = jnp.dot(qs[hh], ktblk, preferred_element_type=F32)

        def step(c, state, slot, last):
            off = pl.multiple_of(c * kc, kc)
            vblk = vb_ref[0, pl.ds(off, kc), pr * LANES:(pr + 1) * LANES]
            bias = bias_ref[:, pl.ds(off, kc)]
            return tuple(_flash_update(state[hh], qk_s[slot, hh] + bias, vblk) for hh in range(2))

        init = (_flash_init(tq, LANES), _flash_init(tq, LANES))
        (_, l0, acc0), (_, l1, acc1) = _pipelined_blocks(nch, store_qk, step, init)
        o_ref[0, :, pr * LANES:(pr + 1) * LANES] = _join_pair(acc0 / l0, acc1 / l1).astype(BF16)


def _dsa_prompt(qi, misc, miscb, qb, kbb, vbb, tq, kc, topk):
    b, s, _ = qb.shape
    nbits = max(1, (s - 1).bit_length()) + 1
    kcc = 2 * kc if s % (2 * kc) == 0 else kc
    qspec = lambda w: pl.BlockSpec((1, tq, w), lambda bb, i: (bb, i, 0))
    full = lambda w: pl.BlockSpec((1, s, w), lambda bb, i: (bb, 0, 0))
    full_t = lambda w: pl.BlockSpec((1, w, s), lambda bb, i: (bb, 0, 0))
    return pl.pallas_call(
        functools.partial(_dsa_kernel, tq=tq, kc=kc, kcc=kcc, topk=topk, nbits=nbits),
        grid=(b, s // tq),
        in_specs=[qspec(IDX_HEADS * LANES), qspec(LANES), full_t(LANES), qspec(B_WIDTH), full_t(B_WIDTH),
                  full(B_WIDTH)],
        out_specs=qspec(B_WIDTH),
        out_shape=jax.ShapeDtypeStruct((b, s, B_WIDTH), BF16),
        scratch_shapes=[pltpu.VMEM((tq, s), I32), pltpu.VMEM((tq, s), F32), pltpu.VMEM((2, 2, tq, kc), F32),
                        pltpu.VMEM((tq, s), I16), pltpu.VMEM((tq, s), I16)],
        compiler_params=_cparams(("parallel", "arbitrary")),
        name="dsa_prompt",
    )(qi, misc, miscb, qb, kbb, vbb)


def _sidx_kernel(pt_ref, qi_ref, misc_ref, *rest, pps, nps, past, topk, nbits):
    page_refs = rest[:pps]
    knew_ref, bias_ref, key_ref, qst_ref, wst_ref = rest[pps:]
    j = pl.program_id(1)
    t = qi_ref.shape[1]
    lp = key_ref.shape[1]

    @pl.when(j == 0)
    def _():
        qf = qi_ref[0].astype(F32)
        qst_ref[...] = jnp.concatenate([qf[:, h * LANES:h * LANES + IDX_DIM] for h in range(IDX_HEADS)],
                                       axis=0).astype(BF16)
        w8 = misc_ref[0][:, MISC_WI:MISC_WI + IDX_HEADS] * (IDX_HEADS ** -0.5)
        wst_ref[...] = jnp.concatenate([w8[:, h:h + 1] for h in range(IDX_HEADS)], axis=0)

    def score(kt):
        r = jnp.dot(qst_ref[...], kt.astype(BF16), preferred_element_type=F32)
        r = wst_ref[...] * jnp.maximum(r, 0.0)
        acc = r[0:t, :]
        for h in range(1, IDX_HEADS):
            acc = acc + r[h * t:(h + 1) * t, :]
        return jnp.where(acc == 0.0, 0.0, acc)

    @pl.when(j < nps)
    def _():
        kt = jnp.concatenate([page_refs[k][0, 0] for k in range(pps)], axis=1)
        off = pl.multiple_of(j * (pps * PAGE_SIZE), pps * PAGE_SIZE)
        key_ref[:, pl.ds(off, pps * PAGE_SIZE)] = _sort_key(score(kt))

    @pl.when(j == nps)
    def _():
        rowi = lax.broadcasted_iota(I32, (t, PAGE_SIZE), 0)
        coli = lax.broadcasted_iota(I32, (t, PAGE_SIZE), 1)
        sc = jnp.where(coli <= rowi, score(knew_ref[0]), -jnp.inf)
        key_ref[:, past:past + PAGE_SIZE] = _sort_key(sc)

        keys = key_ref[...]
        colg = lax.broadcasted_iota(I32, (t, lp), 1)
        count = lambda pred: jnp.sum(pred.astype(I32), axis=1, keepdims=True)
        thr, n_ge = _kth_threshold(lambda th: count(keys >= th), t, topk, lp)

        def tie_cutoff():
            need = topk - count(keys > thr)
            return _tie_cutoff(lambda jj: count((keys == thr) & (colg < jj)), t, need, nbits)

        jstar = lax.cond(jnp.max(n_ge) > topk, tie_cutoff, lambda: jnp.full((t, 1), 2 ** nbits - 1, I32))
        sel = (keys > thr) | ((keys == thr) & (colg < jstar))
        rowg = past + lax.broadcasted_iota(I32, (t, lp), 0)
        bias_ref[0] = jnp.where(sel & (colg <= rowg), 0.0, NEG)


def _sample_index(page_table, qi3, misc3, cache_idx_t, layer, knew_t, pps, topk):
    nb, t, _ = qi3.shape
    npg = page_table.shape[1]
    nps = npg // pps
    past = npg * PAGE_SIZE
    lp = past + PAGE_SIZE
    nbits = lp.bit_length() + 1

    def page_map(k):
        return lambda b, j, pt: (layer, pt[b * npg + jnp.minimum(j, nps - 1) * pps + k], 0, 0)

    bmap = lambda b, j, pt: (b, 0, 0)
    grid_spec = pltpu.PrefetchScalarGridSpec(
        num_scalar_prefetch=1,
        grid=(nb, nps + 1),
        in_specs=[pl.BlockSpec((1, t, IDX_HEADS * LANES), bmap),
                  pl.BlockSpec((1, t, LANES), bmap)]
                 + [pl.BlockSpec((1, 1, IDX_DIM, PAGE_SIZE), page_map(k)) for k in range(pps)]
                 + [pl.BlockSpec((1, IDX_DIM, PAGE_SIZE), bmap)],
        out_specs=pl.BlockSpec((1, t, lp), bmap),
        scratch_shapes=[pltpu.VMEM((t, lp), I32), pltpu.VMEM((IDX_HEADS * t, IDX_DIM), BF16),
                        pltpu.VMEM((IDX_HEADS * t, 1), F32)],
    )
    return pl.pallas_call(
        functools.partial(_sidx_kernel, pps=pps, nps=nps, past=past, topk=topk, nbits=nbits),
        grid_spec=grid_spec,
        out_shape=jax.ShapeDtypeStruct((nb, t, lp), F32),
        compiler_params=_cparams(("parallel", "arbitrary")),
        name="sample_index",
    )(page_table.reshape(-1), qi3, misc3, *([cache_idx_t] * pps), knew_t)


def _sattn_kernel(pt_ref, qa_ref, qb_ref, bias_ref, *rest, pps, nps):
    grp = lambda g: rest[g * pps:(g + 1) * pps]
    ak_refs, av_refs, lf_refs, bk_refs, bv_refs = (grp(g) for g in range(5))
    (akn_ref, avn_ref, lfn_ref, bkn_ref, bvn_ref, oa_ref, ob_ref,
     qbd_a, qbd_b, ma, la, acca, mb, lb, accb, carry_ref) = rest[5 * pps:]
    j = pl.program_id(1)
    t = qa_ref.shape[1]
    rows = A_HEADS * t
    lane_q = lax.broadcasted_iota(I32, (t, A_WIDTH), 1)

    def block_diag(q):
        qf = q.astype(F32)
        parts = [jnp.where((lane_q >= h * HEAD_DIM) & (lane_q < (h + 1) * HEAD_DIM), qf, 0.0)
                 for h in range(A_HEADS)]
        return jnp.concatenate(parts, axis=0).astype(BF16)

    @pl.when(j == 0)
    def _():
        qbd_a[...] = block_diag(qa_ref[0])
        qbd_b[...] = block_diag(qb_ref[0])
        for r in (ma, mb):
            r[...] = jnp.full(r.shape, NEG, F32)
        for r in (la, lb, acca, accb, carry_ref):
            r[...] = jnp.zeros(r.shape, F32)

    r_i = lax.broadcasted_iota(I32, (PAGE_SIZE, PAGE_SIZE), 0)
    c_i = lax.broadcasted_iota(I32, (PAGE_SIZE, PAGE_SIZE), 1)
    triu = (r_i <= c_i).astype(F32)

    def cat(refs, dtype):
        pages = [r[0, 0] if len(r.shape) == 4 else r[0] for r in refs]
        x = pages[0] if len(pages) == 1 else jnp.concatenate(pages, axis=1)
        return x.astype(dtype)

    def update(m_ref, l_ref, acc_ref, s, vt):
        m, l, acc = _flash_update((m_ref[...], l_ref[...], acc_ref[...]), s, vt, v_transposed=True)
        m_ref[...] = m
        l_ref[...] = l
        acc_ref[...] = acc

    def process(ak, av, lf, bk, bv, bias8, mask):
        run = carry_ref[...]
        cls = []
        for r in lf:
            page = r[0, 0] if len(r.shape) == 4 else r[0]
            cl = jnp.dot(page, triu, precision=lax.Precision.HIGHEST, preferred_element_type=F32) + run
            run = jnp.broadcast_to(cl[:, PAGE_SIZE - 1:PAGE_SIZE], cl.shape)
            cls.append(cl)
        carry_ref[...] = run
        ck = cls[0] if len(cls) == 1 else jnp.concatenate(cls, axis=1)
        width = ck.shape[1]
        ck_rows = jnp.concatenate([jnp.broadcast_to(ck[h:h + 1, :], (t, width)) for h in range(A_HEADS)], axis=0)
        sa = jnp.dot(qbd_a[...], cat(ak, BF16), preferred_element_type=F32) - ck_rows
        if mask is not None:
            sa = jnp.where(mask, sa, NEG)
        update(ma, la, acca, sa, cat(av, BF16))
        sb = jnp.dot(qbd_b[...], cat(bk, BF16), preferred_element_type=F32)
        sb = sb + jnp.concatenate([bias8] * B_HEADS, axis=0)
        update(mb, lb, accb, sb, cat(bv, BF16))

    @pl.when(j < nps)
    def _():
        off = pl.multiple_of(j * (pps * PAGE_SIZE), pps * PAGE_SIZE)
        process(ak_refs, av_refs, lf_refs, bk_refs, bv_refs, bias_ref[0, :, pl.ds(off, pps * PAGE_SIZE)], None)

    @pl.when(j == nps)
    def _():
        past = nps * pps * PAGE_SIZE
        rowi = lax.broadcasted_iota(I32, (rows, PAGE_SIZE), 0)
        coli = lax.broadcasted_iota(I32, (rows, PAGE_SIZE), 1)
        mask = coli <= (rowi & (t - 1))
        process([akn_ref], [avn_ref], [lfn_ref], [bkn_ref], [bvn_ref],
                bias_ref[0, :, past:past + PAGE_SIZE], mask)

        def gather_heads(acc_ref, l_ref):
            o = jnp.zeros((t, A_WIDTH), F32)
            for h in range(A_HEADS):
                blk = acc_ref[h * t:(h + 1) * t, :] / l_ref[h * t:(h + 1) * t, :]
                o = jnp.where((lane_q >= h * HEAD_DIM) & (lane_q < (h + 1) * HEAD_DIM), blk, o)
            return o.astype(BF16)

        oa_ref[0] = gather_heads(acca, la)
        ob_ref[0] = gather_heads(accb, lb)


def _sample_attn(page_table, qa3, qb3, bias, caches, layer, news, pps):
    nb, t, _ = qa3.shape
    assert t & (t - 1) == 0
    npg = page_table.shape[1]
    nps = npg // pps
    lp = bias.shape[2]
    rows = A_HEADS * t

    def page_map(k):
        return lambda b, j, pt: (layer, pt[b * npg + jnp.minimum(j, nps - 1) * pps + k], 0, 0)

    bmap = lambda b, j, pt: (b, 0, 0)
    cache_specs, cache_args = [], []
    for c in caches:
        blk = (1, 1) + c.shape[2:]
        for k in range(pps):
            cache_specs.append(pl.BlockSpec(blk, page_map(k)))
            cache_args.append(c)
    new_specs = [pl.BlockSpec((1,) + a.shape[1:], bmap) for a in news]
    grid_spec = pltpu.PrefetchScalarGridSpec(
        num_scalar_prefetch=1,
        grid=(nb, nps + 1),
        in_specs=[pl.BlockSpec((1, t, A_WIDTH), bmap), pl.BlockSpec((1, t, B_WIDTH), bmap),
                  pl.BlockSpec((1, t, lp), bmap)] + cache_specs + new_specs,
        out_specs=[pl.BlockSpec((1, t, A_WIDTH), bmap), pl.BlockSpec((1, t, B_WIDTH), bmap)],
        scratch_shapes=[pltpu.VMEM((rows, A_WIDTH), BF16), pltpu.VMEM((rows, B_WIDTH), BF16),
                        pltpu.VMEM((rows, 1), F32), pltpu.VMEM((rows, 1), F32), pltpu.VMEM((rows, A_WIDTH), F32),
                        pltpu.VMEM((rows, 1), F32), pltpu.VMEM((rows, 1), F32), pltpu.VMEM((rows, B_WIDTH), F32),
                        pltpu.VMEM((8, PAGE_SIZE), F32)],
    )
    return pl.pallas_call(
        functools.partial(_sattn_kernel, pps=pps, nps=nps),
        grid_spec=grid_spec,
        out_shape=[jax.ShapeDtypeStruct((nb, t, A_WIDTH), BF16), jax.ShapeDtypeStruct((nb, t, B_WIDTH), BF16)],
        compiler_params=_cparams(("parallel", "arbitrary")),
        name="sample_attn",
    )(page_table.reshape(-1), qa3, qb3, bias, *cache_args, *news)


def _merge_kernel(oa_ref, ob_ref, u_ref, gb_ref, x_ref, buf_ref, cw_ref, wo_ref, g_ref, b_ref,
                  y_ref, nbuf_ref, carry_ref, *, tm):
    j = pl.program_id(1)

    @pl.when(j == 0)
    def _():
        carry_ref[...] = buf_ref[0]

    u = u_ref[0]
    rowi = lax.broadcasted_iota(I32, u.shape, 0)
    c0 = carry_ref[0:1, :]
    c1 = carry_ref[1:2, :]
    u1 = jnp.where(rowi == 0, c1, pltpu.roll(u, 1, 0))
    u2 = jnp.where(rowi == 0, c0, jnp.where(rowi == 1, c1, pltpu.roll(u, 2, 0)))
    yc = cw_ref[0:1, :] * u2 + cw_ref[1:2, :] * u1 + cw_ref[2:3, :] * u
    oc = (gb_ref[0] * yc).astype(BF16)
    new_carry = u[tm - 2:tm, :]
    carry_ref[...] = new_carry
    nbuf_ref[0] = new_carry
    mix = jnp.dot(oa_ref[0], wo_ref[0:A_WIDTH, :], preferred_element_type=F32)
    mix = mix + jnp.dot(ob_ref[0], wo_ref[A_WIDTH:A_WIDTH + B_WIDTH, :], preferred_element_type=F32)
    mix = mix + jnp.dot(oc, wo_ref[A_WIDTH + B_WIDTH:, :], preferred_element_type=F32)
    y_ref[0] = _layer_norm_rows(DN_ALPHA * x_ref[0] + mix, g_ref[...], b_ref[...])


def _merge(oa, ob, u, gb, x, buf, conv_w, wo_bf, g, b, tm):
    nseq, t, _ = x.shape
    blk = lambda w: pl.BlockSpec((1, tm, w), lambda s, j: (s, j, 0))
    const = lambda r, w: pl.BlockSpec((r, w), lambda s, j: (0, 0))
    return pl.pallas_call(
        functools.partial(_merge_kernel, tm=tm),
        grid=(nseq, t // tm),
        in_specs=[blk(A_WIDTH), blk(B_WIDTH), blk(C_WIDTH), blk(C_WIDTH), blk(D_MODEL),
                  pl.BlockSpec((1, CONV_W - 1, C_WIDTH), lambda s, j: (s, 0, 0)),
                  const(CONV_W, C_WIDTH), const(D_MODEL, D_MODEL), const(1, D_MODEL), const(1, D_MODEL)],
        out_specs=[blk(D_MODEL), pl.BlockSpec((1, CONV_W - 1, C_WIDTH), lambda s, j: (s, 0, 0))],
        out_shape=[jax.ShapeDtypeStruct((nseq, t, D_MODEL), F32),
                   jax.ShapeDtypeStruct((nseq, CONV_W - 1, C_WIDTH), F32)],
        scratch_shapes=[pltpu.VMEM((CONV_W - 1, C_WIDTH), F32)],
        compiler_params=_cparams(("parallel", "arbitrary")),
        name="merge_ln",
    )(oa, ob, u, gb, x, buf, conv_w, wo_bf, g, b)


def _top16_rows(s_list):
    nk, tm = s_list[0].shape

    def extract(s, first_occurrence):
        idx = lax.broadcasted_iota(I32, (nk, tm), 0).astype(F32)
        work = s
        rank = jnp.full((nk, tm), 99.0, F32)
        vals = []
        for r in range(PEER_TOPK):
            m = jnp.max(work, axis=0, keepdims=True)
            hit = work == m
            if first_occurrence:
                hit = idx == jnp.min(jnp.where(hit, idx, 1e9), axis=0, keepdims=True)
            rank = jnp.where(hit, float(r), rank)
            work = jnp.where(hit, -jnp.inf, work)
            vals.append(m)
        return jnp.concatenate(vals, axis=0), rank

    fast = [extract(s, False) for s in s_list]
    bad = jnp.zeros((1, tm), F32)
    for _, rank in fast:
        nsel = jnp.sum(jnp.where(rank < 99.0, 1.0, 0.0), axis=0, keepdims=True)
        bad = jnp.maximum(bad, jnp.where(nsel != float(PEER_TOPK), 1.0, 0.0))
    flat = lambda pairs: tuple(x for pair in pairs for x in pair)
    out = lax.cond(jnp.max(bad) > 0.0, lambda: flat([extract(s, True) for s in s_list]), lambda: flat(fast))
    return [(out[2 * i], out[2 * i + 1]) for i in range(len(s_list))]


def _peer_kernel(x_ref, wqt_ref, sk_ref, u_ref, vt_ref, g_ref, b_ref, o_ref,
                 xt_s, qt_s, outt_s, a_s, n1_s, bb_s, r2_s, *, tm, ec, nk):
    e = pl.program_id(1)
    ne = pl.num_programs(1)
    k = PEER_TOPK
    half = PEER_DKEY // 2

    @pl.when(e == 0)
    def _():
        xt = x_ref[...].T.astype(BF16)
        xt_s[...] = xt
        qt_s[...] = jnp.dot(wqt_ref[...], xt, preferred_element_type=F32).astype(BF16)
        rho = lax.broadcasted_iota(I32, (80, tm), 0)
        mid = rho - 16
        r1 = jnp.where(rho < 16, rho, jnp.where(rho < 72, mid & 7, 0))
        r2 = jnp.where(rho < 16, 0, jnp.where(rho < 72, (mid >> 3) + 1, rho - 64))
        pos = (r1 * k + r2).astype(F32)
        valid = (r1 + 1) * (r2 + 1) <= k
        row8 = lax.broadcasted_iota(I32, (8, tm), 0)
        def head(h, carry):
            q1 = qt_s[pl.ds(pl.multiple_of(2 * h * half, half), half), :]
            q2 = qt_s[pl.ds(pl.multiple_of((2 * h + 1) * half, half), half), :]
            s1 = jnp.dot(sk_ref[h, 0], q1, preferred_element_type=F32)
            s2 = jnp.dot(sk_ref[h, 1], q2, preferred_element_type=F32)
            (v1, rank1), (v2, rank2) = _top16_rows([s1, s2])
            slabs = [v1 + v2[0:1, :]]
            for j in range(1, 8):
                slabs.append(v1[0:8, :] + v2[j:j + 1, :])
            slabs.append(v1[0:1, :] + v2[8:16, :])
            cand = jnp.where(valid, jnp.concatenate(slabs, axis=0), -jnp.inf)
            cmax = v1[0:1, :] + v2[0:1, :]
            sel = jnp.zeros((80, tm), F32)
            z = jnp.zeros((1, tm), F32)
            for r in range(k):
                m = jnp.max(cand, axis=0, keepdims=True)
                first = jnp.min(jnp.where(cand == m, pos, 1e9), axis=0, keepdims=True)
                hit = pos == first
                sel = jnp.where(hit, 1.0, sel)
                cand = jnp.where(hit, -jnp.inf, cand)
                z = z + jnp.exp(m - cmax)
            top8 = sel[0:8, :]
            for j in range(1, 8):
                top8 = top8 + sel[8 + 8 * j:16 + 8 * j, :]
            extra = jnp.sum(sel[72:80, :], axis=0, keepdims=True)
            top8 = top8 + jnp.where(row8 == 0, extra, 0.0)
            ncount = jnp.concatenate([top8, sel[8:16, :]], axis=0)
            n1 = jnp.zeros((nk, tm), F32)
            for r in range(k):
                n1 = jnp.where(rank1 == float(r), ncount[r:r + 1, :], n1)
            a_s[h] = jnp.exp(s1 - v1[0:1, :])
            n1_s[h] = n1
            bb_s[h] = (jnp.exp(s2 - v2[0:1, :]) / z).astype(BF16)
            r2_s[h] = rank2.astype(BF16)
            return carry

        lax.fori_loop(0, PEER_HEADS, head, 0)
        outt_s[...] = jnp.zeros_like(outt_s)

    per = ec // nk
    gps = max(1, min(per, MXU_DEPTH // nk))
    sl = gps * nk
    xt = xt_s[...]
    acc = None
    nsl = ec // sl

    def rows16(row):
        if nk % BF16_ROWS:
            return jnp.broadcast_to(row, (nk, tm)).astype(BF16)
        tile = jnp.broadcast_to(row, (BF16_ROWS, tm)).astype(BF16)
        return jnp.concatenate([tile] * (nk // BF16_ROWS), axis=0) if nk > BF16_ROWS else tile

    hidden = lambda sb: jnp.dot(u_ref[sb * sl:(sb + 1) * sl, :], xt, preferred_element_type=F32)
    ht_next = hidden(0)
    for sb in range(nsl):
        ht = ht_next
        if sb + 1 < nsl:
            ht_next = hidden(sb + 1)
        act = (0.5 * ht * (1.0 + lax.erf(ht * (0.5 ** 0.5)))).astype(BF16)
        coefs = []
        for g in range(gps):
            i1 = e * per + sb * gps + g
            gate = jnp.zeros((nk, tm), BF16)
            for h in range(PEER_HEADS):
                arow = rows16(a_s[h, pl.ds(i1, 1), :])
                nrow = rows16(n1_s[h, pl.ds(i1, 1), :])
                gate = gate + jnp.where(r2_s[h] < nrow, bb_s[h], jnp.zeros((), BF16)) * arow
            coefs.append(gate * act[g * nk:(g + 1) * nk, :])
        coef = jnp.concatenate(coefs, axis=0) if gps > 1 else coefs[0]
        part = jnp.dot(vt_ref[:, sb * sl:(sb + 1) * sl], coef, preferred_element_type=F32)
        acc = part if acc is None else acc + part
    outt_s[...] += acc

    @pl.when(e == ne - 1)
    def _():
        y = outt_s[...].T
        o_ref[...] = _layer_norm_rows(DN_ALPHA * x_ref[...] + y, g_ref[...], b_ref[...])


def _peer(x, wqt, sk, u_bf, vt_bf, g, b, tm, ec):
    n = x.shape[0]
    nk = sk.shape[2]
    ne = (nk * nk) // ec
    return pl.pallas_call(
        functools.partial(_peer_kernel, tm=tm, ec=ec, nk=nk),
        grid=(n // tm, ne),
        in_specs=[pl.BlockSpec((tm, D_MODEL), lambda i, e: (i, 0)),
                  pl.BlockSpec((D_MODEL, D_MODEL), lambda i, e: (0, 0)),
                  pl.BlockSpec(sk.shape, lambda i, e: (0, 0, 0, 0)),
                  pl.BlockSpec((ec, D_MODEL), lambda i, e: (e, 0)),
                  pl.BlockSpec((D_MODEL, ec), lambda i, e: (0, e)),
                  pl.BlockSpec((1, D_MODEL), lambda i, e: (0, 0)),
                  pl.BlockSpec((1, D_MODEL), lambda i, e: (0, 0))],
        out_specs=pl.BlockSpec((tm, D_MODEL), lambda i, e: (i, 0)),
        out_shape=jax.ShapeDtypeStruct((n, D_MODEL), F32),
        scratch_shapes=[pltpu.VMEM((D_MODEL, tm), BF16), pltpu.VMEM((PEER_HEADS * PEER_DKEY, tm), BF16),
                        pltpu.VMEM((D_MODEL, tm), F32)]
                       + [pltpu.VMEM((PEER_HEADS, nk, tm), F32)] * 2
                       + [pltpu.VMEM((PEER_HEADS, nk, tm), BF16)] * 2,
        compiler_params=_cparams(("parallel", "arbitrary")),
        name="peer_ln",
    )(x, wqt, sk, u_bf, vt_bf, g, b)


def _pick(n, pref):
    t = min(n, pref)
    while n % t:
        t //= 2
    return t


def kernel(x_prompt, x_sample, cache_a_k, cache_a_v, cache_a_logf, cache_b_k, cache_b_v, cache_b_idx_k, state_conv, page_table, w_in, b_fgate, conv_w, w_o, ln1_g, ln1_b, peer_wq, peer_subkeys, peer_u, peer_v, ln2_g, ln2_b):
    bp, s, d = x_prompt.shape
    nb, t, _ = x_sample.shape
    depth = w_in.shape[0]
    npool = cache_a_k.shape[1]
    npg = page_table.shape[1]
    past = npg * PAGE_SIZE
    nk = peer_subkeys.shape[3]
    topk_p = min(TOPK_MAX, s // 4)
    topk_s = min(TOPK_MAX, (past + t) // 4)

    n_p, n_s = bp * s, nb * t
    tm_p = _pick(n_p, 512)
    tm_s = _pick(n_s, 256)
    tq_fox = _pick(s, 128)
    tk_fox = _pick(s, 512)
    tq_dsa = _pick(s, 128)
    kc_dsa = _pick(s, 512)
    tc = _pick(s, 256)
    tm_merge = _pick(s, 512)
    tm_peer_p = _pick(n_p, 256)
    tm_peer_s = _pick(n_s, 256)
    ec = _pick(nk * nk, max(4096, nk))
    pps_i = _pick(npg, 16)
    pps_a = _pick(npg, 16)

    page_t = lambda c: jnp.transpose(c, (0, 1, 3, 4, 2)).reshape(depth, npool, -1, PAGE_SIZE)
    cak, cav, cbk, cbv = page_t(cache_a_k), page_t(cache_a_v), page_t(cache_b_k), page_t(cache_b_v)
    cidx = jnp.swapaxes(cache_b_idx_k, 2, 3)
    clf = jnp.pad(jnp.swapaxes(cache_a_logf, 2, 3), ((0, 0), (0, 0), (0, 8 - A_HEADS), (0, 0)))

    w_in_t = jnp.transpose(w_in, (2, 0, 1))
    xp = x_prompt.reshape(n_p, d)
    xs = x_sample.reshape(n_s, d)
    outs_p = [[] for _ in range(7)]
    outs_s = [[] for _ in range(7)]
    zero_buf = jnp.zeros((bp, CONV_W - 1, C_WIDTH), F32)
    row2 = lambda v: v.reshape(1, -1).astype(F32)

    for l in range(depth):
        wcat, bfrow = _build_wcat(w_in_t[:, l, :], b_fgate[l])
        wo_bf = w_o[l].astype(BF16)
        wqt = peer_wq[l].T.astype(BF16)
        sk = peer_subkeys[l].astype(BF16)
        u_bf = peer_u[l].astype(BF16)
        vt_bf = peer_v[l].T.astype(BF16)
        g1, b1, g2, b2 = row2(ln1_g[l]), row2(ln1_b[l]), row2(ln2_g[l]), row2(ln2_b[l])
        cw = conv_w[l].astype(F32)

        pr = _project(xp, wcat, bfrow, tm_p, seq=s)
        r3 = lambda a: a.reshape(bp, s, a.shape[-1])
        cum, cumt = _cumsum(r3(pr["misc"]), tc)
        oa = _fox_prompt(r3(pr["qa"]), pr["kabt"], r3(pr["vab"]), cum, cumt, tq_fox, tk_fox)
        ob = _dsa_prompt(r3(pr["qi"]), r3(pr["misc"]), pr["miscbt"], r3(pr["qb"]), pr["kbbt"],
                         r3(pr["vbb"]), tq_dsa, kc_dsa, topk_p)
        x1, buf_p = _merge(oa, ob, r3(pr["u"]), r3(pr["gb"]), r3(xp), zero_buf, cw, wo_bf, g1, b1, tm_merge)
        xp = _peer(x1.reshape(n_p, d), wqt, sk, u_bf, vt_bf, g2, b2, tm_peer_p, ec)
        heads_t = lambda a: a.reshape(bp, A_HEADS, HEAD_DIM, s)
        for lst, val in zip(outs_p, (heads_t(pr["ka"]), heads_t(pr["va"]), pr["misct"][:, MISC_LF:MISC_WI, :],
                                     heads_t(pr["kb"]), heads_t(pr["vb"]), pr["misct"][:, :IDX_DIM, :], buf_p)):
            lst.append(val)

        ps = _project(xs, wcat, bfrow, tm_s)
        q3 = lambda a: a.reshape(nb, t, a.shape[-1])
        new_t = lambda a: jnp.pad(jnp.swapaxes(q3(a), 1, 2), ((0, 0), (0, 0), (0, PAGE_SIZE - t)))
        ki_new = new_t(ps["misc"][:, :IDX_DIM])
        lf_new = jnp.pad(new_t(ps["misc"][:, MISC_LF:MISC_WI]), ((0, 0), (0, 8 - A_HEADS), (0, 0)))
        bias = _sample_index(page_table, q3(ps["qi"]), q3(ps["misc"]), cidx, l, ki_new, pps_i, topk_s)
        oa_s, ob_s = _sample_attn(page_table, q3(ps["qa"]), q3(ps["qb"]), bias, (cak, cav, clf, cbk, cbv), l,
                                  (new_t(ps["ka"]), new_t(ps["va"]), lf_new, new_t(ps["kb"]), new_t(ps["vb"])),
                                  pps_a)
        x1s, buf_s = _merge(oa_s, ob_s, q3(ps["u"]), q3(ps["gb"]), q3(xs), state_conv[l].astype(F32), cw, wo_bf,
                            g1, b1, t)
        xs = _peer(x1s.reshape(n_s, d), wqt, sk, u_bf, vt_bf, g2, b2, tm_peer_s, ec)
        for lst, val in zip(outs_s, (ps["ka"].reshape(nb, t, A_HEADS, HEAD_DIM), ps["va"].reshape(nb, t, A_HEADS, HEAD_DIM),
                                     ps["misc"][:, MISC_LF:MISC_WI].reshape(nb, t, A_HEADS),
                                     ps["kb"].reshape(nb, t, B_HEADS, HEAD_DIM), ps["vb"].reshape(nb, t, B_HEADS, HEAD_DIM),
                                     ps["misc"][:, :IDX_DIM].reshape(nb, t, IDX_DIM), buf_s)):
            lst.append(val)

    pos_perm = {5: (0, 1, 4, 2, 3), 4: (0, 1, 3, 2)}
    prompt_out = [jnp.stack(v) for v in outs_p]
    prompt_out = [jnp.transpose(a, pos_perm[a.ndim]) for a in prompt_out[:6]] + prompt_out[6:]
    return ((xp.reshape(bp, s, d), xs.reshape(nb, t, d))
            + tuple(prompt_out) + tuple(jnp.stack(v) for v in outs_s))
```
